```python
import jax, jax.numpy as jnp
from jax import lax
import numpy as np

D_MODEL = 1024
BATCH = 1
SEQ = 16384
DEPTH = 1
DEC_BATCH = 32
DEC_SEQ = 64
PAST_LEN = 2048

CHUNK = 64
RET_HEADS = 4
RET_DK = 128
RET_DV = 128
ATTN_HEADS = 8
ATTN_DH = 64
BAND_PAST_CHUNKS = 8
BAND_PAST = BAND_PAST_CHUNKS * CHUNK
REL_CLIP = 256
D_FF = -(-8 * D_MODEL // (3 * 256)) * 256
PLE_DIM = 256
ROPE_THETA = 10000.0
EPS = 1e-6
COL_SIZES = (RET_HEADS * RET_DK, RET_HEADS * RET_DK, RET_HEADS * RET_DV, RET_HEADS * RET_DV,
             ATTN_HEADS * ATTN_DH, ATTN_HEADS * ATTN_DH, ATTN_HEADS * ATTN_DH)
IN_COLS = sum(COL_SIZES)

kernel_name = "hybrid_retention_chunkband_stream_step"


def rmsnorm(x, g):
    xf = x.astype(jnp.float32)
    y = xf * lax.rsqrt(jnp.mean(xf * xf, axis=-1, keepdims=True) + EPS) * g.astype(jnp.float32)
    return y.astype(x.dtype)


def rope(x, pos):
    half = x.shape[-1] // 2
    inv = ROPE_THETA ** (-jnp.arange(half, dtype=jnp.float32) / half)
    ang = pos[:, None] * inv[None, :]
    cos = jnp.cos(ang)[None, :, None, :]
    sin = jnp.sin(ang)[None, :, None, :]
    x1, x2 = x[..., :half], x[..., half:]
    return jnp.concatenate([x1 * cos - x2 * sin, x1 * sin + x2 * cos], axis=-1).astype(x.dtype)


def retention_log_decay():
    return jnp.log1p(-jnp.exp2(-5.0 - jnp.arange(RET_HEADS, dtype=jnp.float32)))


def retention_chunk(R, q, k, v, lg):
    T = q.shape[1]
    n = jnp.arange(T, dtype=jnp.float32)
    diff = n[:, None] - n[None, :]
    D = jnp.where(diff[None] >= 0, jnp.exp(jnp.maximum(diff, 0.0)[None] * lg[:, None, None]), 0.0)
    s = jnp.einsum('bnhd,bmhd->bhnm', q, k) * D[None]
    inner = jnp.einsum('bhnm,bmhe->bnhe', s, v)
    xi = jnp.exp((n + 1.0)[:, None] * lg[None, :])
    cross = jnp.einsum('bnhd,bhde->bnhe', q, R) * xi[None, :, :, None]
    zeta = jnp.exp((T - 1.0 - n)[:, None] * lg[None, :])
    R_new = jnp.exp(T * lg)[None, :, None, None] * R + jnp.einsum(
        'bmhd,bmhe->bhde', k * zeta[None, :, :, None], v)
    return R_new, inner + cross


def rel_bias_lookup(rel_bias, qpos, kpos):
    rel = qpos[:, None] - kpos[None, :]
    idx = jnp.clip(rel, -REL_CLIP, REL_CLIP) + REL_CLIP
    return rel_bias[:, idx].astype(jnp.float32)


def band_attention(q, k, v, valid, bias):
    s = jnp.einsum('bnthd,bnlhd->bnhtl', q, k).astype(jnp.float32) * (ATTN_DH ** -0.5) + bias[None, None]
    s = jnp.where(valid[None, :, None, None, :], s, jnp.finfo(jnp.float32).min)
    p = jax.nn.softmax(s, axis=-1)
    return jnp.einsum('bnhtl,bnlhd->bnthd', p.astype(v.dtype), v)


def project(u, w_in, g_q, g_k, pos):
    B, S, _ = u.shape
    z = u @ w_in
    splits = list(np.cumsum(COL_SIZES)[:-1])
    q_r, k_r, v_r, g_r, q_a, k_a, v_a = jnp.split(z, splits, axis=-1)
    q_r = rope(q_r.reshape(B, S, RET_HEADS, RET_DK), pos)
    k_r = rope(k_r.reshape(B, S, RET_HEADS, RET_DK), pos) * (RET_DK ** -0.5)
    v_r = v_r.reshape(B, S, RET_HEADS, RET_DV)
    q_a = rmsnorm(q_a.reshape(B, S, ATTN_HEADS, ATTN_DH), g_q)
    k_a = rmsnorm(k_a.reshape(B, S, ATTN_HEADS, ATTN_DH), g_k)
    v_a = v_a.reshape(B, S, ATTN_HEADS, ATTN_DH)
    return q_r, k_r, v_r, g_r, q_a, k_a, v_a


def merge_and_tail(x, p, o_r, g_r, o_a, g_ret_out, w_out, g_ffn, w_ffn_gate, w_ffn_up, w_ffn_down,
                   g_ple, w_ple_gate, w_ple_proj):
    B, S, _ = x.shape
    ret = jax.nn.silu(g_r) * rmsnorm(o_r, g_ret_out).reshape(B, S, RET_HEADS * RET_DV)
    mixed = jnp.concatenate([ret.astype(x.dtype), o_a.astype(x.dtype)], axis=-1)
    h = x + mixed @ w_out
    u = rmsnorm(h, g_ffn)
    h = h + (jax.nn.silu(u @ w_ffn_gate) * (u @ w_ffn_up)) @ w_ffn_down
    gate = jax.nn.sigmoid(rmsnorm(h, g_ple) @ w_ple_gate)
    return h + (p @ w_ple_proj) * gate


def mixers_prompt(u, w_in, g_q, g_k, rel_bias):
    B, S, _ = u.shape
    nC = S // CHUNK
    pos = jnp.arange(S, dtype=jnp.float32)
    q_r, k_r, v_r, g_r, q_a, k_a, v_a = project(u, w_in, g_q, g_k, pos)
    lg = retention_log_decay()

    def to_chunks(t):
        return jnp.moveaxis(t.reshape(B, nC, CHUNK, *t.shape[2:]), 1, 0)

    R0 = jnp.zeros((B, RET_HEADS, RET_DK, RET_DV), jnp.float32)
    R_fin, o_chunks = lax.scan(lambda R, xs: retention_chunk(R, xs[0], xs[1], xs[2], lg), R0,
                               (to_chunks(q_r), to_chunks(k_r), to_chunks(v_r)))
    o_r = jnp.moveaxis(o_chunks, 0, 1).reshape(B, S, RET_HEADS, RET_DV)

    pad = ((0, 0), (BAND_PAST_CHUNKS, 0), (0, 0), (0, 0), (0, 0))
    kp = jnp.pad(k_a.reshape(B, nC, CHUNK, ATTN_HEADS, ATTN_DH), pad)
    vp = jnp.pad(v_a.reshape(B, nC, CHUNK, ATTN_HEADS, ATTN_DH), pad)
    idx = jnp.arange(nC)[:, None] + jnp.arange(BAND_PAST_CHUNKS + 1)[None, :]
    L = (BAND_PAST_CHUNKS + 1) * CHUNK
    kb = kp[:, idx].reshape(B, nC, L, ATTN_HEADS, ATTN_DH)
    vb = vp[:, idx].reshape(B, nC, L, ATTN_HEADS, ATTN_DH)
    valid = jnp.repeat((idx - BAND_PAST_CHUNKS) >= 0, CHUNK, axis=1)
    bias = rel_bias_lookup(rel_bias, BAND_PAST + jnp.arange(CHUNK), jnp.arange(L))
    o_a = band_attention(q_a.reshape(B, nC, CHUNK, ATTN_HEADS, ATTN_DH), kb, vb, valid, bias)
    o_a = o_a.reshape(B, S, ATTN_HEADS * ATTN_DH)
    tail = min(BAND_PAST, S)
    return o_r, g_r, o_a, R_fin, k_a[:, S - tail:], v_a[:, S - tail:]


def mixers_sample(u, R0, ck, cv, w_in, g_q, g_k, rel_bias):
    B, T, _ = u.shape
    pos = PAST_LEN + jnp.arange(T, dtype=jnp.float32)
    q_r, k_r, v_r, g_r, q_a, k_a, v_a = project(u, w_in, g_q, g_k, pos)
    R_new, o_r = retention_chunk(R0.astype(jnp.float32), q_r, k_r, v_r, retention_log_decay())
    W = ck.shape[1]
    kb = jnp.concatenate([ck.astype(k_a.dtype), k_a], axis=1)[:, None]
    vb = jnp.concatenate([cv.astype(v_a.dtype), v_a], axis=1)[:, None]
    valid = jnp.ones((1, W + T), dtype=bool)
    bias = rel_bias_lookup(rel_bias, jnp.arange(T), jnp.arange(W + T) - W)
    o_a = band_attention(q_a[:, None], kb, vb, valid, bias)[:, 0].reshape(B, T, ATTN_HEADS * ATTN_DH)
    return o_r, g_r, o_a, R_new, k_a, v_a


def setup_inputs(seed: int = 0) -> dict:
    key = jax.random.key(seed)
    ks = jax.random.split(key, 24)
    nrm = jax.random.normal
    W = min(BAND_PAST, PAST_LEN)
    f = jnp.float32
    return {
        "x_prompt": nrm(ks[0], (BATCH, SEQ, D_MODEL), f),
        "x_sample": nrm(ks[1], (DEC_BATCH, DEC_SEQ, D_MODEL), f),
        "cache_attn_k": nrm(ks[2], (DEPTH, DEC_BATCH, W, ATTN_HEADS, ATTN_DH), f),
        "cache_attn_v": nrm(ks[3], (DEPTH, DEC_BATCH, W, ATTN_HEADS, ATTN_DH), f),
        "state_ret": 0.1 * nrm(ks[4], (DEPTH, DEC_BATCH, RET_HEADS, RET_DK, RET_DV), f),
        "p_prompt": nrm(ks[5], (DEPTH, BATCH, SEQ, PLE_DIM), f),
        "p_sample": nrm(ks[6], (DEPTH, DEC_BATCH, DEC_SEQ, PLE_DIM), f),
        "g_mix": 1.0 + 0.05 * nrm(ks[7], (DEPTH, D_MODEL), f),
        "w_in": nrm(ks[8], (DEPTH, D_MODEL, IN_COLS), f) * D_MODEL ** -0.5,
        "g_ret_out": 1.0 + 0.05 * nrm(ks[9], (DEPTH, RET_HEADS, RET_DV), f),
        "g_q_attn": 1.0 + 0.05 * nrm(ks[10], (DEPTH, ATTN_DH), f),
        "g_k_attn": 1.0 + 0.05 * nrm(ks[11], (DEPTH, ATTN_DH), f),
        "rel_bias": 0.1 * nrm(ks[12], (DEPTH, ATTN_HEADS, 2 * REL_CLIP + 1), f),
        "w_out": nrm(ks[13], (DEPTH, D_MODEL, D_MODEL), f) * D_MODEL ** -0.5,
        "g_ffn": 1.0 + 0.05 * nrm(ks[14], (DEPTH, D_MODEL), f),
        "w_ffn_gate": nrm(ks[15], (DEPTH, D_MODEL, D_FF), f) * D_MODEL ** -0.5,
        "w_ffn_up": nrm(ks[16], (DEPTH, D_MODEL, D_FF), f) * D_MODEL ** -0.5,
        "w_ffn_down": nrm(ks[17], (DEPTH, D_FF, D_MODEL), f) * D_FF ** -0.5,
        "g_ple": 1.0 + 0.05 * nrm(ks[18], (DEPTH, D_MODEL), f),
        "w_ple_gate": nrm(ks[19], (DEPTH, D_MODEL, D_MODEL), f) * D_MODEL ** -0.5,
        "w_ple_proj": nrm(ks[20], (DEPTH, PLE_DIM, D_MODEL), f) * PLE_DIM ** -0.5,
    }


def reference(x_prompt, x_sample, cache_attn_k, cache_attn_v, state_ret, p_prompt, p_sample,
              g_mix, w_in, g_ret_out, g_q_attn, g_k_attn, rel_bias, w_out, g_ffn,
              w_ffn_gate, w_ffn_up, w_ffn_down, g_ple, w_ple_gate, w_ple_proj):
    hp, hs = x_prompt, x_sample
    ret_p, kp_l, vp_l, ret_s, ks_l, vs_l = [], [], [], [], [], []
    for i in range(DEPTH):
        tail_w = (g_ret_out[i], w_out[i], g_ffn[i], w_ffn_gate[i], w_ffn_up[i], w_ffn_down[i],
                  g_ple[i], w_ple_gate[i], w_ple_proj[i])
        o_r, g_r, o_a, R_fin, k_tail, v_tail = mixers_prompt(
            rmsnorm(hp, g_mix[i]), w_in[i], g_q_attn[i], g_k_attn[i], rel_bias[i])
        hp = merge_and_tail(hp, p_prompt[i], o_r, g_r, o_a, *tail_w)
        o_r_s, g_r_s, o_a_s, R_new, k_new, v_new = mixers_sample(
            rmsnorm(hs, g_mix[i]), state_ret[i], cache_attn_k[i], cache_attn_v[i],
            w_in[i], g_q_attn[i], g_k_attn[i], rel_bias[i])
        hs = merge_and_tail(hs, p_sample[i], o_r_s, g_r_s, o_a_s, *tail_w)
        ret_p.append(R_fin); kp_l.append(k_tail); vp_l.append(v_tail)
        ret_s.append(R_new); ks_l.append(k_new); vs_l.append(v_new)
    new_ret_prompt = jnp.stack(ret_p)
    new_k_prompt = jnp.stack(kp_l)
    new_v_prompt = jnp.stack(vp_l)
    new_ret_sample = jnp.stack(ret_s)
    new_k_sample = jnp.stack(ks_l)
    new_v_sample = jnp.stack(vs_l)
    return (hp, hs, new_ret_prompt, new_k_prompt, new_v_prompt, new_ret_sample, new_k_sample, new_v_sample)
```

```python
import functools
import math

import jax
import jax.numpy as jnp
from jax import lax
from jax.experimental import pallas as pl
from jax.experimental.pallas import tpu as pltpu

F32 = jnp.float32
BF16 = jnp.bfloat16

D_MODEL = 1024
CHUNK = 64
PAST_LEN = 2048
RET_HEADS = 4
RET_DK = 128
RET_DV = 128
ATTN_HEADS = 8
ATTN_DH = 64
BAND_PAST_CHUNKS = 8
BAND_PAST = BAND_PAST_CHUNKS * CHUNK
REL_CLIP = 256
D_FF = 2816
PLE_DIM = 256
ROPE_THETA = 10000.0
EPS = 1e-6
RET_W = RET_HEADS * RET_DK
ATT_W = ATTN_HEADS * ATTN_DH
IN_COLS = 4 * RET_W + 3 * ATT_W
LOG_DECAY = tuple(math.log1p(-(2.0 ** (-5 - h))) for h in range(RET_HEADS))
MASKED = -1e30

LANES = 128
BIAS_LANES = 1024
TM_DENSE = 512
TQ_PROMPT = 256
VMEM_LIMIT = 56 * 1024 * 1024

_NT = (((1,), (1,)), ((), ()))
_TN = (((0,), (0,)), ((), ()))


def _rms(x, g):
    return x * lax.rsqrt(jnp.mean(x * x, axis=-1, keepdims=True) + EPS) * g


def _const_spec(shape):
    zeros = (0,) * len(shape)
    return pl.BlockSpec(shape, lambda i: zeros, pipeline_mode=pl.Buffered(1))


def _proj_kernel(x_ref, gmix_ref, w_ref, cos_ref, sin_ref, gq_ref, gk_ref,
                 qr_ref, kr_ref, vr_ref, sg_ref, qa_ref, ka_ref, va_ref, ka32_ref, va32_ref):
    tm = x_ref.shape[0]
    xn = _rms(x_ref[...], gmix_ref[...]).astype(BF16)
    cos = cos_ref[...]
    sin = sin_ref[...]

    def proj(group):
        return jnp.dot(xn, w_ref[:, group * RET_W:(group + 1) * RET_W], preferred_element_type=F32)

    def rope_store(z, out_ref, scale):
        for h in range(RET_HEADS):
            sl = slice(h * RET_DK, (h + 1) * RET_DK)
            zh = z[:, sl]
            r = zh * cos + pltpu.roll(zh, RET_DK // 2, 1) * sin
            if scale is not None:
                r = r * scale
            out_ref[:, sl] = r.astype(out_ref.dtype)

    lane = lax.broadcasted_iota(jnp.int32, (tm, LANES), 1)
    first_head = lane < ATTN_DH

    def head_rms(z, g_ref):
        outs = []
        for j in range(ATT_W // LANES):
            sl = slice(j * LANES, (j + 1) * LANES)
            zj = z[:, sl]
            zz = zj * zj
            s0 = jnp.sum(jnp.where(first_head, zz, 0.0), axis=-1, keepdims=True)
            s1 = jnp.sum(jnp.where(first_head, 0.0, zz), axis=-1, keepdims=True)
            ms = jnp.where(first_head, s0, s1) * (1.0 / ATTN_DH)
            outs.append(zj * lax.rsqrt(ms + EPS) * g_ref[:, sl])
        return jnp.concatenate(outs, axis=1)

    rope_store(proj(0), qr_ref, None)
    rope_store(proj(1), kr_ref, RET_DK ** -0.5)
    vr_ref[...] = proj(2).astype(BF16)
    sg_ref[...] = jax.nn.silu(proj(3)).astype(BF16)
    qa_ref[...] = (head_rms(proj(4), gq_ref) * (ATTN_DH ** -0.5)).astype(BF16)
    ka = head_rms(proj(5), gk_ref)
    ka32_ref[...] = ka
    ka_ref[...] = ka.astype(BF16)
    va = proj(6)
    va32_ref[...] = va
    va_ref[...] = va.astype(BF16)


def _proj_call(x, g_mix, w_in, cos, sin, gq, gk):
    n = x.shape[0]
    tm = TM_DENSE
    row = lambda w: pl.BlockSpec((tm, w), lambda i: (i, 0))
    bf = jax.ShapeDtypeStruct((n, RET_W), BF16)
    f32 = jax.ShapeDtypeStruct((n, ATT_W), F32)
    return pl.pallas_call(
        _proj_kernel,
        grid=(n // tm,),
        in_specs=[row(D_MODEL), _const_spec((1, D_MODEL)), _const_spec((D_MODEL, IN_COLS)),
                  row(LANES), row(LANES), _const_spec((1, ATT_W)), _const_spec((1, ATT_W))],
        out_specs=[row(RET_W)] * 9,
        out_shape=[bf] * 7 + [f32] * 2,
        compiler_params=pltpu.CompilerParams(dimension_semantics=("arbitrary",),
                                             vmem_limit_bytes=VMEM_LIMIT),
        name="proj",
    )(x, g_mix, w_in, cos, sin, gq, gk)


def _init_decay(dmat_scr, xi_scr, zeta_scr, t):
    n = lax.broadcasted_iota(jnp.int32, (t, t), 0)
    m = lax.broadcasted_iota(jnp.int32, (t, t), 1)
    diff = (n - m).astype(F32)
    pos = lax.broadcasted_iota(jnp.int32, (t, RET_DK), 0).astype(F32)
    for h in range(RET_HEADS):
        lg = LOG_DECAY[h]
        dmat_scr[h] = jnp.where(diff >= 0, jnp.exp(jnp.maximum(diff, 0.0) * lg), 0.0)
        sl = slice(h * RET_DK, (h + 1) * RET_DK)
        xi_scr[:, sl] = jnp.exp((pos + 1.0) * lg)
        zeta_scr[:, sl] = jnp.exp((t - 1.0 - pos) * lg)


def _init_bias(vb_ref, bias_scr, t, width, band):
    q = lax.broadcasted_iota(jnp.int32, (t, BIAS_LANES), 0)
    if band:
        qc = lax.broadcasted_iota(jnp.int32, (t, width), 0) // CHUNK
        kc = lax.broadcasted_iota(jnp.int32, (t, width), 1) // CHUNK
        in_band = jnp.logical_and(kc >= qc, kc <= qc + BAND_PAST_CHUNKS)
    for h in range(ATTN_HEADS):
        x = jnp.broadcast_to(vb_ref[h:h + 1, :], (t, BIAS_LANES))
        x = pltpu.roll(x, BIAS_LANES - REL_CLIP, 1)
        shift = 1
        while shift < t:
            x = jnp.where((q & shift) != 0, pltpu.roll(x, shift, 1), x)
            shift *= 2
        b = x[:, :width]
        if band:
            b = jnp.where(in_band, b, MASKED)
        bias_scr[h] = b


def _retention(q, k, v, r, dmat, xi, zeta, decay_t):
    s = lax.dot_general(q, k, _NT, preferred_element_type=F32) * dmat
    inner = jnp.dot(s.astype(BF16), v, preferred_element_type=F32)
    cross = jnp.dot(q, r.astype(BF16), preferred_element_type=F32) * xi
    kz = (k.astype(F32) * zeta).astype(BF16)
    r_new = r * decay_t + lax.dot_general(kz, v, _TN, preferred_element_type=F32)
    return inner + cross, r_new


def _attend(q, k, v, bias):
    s = lax.dot_general(q, k, _NT, preferred_element_type=F32) + bias
    e = jnp.exp(s - jnp.max(s, axis=-1, keepdims=True))
    denom = jnp.sum(e, axis=-1, keepdims=True)
    return jnp.dot(e.astype(BF16), v, preferred_element_type=F32) / denom


def _mix_prompt_kernel(qr_ref, kr_ref, vr_ref, sg_ref, qa_ref, k0_ref, k1_ref, k2_ref,
                       v0_ref, v1_ref, v2_ref, gro_ref, vb_ref,
                       mixed_ref, rfin_ref,
                       r_scr, bias_scr, dmat_scr, xi_scr, zeta_scr):
    i = pl.program_id(0)
    t = TQ_PROMPT
    width = 3 * t

    @pl.when(i == 0)
    def _init():
        r_scr[...] = jnp.zeros_like(r_scr)
        _init_decay(dmat_scr, xi_scr, zeta_scr, t)
        _init_bias(vb_ref, bias_scr, t, width, band=True)

    for h in range(RET_HEADS):
        sl = slice(h * RET_DK, (h + 1) * RET_DK)
        o, r_new = _retention(qr_ref[:, sl], kr_ref[:, sl], vr_ref[:, sl], r_scr[h],
                              dmat_scr[h], xi_scr[:, sl], zeta_scr[:, sl],
                              math.exp(t * LOG_DECAY[h]))
        r_scr[h] = r_new
        mixed_ref[:, sl] = (sg_ref[:, sl].astype(F32) * _rms(o, gro_ref[:, sl])).astype(BF16)

    blk = lax.broadcasted_iota(jnp.int32, (1, width), 1) // t
    start_mask = jnp.where(blk + i >= 2, 0.0, MASKED)
    outs = []
    for h in range(ATTN_HEADS):
        sl = slice(h * ATTN_DH, (h + 1) * ATTN_DH)
        k = jnp.concatenate([k0_ref[:, sl], k1_ref[:, sl], k2_ref[:, sl]], axis=0)
        v = jnp.concatenate([v0_ref[:, sl], v1_ref[:, sl], v2_ref[:, sl]], axis=0)
        outs.append(_attend(qa_ref[:, sl], k, v, bias_scr[h] + start_mask))
    mixed_ref[:, RET_W:] = jnp.concatenate(outs, axis=1).astype(BF16)

    @pl.when(i == pl.num_programs(0) - 1)
    def _fin():
        rfin_ref[...] = r_scr[...]


def _mix_prompt_call(qr, kr, vr, sg, qa, ka, va, gro, vb):
    n = qr.shape[0]
    t = TQ_PROMPT
    row = pl.BlockSpec((t, RET_W), lambda i: (i, 0))
    back = lambda d: pl.BlockSpec((t, ATT_W), lambda i: (jnp.maximum(i - d, 0), 0))
    state = (RET_HEADS, RET_DK, RET_DV)
    return pl.pallas_call(
        _mix_prompt_kernel,
        grid=(n // t,),
        in_specs=[row] * 5 + [back(2), back(1), back(0)] * 2
                 + [_const_spec((1, RET_W)), _const_spec((ATTN_HEADS, BIAS_LANES))],
        out_specs=[pl.BlockSpec((t, D_MODEL), lambda i: (i, 0)),
                   pl.BlockSpec(state, lambda i: (0, 0, 0))],
        out_shape=[jax.ShapeDtypeStruct((n, D_MODEL), BF16), jax.ShapeDtypeStruct(state, F32)],
        scratch_shapes=[pltpu.VMEM(state, F32),
                        pltpu.VMEM((ATTN_HEADS, t, 3 * t), F32),
                        pltpu.VMEM((RET_HEADS, t, t), F32),
                        pltpu.VMEM((t, RET_W), F32),
                        pltpu.VMEM((t, RET_W), F32)],
        compiler_params=pltpu.CompilerParams(dimension_semantics=("arbitrary",),
                                             vmem_limit_bytes=VMEM_LIMIT),
        name="mix_prompt",
    )(qr, kr, vr, sg, qa, ka, ka, ka, va, va, va, gro, vb)


def _mix_sample_kernel(qr_ref, kr_ref, vr_ref, sg_ref, qa_ref, ka_ref, va_ref, ck_ref, cv_ref,
                       r0_ref, gro_ref, vb_ref,
                       mixed_ref, rnew_ref,
                       bias_scr, dmat_scr, xi_scr, zeta_scr):
    t = qr_ref.shape[0]
    width = ck_ref.shape[1] + t

    @pl.when(pl.program_id(0) == 0)
    def _init():
        _init_decay(dmat_scr, xi_scr, zeta_scr, t)
        _init_bias(vb_ref, bias_scr, t, width, band=False)

    for h in range(RET_HEADS):
        sl = slice(h * RET_DK, (h + 1) * RET_DK)
        o, r_new = _retention(qr_ref[:, sl], kr_ref[:, sl], vr_ref[:, sl], r0_ref[0, h],
                              dmat_scr[h], xi_scr[:, sl], zeta_scr[:, sl],
                              math.exp(t * LOG_DECAY[h]))
        rnew_ref[0, h] = r_new
        mixed_ref[:, sl] = (sg_ref[:, sl].astype(F32) * _rms(o, gro_ref[:, sl])).astype(BF16)

    outs = []
    for h in range(ATTN_HEADS):
        sl = slice(h * ATTN_DH, (h + 1) * ATTN_DH)
        k = jnp.concatenate([ck_ref[0, :, sl].astype(BF16), ka_ref[:, sl]], axis=0)
        v = jnp.concatenate([cv_ref[0, :, sl].astype(BF16), va_ref[:, sl]], axis=0)
        outs.append(_attend(qa_ref[:, sl], k, v, bias_scr[h]))
    mixed_ref[:, RET_W:] = jnp.concatenate(outs, axis=1).astype(BF16)


def _mix_sample_call(qr, kr, vr, sg, qa, ka, va, ck, cv, r0, gro, vb, t):
    n = qr.shape[0]
    nb, w = ck.shape[0], ck.shape[1]
    row = pl.BlockSpec((t, RET_W), lambda b: (b, 0))
    cache = pl.BlockSpec((1, w, ATT_W), lambda b: (b, 0, 0))
    state = pl.BlockSpec((1, RET_HEADS, RET_DK, RET_DV), lambda b: (b, 0, 0, 0))
    return pl.pallas_call(
        _mix_sample_kernel,
        grid=(nb,),
        in_specs=[row] * 7 + [cache, cache, state,
                              _const_spec((1, RET_W)), _const_spec((ATTN_HEADS, BIAS_LANES))],
        out_specs=[pl.BlockSpec((t, D_MODEL), lambda b: (b, 0)), state],
        out_shape=[jax.ShapeDtypeStruct((n, D_MODEL), BF16),
                   jax.ShapeDtypeStruct((nb, RET_HEADS, RET_DK, RET_DV), F32)],
        scratch_shapes=[pltpu.VMEM((ATTN_HEADS, t, w + t), F32),
                        pltpu.VMEM((RET_HEADS, t, t), F32),
                        pltpu.VMEM((t, RET_W), F32),
                        pltpu.VMEM((t, RET_W), F32)],
        compiler_params=pltpu.CompilerParams(dimension_semantics=("arbitrary",),
                                             vmem_limit_bytes=VMEM_LIMIT),
        name="mix_sample",
    )(qr, kr, vr, sg, qa, ka, va, ck, cv, r0, gro, vb)


FF_SPLITS = (0, 1024, 2048, D_FF)


def _tail_kernel(x_ref, mixed_ref, p_ref, wout_ref, gffn_ref, wg_ref, wu_ref, wd_ref,
                 gple_ref, wpg_ref, wpp_ref, out_ref):
    h = x_ref[...] + jnp.dot(mixed_ref[...], wout_ref[...], preferred_element_type=F32)
    u = _rms(h, gffn_ref[...]).astype(BF16)
    ffn = None
    for c0, c1 in zip(FF_SPLITS[:-1], FF_SPLITS[1:]):
        a = jnp.dot(u, wg_ref[:, c0:c1], preferred_element_type=F32)
        b = jnp.dot(u, wu_ref[:, c0:c1], preferred_element_type=F32)
        part = jnp.dot((jax.nn.silu(a) * b).astype(BF16), wd_ref[c0:c1, :],
                       preferred_element_type=F32)
        ffn = part if ffn is None else ffn + part
    h = h + ffn
    gate = jax.nn.sigmoid(jnp.dot(_rms(h, gple_ref[...]).astype(BF16), wpg_ref[...],
                                  preferred_element_type=F32))
    ple = jnp.dot(p_ref[...].astype(BF16), wpp_ref[...], preferred_element_type=F32)
    out_ref[...] = h + ple * gate


def _tail_call(x, mixed, p, w_out, g_ffn, w_gate, w_up, w_down, g_ple, w_pg, w_pp):
    n = x.shape[0]
    tm = TM_DENSE
    row = lambda w: pl.BlockSpec((tm, w), lambda i: (i, 0))
    return pl.pallas_call(
        _tail_kernel,
        grid=(n // tm,),
        in_specs=[row(D_MODEL), row(D_MODEL), row(PLE_DIM),
                  _const_spec((D_MODEL, D_MODEL)), _const_spec((1, D_MODEL)),
                  _const_spec((D_MODEL, D_FF)), _const_spec((D_MODEL, D_FF)),
                  _const_spec((D_FF, D_MODEL)), _const_spec((1, D_MODEL)),
                  _const_spec((D_MODEL, D_MODEL)), _const_spec((PLE_DIM, D_MODEL))],
        out_specs=row(D_MODEL),
        out_shape=jax.ShapeDtypeStruct((n, D_MODEL), F32),
        compiler_params=pltpu.CompilerParams(dimension_semantics=("arbitrary",),
                                             vmem_limit_bytes=VMEM_LIMIT),
        name="tail",
    )(x, mixed, p, w_out, g_ffn, w_gate, w_up, w_down, g_ple, w_pg, w_pp)


def _rope_tables(pos):
    half = RET_DK // 2
    inv = ROPE_THETA ** (-jnp.arange(half, dtype=F32) / half)
    ang = pos[:, None] * inv[None, :]
    cos, sin = jnp.cos(ang), jnp.sin(ang)
    return jnp.concatenate([cos, cos], axis=-1), jnp.concatenate([-sin, sin], axis=-1)


def _bias_rows(rel_bias):
    edge = jnp.broadcast_to(rel_bias[:, 2 * REL_CLIP:], (ATTN_HEADS, BIAS_LANES - 2 * REL_CLIP))
    return jnp.concatenate([edge, rel_bias[:, 2 * REL_CLIP:0:-1]], axis=1)


def kernel(x_prompt, x_sample, cache_attn_k, cache_attn_v, state_ret, p_prompt, p_sample,
           g_mix, w_in, g_ret_out, g_q_attn, g_k_attn, rel_bias, w_out, g_ffn,
           w_ffn_gate, w_ffn_up, w_ffn_down, g_ple, w_ple_gate, w_ple_proj):
    depth = w_in.shape[0]
    assert depth == 1, "single-layer trunk"
    batch, seq, _ = x_prompt.shape
    dec_batch, dec_seq, _ = x_sample.shape
    cache_w = cache_attn_k.shape[2]
    assert batch == 1 and seq % TM_DENSE == 0 and dec_seq == CHUNK and cache_w == BAND_PAST
    tail_w = min(BAND_PAST, seq)

    row = lambda g: g.reshape(1, -1)
    gq = row(jnp.tile(g_q_attn[0], ATTN_HEADS))
    gk = row(jnp.tile(g_k_attn[0], ATTN_HEADS))
    gro = row(g_ret_out[0])
    vb = _bias_rows(rel_bias[0])
    w_in_b = w_in[0].astype(BF16)
    tail_w_args = (w_out[0].astype(BF16), row(g_ffn[0]), w_ffn_gate[0].astype(BF16),
                   w_ffn_up[0].astype(BF16), w_ffn_down[0].astype(BF16), row(g_ple[0]),
                   w_ple_gate[0].astype(BF16), w_ple_proj[0].astype(BF16))

    xp = x_prompt.reshape(seq, D_MODEL)
    cos_p, sin_p = _rope_tables(jnp.arange(seq, dtype=F32))
    qr, kr, vr, sg, qa, ka, va, ka32, va32 = _proj_call(xp, row(g_mix[0]), w_in_b, cos_p, sin_p, gq, gk)
    mixed_p, r_fin = _mix_prompt_call(qr, kr, vr, sg, qa, ka, va, gro, vb)
    y_prompt = _tail_call(xp, mixed_p, p_prompt[0].reshape(seq, PLE_DIM), *tail_w_args)

    n_s = dec_batch * dec_seq
    xs = x_sample.reshape(n_s, D_MODEL)
    pos_s = PAST_LEN + jnp.tile(jnp.arange(dec_seq, dtype=F32), dec_batch)
    cos_s, sin_s = _rope_tables(pos_s)
    qr, kr, vr, sg, qa, ka, va, ka32_s, va32_s = _proj_call(xs, row(g_mix[0]), w_in_b, cos_s, sin_s, gq, gk)
    mixed_s, r_new = _mix_sample_call(
        qr, kr, vr, sg, qa, ka, va,
        cache_attn_k[0].reshape(dec_batch, cache_w, ATT_W),
        cache_attn_v[0].reshape(dec_batch, cache_w, ATT_W),
        state_ret[0], gro, vb, dec_seq)
    y_sample = _tail_call(xs, mixed_s, p_sample[0].reshape(n_s, PLE_DIM), *tail_w_args)

    kv_p = lambda a: a[seq - tail_w:].reshape(1, batch, tail_w, ATTN_HEADS, ATTN_DH)
    kv_s = lambda a: a.reshape(1, dec_batch, dec_seq, ATTN_HEADS, ATTN_DH)
    return (y_prompt.reshape(batch, seq, D_MODEL),
            y_sample.reshape(dec_batch, dec_seq, D_MODEL),
            r_fin.reshape(1, batch, RET_HEADS, RET_DK, RET_DV),
            kv_p(ka32), kv_p(va32),
            r_new.reshape(1, dec_batch, RET_HEADS, RET_DK, RET_DV),
            kv_s(ka32_s), kv_s(va32_s))
```

```python
import functools
import math

import jax
import jax.numpy as jnp
from jax import lax
from jax.experimental import pallas as pl
from jax.experimental.pallas import tpu as pltpu

F32 = jnp.float32
BF16 = jnp.bfloat16

D_MODEL = 1024
CHUNK = 64
PAST_LEN = 2048
RET_HEADS = 4
RET_DK = 128
RET_DV = 128
ATTN_HEADS = 8
ATTN_DH = 64
BAND_PAST_CHUNKS = 8
BAND_PAST = BAND_PAST_CHUNKS * CHUNK
REL_CLIP = 256
D_FF = 2816
PLE_DIM = 256
ROPE_THETA = 10000.0
EPS = 1e-6
RET_W = RET_HEADS * RET_DK
ATT_W = ATTN_HEADS * ATTN_DH
IN_COLS = 4 * RET_W + 3 * ATT_W
LOG_DECAY = tuple(math.log1p(-(2.0 ** (-5 - h))) for h in range(RET_HEADS))
LOG2E = math.log2(math.e)
MASKED = -1e30

LANES = 128
SUBLANES = 8
BIAS_LANES = 1024
TM_DENSE = 512
TQ_PROMPT = 256
VMEM_LIMIT = 56 * 1024 * 1024

_NT = (((1,), (1,)), ((), ()))
_TN = (((0,), (0,)), ((), ()))


def _rms(x, g):
    return x * lax.rsqrt(jnp.mean(x * x, axis=-1, keepdims=True) + EPS) * g


def _const_spec(shape):
    zeros = (0,) * len(shape)
    return pl.BlockSpec(shape, lambda i: zeros, pipeline_mode=pl.Buffered(1))


def _proj_kernel(x_ref, gmix_ref, w_ref, inv_ref, gq_ref, gk_ref,
                 qr_ref, kr_ref, vr_ref, sg_ref, qa_ref, qb_ref, ka_ref, va_ref, ka4_ref, va4_ref,
                 cos_scr, sin_scr, *, pos0, pos_stride, period, tail_only):
    i = pl.program_id(0)
    tm = x_ref.shape[0]
    inv = inv_ref[...]

    @pl.when(i == 0)
    def _init():
        r = lax.broadcasted_iota(jnp.int32, (tm, LANES), 0) % period
        ang = r.astype(F32) * inv
        cos_scr[...] = jnp.cos(ang)
        sin_scr[...] = jnp.sin(ang)

    base = jnp.full((1, LANES), pos0 + i * pos_stride, jnp.int32).astype(F32) * inv
    cb, sb = jnp.cos(base), jnp.sin(base)
    cos = cb * cos_scr[...] - sb * sin_scr[...]
    sin = sb * cos_scr[...] + cb * sin_scr[...]
    lane = lax.broadcasted_iota(jnp.int32, (tm, LANES), 1)
    first_half = lane < LANES // 2
    sin = jnp.where(first_half, -sin, sin)

    xn = _rms(x_ref[...], gmix_ref[...]).astype(BF16)

    def proj(group):
        return jnp.dot(xn, w_ref[:, group * RET_W:(group + 1) * RET_W], preferred_element_type=F32)

    def rope_store(z, out_ref, scale):
        for h in range(RET_HEADS):
            sl = slice(h * RET_DK, (h + 1) * RET_DK)
            zh = z[:, sl]
            r = zh * cos + pltpu.roll(zh, RET_DK // 2, 1) * sin
            if scale is not None:
                r = r * scale
            out_ref[:, sl] = r.astype(out_ref.dtype)

    def head_rms(z, g_ref):
        outs = []
        for j in range(ATT_W // LANES):
            sl = slice(j * LANES, (j + 1) * LANES)
            zj = z[:, sl]
            zz = zj * zj
            s0 = jnp.sum(jnp.where(first_half, zz, 0.0), axis=-1, keepdims=True)
            s1 = jnp.sum(jnp.where(first_half, 0.0, zz), axis=-1, keepdims=True)
            ms = jnp.where(first_half, s0, s1) * (1.0 / ATTN_DH)
            outs.append(zj * lax.rsqrt(ms + EPS) * g_ref[:, sl])
        return jnp.concatenate(outs, axis=1)

    def store_heads(out4_ref, z):
        for h in range(ATTN_HEADS):
            out4_ref[:, h, :] = z[:, h * ATTN_DH:(h + 1) * ATTN_DH]

    rope_store(proj(0), qr_ref, None)
    rope_store(proj(1), kr_ref, RET_DK ** -0.5)
    vr_ref[...] = proj(2).astype(BF16)
    sg_ref[...] = jax.nn.silu(proj(3)).astype(BF16)
    qs = head_rms(proj(4), gq_ref) * (ATTN_DH ** -0.5 * LOG2E)
    pair_first = jnp.concatenate([first_half] * (ATT_W // LANES), axis=1)
    qa_ref[...] = jnp.where(pair_first, qs, 0.0).astype(BF16)
    qb_ref[...] = jnp.where(pair_first, 0.0, qs).astype(BF16)
    ka = head_rms(proj(5), gk_ref)
    ka_ref[...] = ka.astype(BF16)
    va = proj(6)
    va_ref[...] = va.astype(BF16)

    if tail_only:
        @pl.when(i == pl.num_programs(0) - 1)
        def _tail():
            store_heads(ka4_ref, ka)
            store_heads(va4_ref, va)
    else:
        store_heads(ka4_ref, ka)
        store_heads(va4_ref, va)


def _proj_call(x, g_mix, w_in, inv, gq, gk, *, pos0, pos_stride, period, tail_only):
    n = x.shape[0]
    tm = TM_DENSE
    row = lambda w: pl.BlockSpec((tm, w), lambda i: (i, 0))
    bf = jax.ShapeDtypeStruct((n, RET_W), BF16)
    heads = (tm, ATTN_HEADS, ATTN_DH)
    if tail_only:
        kv4 = jax.ShapeDtypeStruct(heads, F32)
        kv4_spec = pl.BlockSpec(heads, lambda i: (0, 0, 0))
    else:
        kv4 = jax.ShapeDtypeStruct((n, ATTN_HEADS, ATTN_DH), F32)
        kv4_spec = pl.BlockSpec(heads, lambda i: (i, 0, 0))
    body = functools.partial(_proj_kernel, pos0=pos0, pos_stride=pos_stride, period=period,
                             tail_only=tail_only)
    return pl.pallas_call(
        body,
        grid=(n // tm,),
        in_specs=[row(D_MODEL), _const_spec((1, D_MODEL)), _const_spec((D_MODEL, IN_COLS)),
                  _const_spec((1, LANES)), _const_spec((1, ATT_W)), _const_spec((1, ATT_W))],
        out_specs=[row(RET_W)] * 8 + [kv4_spec] * 2,
        out_shape=[bf] * 8 + [kv4] * 2,
        scratch_shapes=[pltpu.VMEM((tm, LANES), F32), pltpu.VMEM((tm, LANES), F32)],
        compiler_params=pltpu.CompilerParams(dimension_semantics=("arbitrary",),
                                             vmem_limit_bytes=VMEM_LIMIT),
        name="proj",
    )(x, g_mix, w_in, inv, gq, gk)


def _init_decay(dmat_scr, xi_scr, zeta_scr, t):
    n = lax.broadcasted_iota(jnp.int32, (t, t), 0)
    m = lax.broadcasted_iota(jnp.int32, (t, t), 1)
    diff = (n - m).astype(F32)
    pos = lax.broadcasted_iota(jnp.int32, (t, RET_DK), 0).astype(F32)
    for h in range(RET_HEADS):
        lg = LOG_DECAY[h]
        dmat_scr[h] = jnp.where(diff >= 0, jnp.exp(jnp.maximum(diff, 0.0) * lg), 0.0)
        sl = slice(h * RET_DK, (h + 1) * RET_DK)
        xi_scr[:, sl] = jnp.exp((pos + 1.0) * lg)
        zeta_scr[:, sl] = jnp.exp((t - 1.0 - pos) * lg)


def _init_bias(vb_ref, bias_scr, t, n_keys, band):
    width = bias_scr.shape[2]
    r = lax.broadcasted_iota(jnp.int32, (SUBLANES, BIAS_LANES), 0)
    lane = lax.broadcasted_iota(jnp.int32, (SUBLANES, width), 1)
    for h in range(ATTN_HEADS):
        base = jnp.broadcast_to(vb_ref[h:h + 1, :], (SUBLANES, BIAS_LANES)) * LOG2E
        base = pltpu.roll(base, BIAS_LANES - REL_CLIP, 1)
        for shift in (1, 2, 4):
            base = jnp.where((r & shift) != 0, pltpu.roll(base, shift, 1), base)
        for g in range(t // SUBLANES):
            rows = (pltpu.roll(base, SUBLANES * g, 1) if g else base)[:, :width]
            if band:
                lo = (SUBLANES * g) // CHUNK * CHUNK
                hi = lo + (BAND_PAST_CHUNKS + 1) * CHUNK
                rows = jnp.where(jnp.logical_and(lane >= lo, lane < hi), rows, MASKED)
            elif width > n_keys:
                rows = jnp.where(lane < n_keys, rows, MASKED)
            bias_scr[h, g * SUBLANES:(g + 1) * SUBLANES, :] = rows


def _retention(q, k, v, r, dmat, xi, zeta, decay_t):
    s = lax.dot_general(q, k, _NT, preferred_element_type=F32) * dmat
    inner = jnp.dot(s.astype(BF16), v, preferred_element_type=F32)
    cross = jnp.dot(q, r.astype(BF16), preferred_element_type=F32) * xi
    kz = (k.astype(F32) * zeta).astype(BF16)
    r_new = r * decay_t + lax.dot_general(kz, v, _TN, preferred_element_type=F32)
    return inner + cross, r_new


def _attend_pair(qa, qb, k, v, bias_a, bias_b, mask_row):
    rhs = jnp.concatenate([v, jnp.ones_like(v)], axis=1)
    outs = []
    for q, bias in ((qa, bias_a), (qb, bias_b)):
        s = lax.dot_general(q, k, _NT, preferred_element_type=F32) + bias
        if mask_row is not None:
            s = s + mask_row
        e = jnp.exp2(s - jnp.max(s, axis=-1, keepdims=True)).astype(BF16)
        out = jnp.dot(e, rhs, preferred_element_type=F32)
        outs.append(out[:, :LANES] / out[:, LANES:])
    first = lax.broadcasted_iota(jnp.int32, outs[0].shape, 1) < ATTN_DH
    return jnp.where(first, outs[0], outs[1])


def _mix_prompt_kernel(qr_ref, kr_ref, vr_ref, sg_ref, qa_ref, qb_ref, k0_ref, k1_ref, k2_ref,
                       v0_ref, v1_ref, v2_ref, gro_ref, vb_ref,
                       mixed_ref, rfin_ref,
                       r_scr, bias_scr, dmat_scr, xi_scr, zeta_scr):
    i = pl.program_id(0)
    t = TQ_PROMPT
    width = 3 * t

    @pl.when(i == 0)
    def _init():
        r_scr[...] = jnp.zeros_like(r_scr)
        _init_decay(dmat_scr, xi_scr, zeta_scr, t)
        _init_bias(vb_ref, bias_scr, t, width, band=True)

    for h in range(RET_HEADS):
        sl = slice(h * RET_DK, (h + 1) * RET_DK)
        o, r_new = _retention(qr_ref[:, sl], kr_ref[:, sl], vr_ref[:, sl], r_scr[h],
                              dmat_scr[h], xi_scr[:, sl], zeta_scr[:, sl],
                              math.exp(t * LOG_DECAY[h]))
        r_scr[h] = r_new
        mixed_ref[:, sl] = (sg_ref[:, sl].astype(F32) * _rms(o, gro_ref[:, sl])).astype(BF16)

    def attention(mask_row):
        for j in range(ATTN_HEADS // 2):
            sl = slice(j * LANES, (j + 1) * LANES)
            k = jnp.concatenate([k0_ref[:, sl], k1_ref[:, sl], k2_ref[:, sl]], axis=0)
            v = jnp.concatenate([v0_ref[:, sl], v1_ref[:, sl], v2_ref[:, sl]], axis=0)
            o = _attend_pair(qa_ref[:, sl], qb_ref[:, sl], k, v,
                             bias_scr[2 * j], bias_scr[2 * j + 1], mask_row)
            mixed_ref[:, RET_W + j * LANES:RET_W + (j + 1) * LANES] = o.astype(BF16)

    @pl.when(i < 2)
    def _start():
        blk = lax.broadcasted_iota(jnp.int32, (1, width), 1) // t
        attention(jnp.where(blk + i >= 2, 0.0, MASKED))

    @pl.when(i >= 2)
    def _steady():
        attention(None)

    @pl.when(i == pl.num_programs(0) - 1)
    def _fin():
        rfin_ref[...] = r_scr[...]


def _mix_prompt_call(qr, kr, vr, sg, qa, qb, ka, va, gro, vb):
    n = qr.shape[0]
    t = TQ_PROMPT
    row = pl.BlockSpec((t, RET_W), lambda i: (i, 0))
    back = lambda d: pl.BlockSpec((t, ATT_W), lambda i: (jnp.maximum(i - d, 0), 0))
    state = (RET_HEADS, RET_DK, RET_DV)
    return pl.pallas_call(
        _mix_prompt_kernel,
        grid=(n // t,),
        in_specs=[row] * 6 + [back(2), back(1), back(0)] * 2
                 + [_const_spec((1, RET_W)), _const_spec((ATTN_HEADS, BIAS_LANES))],
        out_specs=[pl.BlockSpec((t, D_MODEL), lambda i: (i, 0)),
                   pl.BlockSpec(state, lambda i: (0, 0, 0))],
        out_shape=[jax.ShapeDtypeStruct((n, D_MODEL), BF16), jax.ShapeDtypeStruct(state, F32)],
        scratch_shapes=[pltpu.VMEM(state, F32),
                        pltpu.VMEM((ATTN_HEADS, t, 3 * t), F32),
                        pltpu.VMEM((RET_HEADS, t, t), F32),
                        pltpu.VMEM((t, RET_W), F32),
                        pltpu.VMEM((t, RET_W), F32)],
        compiler_params=pltpu.CompilerParams(dimension_semantics=("arbitrary",),
                                             vmem_limit_bytes=VMEM_LIMIT),
        name="mix_prompt",
    )(qr, kr, vr, sg, qa, qb, ka, ka, ka, va, va, va, gro, vb)


def _mix_sample_kernel(qr_ref, kr_ref, vr_ref, sg_ref, qa_ref, qb_ref, ka_ref, va_ref, ck_ref, cv_ref,
                       r0_ref, gro_ref, vb_ref,
                       mixed_ref, rnew_ref,
                       bias_scr, dmat_scr, xi_scr, zeta_scr):
    t = qr_ref.shape[0]
    n_keys = ck_ref.shape[0] + t
    pad = bias_scr.shape[2] - n_keys

    @pl.when(pl.program_id(0) == 0)
    def _init():
        _init_decay(dmat_scr, xi_scr, zeta_scr, t)
        _init_bias(vb_ref, bias_scr, t, n_keys, band=False)

    for h in range(RET_HEADS):
        sl = slice(h * RET_DK, (h + 1) * RET_DK)
        o, r_new = _retention(qr_ref[:, sl], kr_ref[:, sl], vr_ref[:, sl], r0_ref[0, h],
                              dmat_scr[h], xi_scr[:, sl], zeta_scr[:, sl],
                              math.exp(t * LOG_DECAY[h]))
        rnew_ref[0, h] = r_new
        mixed_ref[:, sl] = (sg_ref[:, sl].astype(F32) * _rms(o, gro_ref[:, sl])).astype(BF16)

    zeros = jnp.zeros((pad, LANES), BF16)
    for j in range(ATTN_HEADS // 2):
        sl = slice(j * LANES, (j + 1) * LANES)
        ck = jnp.concatenate([ck_ref[:, 2 * j, :], ck_ref[:, 2 * j + 1, :]], axis=1).astype(BF16)
        cv = jnp.concatenate([cv_ref[:, 2 * j, :], cv_ref[:, 2 * j + 1, :]], axis=1).astype(BF16)
        k = jnp.concatenate([ck, ka_ref[:, sl], zeros], axis=0)
        v = jnp.concatenate([cv, va_ref[:, sl], zeros], axis=0)
        o = _attend_pair(qa_ref[:, sl], qb_ref[:, sl], k, v,
                         bias_scr[2 * j], bias_scr[2 * j + 1], None)
        mixed_ref[:, RET_W + j * LANES:RET_W + (j + 1) * LANES] = o.astype(BF16)


def _mix_sample_call(qr, kr, vr, sg, qa, qb, ka, va, ck, cv, r0, gro, vb, t):
    n = qr.shape[0]
    nb, w = n // t, ck.shape[0] // (n // t)
    keys_padded = -(-(w + t) // LANES) * LANES
    row = pl.BlockSpec((t, RET_W), lambda b: (b, 0))
    cache = pl.BlockSpec((w, ATTN_HEADS, ATTN_DH), lambda b: (b, 0, 0))
    state = pl.BlockSpec((1, RET_HEADS, RET_DK, RET_DV), lambda b: (b, 0, 0, 0))
    return pl.pallas_call(
        _mix_sample_kernel,
        grid=(nb,),
        in_specs=[row] * 8 + [cache, cache, state,
                              _const_spec((1, RET_W)), _const_spec((ATTN_HEADS, BIAS_LANES))],
        out_specs=[pl.BlockSpec((t, D_MODEL), lambda b: (b, 0)), state],
        out_shape=[jax.ShapeDtypeStruct((n, D_MODEL), BF16),
                   jax.ShapeDtypeStruct((nb, RET_HEADS, RET_DK, RET_DV), F32)],
        scratch_shapes=[pltpu.VMEM((ATTN_HEADS, t, keys_padded), F32),
                        pltpu.VMEM((RET_HEADS, t, t), F32),
                        pltpu.VMEM((t, RET_W), F32),
                        pltpu.VMEM((t, RET_W), F32)],
        compiler_params=pltpu.CompilerParams(dimension_semantics=("arbitrary",),
                                             vmem_limit_bytes=VMEM_LIMIT),
        name="mix_sample",
    )(qr, kr, vr, sg, qa, qb, ka, va, ck, cv, r0, gro, vb)


FF_SPLITS = (0, 1024, 2048, D_FF)


def _tail_kernel(x_ref, mixed_ref, p_ref, wout_ref, gffn_ref, wg_ref, wu_ref, wd_ref,
                 gple_ref, wpg_ref, wpp_ref, out_ref):
    h = x_ref[...] + jnp.dot(mixed_ref[...], wout_ref[...], preferred_element_type=F32)
    u = _rms(h, gffn_ref[...]).astype(BF16)
    ffn = None
    for c0, c1 in zip(FF_SPLITS[:-1], FF_SPLITS[1:]):
        a = jnp.dot(u, wg_ref[:, c0:c1], preferred_element_type=F32)
        b = jnp.dot(u, wu_ref[:, c0:c1], preferred_element_type=F32)
        part = jnp.dot((jax.nn.silu(a) * b).astype(BF16), wd_ref[c0:c1, :],
                       preferred_element_type=F32)
        ffn = part if ffn is None else ffn + part
    h = h + ffn
    gate = jax.nn.sigmoid(jnp.dot(_rms(h, gple_ref[...]).astype(BF16), wpg_ref[...],
                                  preferred_element_type=F32))
    ple = jnp.dot(p_ref[...].astype(BF16), wpp_ref[...], preferred_element_type=F32)
    out_ref[...] = h + ple * gate


def _tail_call(x, mixed, p, w_out, g_ffn, w_gate, w_up, w_down, g_ple, w_pg, w_pp):
    n = x.shape[0]
    tm = TM_DENSE
    row = lambda w: pl.BlockSpec((tm, w), lambda i: (i, 0))
    return pl.pallas_call(
        _tail_kernel,
        grid=(n // tm,),
        in_specs=[row(D_MODEL), row(D_MODEL), row(PLE_DIM),
                  _const_spec((D_MODEL, D_MODEL)), _const_spec((1, D_MODEL)),
                  _const_spec((D_MODEL, D_FF)), _const_spec((D_MODEL, D_FF)),
                  _const_spec((D_FF, D_MODEL)), _const_spec((1, D_MODEL)),
                  _const_spec((D_MODEL, D_MODEL)), _const_spec((PLE_DIM, D_MODEL))],
        out_specs=row(D_MODEL),
        out_shape=jax.ShapeDtypeStruct((n, D_MODEL), F32),
        compiler_params=pltpu.CompilerParams(dimension_semantics=("arbitrary",),
                                             vmem_limit_bytes=VMEM_LIMIT),
        name="tail",
    )(x, mixed, p, w_out, g_ffn, w_gate, w_up, w_down, g_ple, w_pg, w_pp)


def _bias_rows(rel_bias):
    edge = jnp.broadcast_to(rel_bias[:, 2 * REL_CLIP:], (ATTN_HEADS, BIAS_LANES - 2 * REL_CLIP))
    return jnp.concatenate([edge, rel_bias[:, 2 * REL_CLIP:0:-1]], axis=1)


def kernel(x_prompt, x_sample, cache_attn_k, cache_attn_v, state_ret, p_prompt, p_sample,
           g_mix, w_in, g_ret_out, g_q_attn, g_k_attn, rel_bias, w_out, g_ffn,
           w_ffn_gate, w_ffn_up, w_ffn_down, g_ple, w_ple_gate, w_ple_proj):
    depth = w_in.shape[0]
    assert depth == 1, "single-layer trunk"
    batch, seq, _ = x_prompt.shape
    dec_batch, dec_seq, _ = x_sample.shape
    cache_w = cache_attn_k.shape[2]
    assert batch == 1 and seq % TM_DENSE == 0 and dec_seq == CHUNK and cache_w == BAND_PAST
    assert min(BAND_PAST, seq) == TM_DENSE, "the returned prompt keys are the last projection tile"

    row = lambda g: g.reshape(1, -1)
    gq = row(jnp.tile(g_q_attn[0], ATTN_HEADS))
    gk = row(jnp.tile(g_k_attn[0], ATTN_HEADS))
    gro = row(g_ret_out[0])
    vb = _bias_rows(rel_bias[0])
    half = RET_DK // 2
    inv = ROPE_THETA ** (-jnp.arange(half, dtype=F32) / half)
    inv = row(jnp.concatenate([inv, inv]))
    w_in_b = w_in[0].astype(BF16)
    tail_w_args = (w_out[0].astype(BF16), row(g_ffn[0]), w_ffn_gate[0].astype(BF16),
                   w_ffn_up[0].astype(BF16), w_ffn_down[0].astype(BF16), row(g_ple[0]),
                   w_ple_gate[0].astype(BF16), w_ple_proj[0].astype(BF16))

    xp = x_prompt.reshape(seq, D_MODEL)
    qr, kr, vr, sg, qa, qb, ka, va, k_tail, v_tail = _proj_call(
        xp, row(g_mix[0]), w_in_b, inv, gq, gk,
        pos0=0, pos_stride=TM_DENSE, period=TM_DENSE, tail_only=True)
    mixed_p, r_fin = _mix_prompt_call(qr, kr, vr, sg, qa, qb, ka, va, gro, vb)
    y_prompt = _tail_call(xp, mixed_p, p_prompt[0].reshape(seq, PLE_DIM), *tail_w_args)

    n_s = dec_batch * dec_seq
    xs = x_sample.reshape(n_s, D_MODEL)
    qr, kr, vr, sg, qa, qb, ka, va, k_new, v_new = _proj_call(
        xs, row(g_mix[0]), w_in_b, inv, gq, gk,
        pos0=PAST_LEN, pos_stride=0, period=dec_seq, tail_only=False)
    mixed_s, r_new = _mix_sample_call(
        qr, kr, vr, sg, qa, qb, ka, va,
        cache_attn_k[0].reshape(dec_batch * cache_w, ATTN_HEADS, ATTN_DH),
        cache_attn_v[0].reshape(dec_batch * cache_w, ATTN_HEADS, ATTN_DH),
        state_ret[0], gro, vb, dec_seq)
    y_sample = _tail_call(xs, mixed_s, p_sample[0].reshape(n_s, PLE_DIM), *tail_w_args)

    kv_p = lambda a: a.reshape(1, batch, TM_DENSE, ATTN_HEADS, ATTN_DH)
    kv_s = lambda a: a.reshape(1, dec_batch, dec_seq, ATTN_HEADS, ATTN_DH)
    return (y_prompt.reshape(batch, seq, D_MODEL),
            y_sample.reshape(dec_batch, dec_seq, D_MODEL),
            r_fin.reshape(1, batch, RET_HEADS, RET_DK, RET_DV),
            kv_p(k_tail), kv_p(v_tail),
            r_new.reshape(1, dec_batch, RET_HEADS, RET_DK, RET_DV),
            kv_s(k_new), kv_s(v_new))
```

```python
import functools
import math

import jax
import jax.numpy as jnp
from jax import lax
from jax.experimental import pallas as pl
from jax.experimental.pallas import tpu as pltpu

F32 = jnp.float32
BF16 = jnp.bfloat16

D_MODEL = 1024
CHUNK = 64
PAST_LEN = 2048
RET_HEADS = 4
RET_DK = 128
RET_DV = 128
ATTN_HEADS = 8
ATTN_DH = 64
BAND_PAST_CHUNKS = 8
BAND_PAST = BAND_PAST_CHUNKS * CHUNK
REL_CLIP = 256
D_FF = 2816
PLE_DIM = 256
ROPE_THETA = 10000.0
EPS = 1e-6
RET_W = RET_HEADS * RET_DK
ATT_W = ATTN_HEADS * ATTN_DH
IN_COLS = 4 * RET_W + 3 * ATT_W
LOG_DECAY = tuple(math.log1p(-(2.0 ** (-5 - h))) for h in range(RET_HEADS))
LOG2E = math.log2(math.e)
MASKED = -1e30

LANES = 128
SUBLANES = 8
BIAS_LANES = 1024
TM_DENSE = 512
TQ_PROMPT = 256
VMEM_LIMIT = 56 * 1024 * 1024

_NT = (((1,), (1,)), ((), ()))
_TN = (((0,), (0,)), ((), ()))


def _rms(x, g):
    return x * lax.rsqrt(jnp.mean(x * x, axis=-1, keepdims=True) + EPS) * g


def _const_spec(shape):
    zeros = (0,) * len(shape)
    return pl.BlockSpec(shape, lambda i: zeros, pipeline_mode=pl.Buffered(1))


def _proj_kernel(x_ref, gmix_ref, w_ref, inv_ref, gq_ref, gk_ref,
                 qr_ref, kr_ref, vr_ref, sg_ref, qa_ref, qb_ref, ka_ref, va_ref, ka32_ref, va32_ref,
                 cos_scr, sin_scr, *, pos0, pos_stride, period, tail_only):
    i = pl.program_id(0)
    tm = x_ref.shape[0]
    inv = inv_ref[...]

    @pl.when(i == 0)
    def _init():
        r = lax.broadcasted_iota(jnp.int32, (tm, LANES), 0) % period
        ang = r.astype(F32) * inv
        cos_scr[...] = jnp.cos(ang)
        sin_scr[...] = jnp.sin(ang)

    base = jnp.full((1, LANES), pos0 + i * pos_stride, jnp.int32).astype(F32) * inv
    cb, sb = jnp.cos(base), jnp.sin(base)
    cos = cb * cos_scr[...] - sb * sin_scr[...]
    sin = sb * cos_scr[...] + cb * sin_scr[...]
    lane = lax.broadcasted_iota(jnp.int32, (tm, LANES), 1)
    first_half = lane < LANES // 2
    sin = jnp.where(first_half, -sin, sin)

    xn = _rms(x_ref[...], gmix_ref[...]).astype(BF16)

    def proj(group):
        return jnp.dot(xn, w_ref[:, group * RET_W:(group + 1) * RET_W], preferred_element_type=F32)

    def rope_store(z, out_ref, scale):
        for h in range(RET_HEADS):
            sl = slice(h * RET_DK, (h + 1) * RET_DK)
            zh = z[:, sl]
            r = zh * cos + pltpu.roll(zh, RET_DK // 2, 1) * sin
            if scale is not None:
                r = r * scale
            out_ref[:, sl] = r.astype(out_ref.dtype)

    def head_rms(z, g_ref):
        outs = []
        for j in range(ATT_W // LANES):
            sl = slice(j * LANES, (j + 1) * LANES)
            zj = z[:, sl]
            zz = zj * zj
            s0 = jnp.sum(jnp.where(first_half, zz, 0.0), axis=-1, keepdims=True)
            s1 = jnp.sum(jnp.where(first_half, 0.0, zz), axis=-1, keepdims=True)
            ms = jnp.where(first_half, s0, s1) * (1.0 / ATTN_DH)
            outs.append(zj * lax.rsqrt(ms + EPS) * g_ref[:, sl])
        return jnp.concatenate(outs, axis=1)

    rope_store(proj(0), qr_ref, None)
    rope_store(proj(1), kr_ref, RET_DK ** -0.5)
    vr_ref[...] = proj(2).astype(BF16)
    sg_ref[...] = jax.nn.silu(proj(3)).astype(BF16)
    qs = head_rms(proj(4), gq_ref) * (ATTN_DH ** -0.5 * LOG2E)
    pair_first = jnp.concatenate([first_half] * (ATT_W // LANES), axis=1)
    qa_ref[...] = jnp.where(pair_first, qs, 0.0).astype(BF16)
    qb_ref[...] = jnp.where(pair_first, 0.0, qs).astype(BF16)
    ka = head_rms(proj(5), gk_ref)
    ka_ref[...] = ka.astype(BF16)
    va = proj(6)
    va_ref[...] = va.astype(BF16)

    if tail_only:
        @pl.when(i == pl.num_programs(0) - 1)
        def _tail():
            ka32_ref[...] = ka.T
            va32_ref[...] = va.T
    else:
        ka32_ref[...] = ka
        va32_ref[...] = va


def _proj_call(x, g_mix, w_in, inv, gq, gk, *, pos0, pos_stride, period, tail_only):
    n = x.shape[0]
    tm = TM_DENSE
    row = lambda w: pl.BlockSpec((tm, w), lambda i: (i, 0))
    bf = jax.ShapeDtypeStruct((n, RET_W), BF16)
    if tail_only:
        assert tm == ATT_W
        kv32 = jax.ShapeDtypeStruct((ATT_W, tm), F32)
        kv32_spec = pl.BlockSpec((ATT_W, tm), lambda i: (0, 0))
    else:
        kv32 = jax.ShapeDtypeStruct((n, ATT_W), F32)
        kv32_spec = row(ATT_W)
    body = functools.partial(_proj_kernel, pos0=pos0, pos_stride=pos_stride, period=period,
                             tail_only=tail_only)
    return pl.pallas_call(
        body,
        grid=(n // tm,),
        in_specs=[row(D_MODEL), _const_spec((1, D_MODEL)), _const_spec((D_MODEL, IN_COLS)),
                  _const_spec((1, LANES)), _const_spec((1, ATT_W)), _const_spec((1, ATT_W))],
        out_specs=[row(RET_W)] * 8 + [kv32_spec] * 2,
        out_shape=[bf] * 8 + [kv32] * 2,
        scratch_shapes=[pltpu.VMEM((tm, LANES), F32), pltpu.VMEM((tm, LANES), F32)],
        compiler_params=pltpu.CompilerParams(dimension_semantics=("arbitrary",),
                                             vmem_limit_bytes=VMEM_LIMIT),
        name="proj",
    )(x, g_mix, w_in, inv, gq, gk)


def _init_decay(dmat_scr, xi_scr, zeta_scr, t):
    n = lax.broadcasted_iota(jnp.int32, (t, t), 0)
    m = lax.broadcasted_iota(jnp.int32, (t, t), 1)
    diff = (n - m).astype(F32)
    pos = lax.broadcasted_iota(jnp.int32, (t, RET_DK), 0).astype(F32)
    for h in range(RET_HEADS):
        lg = LOG_DECAY[h]
        dmat_scr[h] = jnp.where(diff >= 0, jnp.exp(jnp.maximum(diff, 0.0) * lg), 0.0)
        sl = slice(h * RET_DK, (h + 1) * RET_DK)
        xi_scr[:, sl] = jnp.exp((pos + 1.0) * lg)
        zeta_scr[:, sl] = jnp.exp((t - 1.0 - pos) * lg)


def _init_bias(vb_ref, bias_scr, t, n_keys, band):
    width = bias_scr.shape[2]
    r = lax.broadcasted_iota(jnp.int32, (SUBLANES, BIAS_LANES), 0)
    lane = lax.broadcasted_iota(jnp.int32, (SUBLANES, width), 1)
    for h in range(ATTN_HEADS):
        base = jnp.broadcast_to(vb_ref[h:h + 1, :], (SUBLANES, BIAS_LANES)) * LOG2E
        base = pltpu.roll(base, BIAS_LANES - REL_CLIP, 1)
        for shift in (1, 2, 4):
            base = jnp.where((r & shift) != 0, pltpu.roll(base, shift, 1), base)
        for g in range(t // SUBLANES):
            rows = (pltpu.roll(base, SUBLANES * g, 1) if g else base)[:, :width]
            if band:
                lo = (SUBLANES * g) // CHUNK * CHUNK
                hi = lo + (BAND_PAST_CHUNKS + 1) * CHUNK
                rows = jnp.where(jnp.logical_and(lane >= lo, lane < hi), rows, MASKED)
            elif width > n_keys:
                rows = jnp.where(lane < n_keys, rows, MASKED)
            bias_scr[h, g * SUBLANES:(g + 1) * SUBLANES, :] = rows


def _retention(q, k, v, r, dmat, xi, zeta, decay_t):
    s = lax.dot_general(q, k, _NT, preferred_element_type=F32) * dmat
    inner = jnp.dot(s.astype(BF16), v, preferred_element_type=F32)
    cross = jnp.dot(q, r.astype(BF16), preferred_element_type=F32) * xi
    kz = (k.astype(F32) * zeta).astype(BF16)
    r_new = r * decay_t + lax.dot_general(kz, v, _TN, preferred_element_type=F32)
    return inner + cross, r_new


def _attend_pair(qa, qb, k, v, bias_a, bias_b, mask_row):
    rhs = jnp.concatenate([v, jnp.ones_like(v)], axis=1)
    outs = []
    for q, bias in ((qa, bias_a), (qb, bias_b)):
        s = lax.dot_general(q, k, _NT, preferred_element_type=F32) + bias
        if mask_row is not None:
            s = s + mask_row
        e = jnp.exp2(s - jnp.max(s, axis=-1, keepdims=True)).astype(BF16)
        out = jnp.dot(e, rhs, preferred_element_type=F32)
        outs.append(out[:, :LANES] / out[:, LANES:])
    first = lax.broadcasted_iota(jnp.int32, outs[0].shape, 1) < ATTN_DH
    return jnp.where(first, outs[0], outs[1])


def _mix_prompt_kernel(qr_ref, kr_ref, vr_ref, sg_ref, qa_ref, qb_ref, k0_ref, k1_ref, k2_ref,
                       v0_ref, v1_ref, v2_ref, gro_ref, vb_ref,
                       mixed_ref, rfin_ref,
                       r_scr, bias_scr, dmat_scr, xi_scr, zeta_scr):
    i = pl.program_id(0)
    t = TQ_PROMPT
    width = 3 * t

    @pl.when(i == 0)
    def _init():
        r_scr[...] = jnp.zeros_like(r_scr)
        _init_decay(dmat_scr, xi_scr, zeta_scr, t)
        _init_bias(vb_ref, bias_scr, t, width, band=True)

    for h in range(RET_HEADS):
        sl = slice(h * RET_DK, (h + 1) * RET_DK)
        o, r_new = _retention(qr_ref[:, sl], kr_ref[:, sl], vr_ref[:, sl], r_scr[h],
                              dmat_scr[h], xi_scr[:, sl], zeta_scr[:, sl],
                              math.exp(t * LOG_DECAY[h]))
        r_scr[h] = r_new
        mixed_ref[:, sl] = (sg_ref[:, sl].astype(F32) * _rms(o, gro_ref[:, sl])).astype(BF16)

    def attention(mask_row):
        for j in range(ATTN_HEADS // 2):
            sl = slice(j * LANES, (j + 1) * LANES)
            k = jnp.concatenate([k0_ref[:, sl], k1_ref[:, sl], k2_ref[:, sl]], axis=0)
            v = jnp.concatenate([v0_ref[:, sl], v1_ref[:, sl], v2_ref[:, sl]], axis=0)
            o = _attend_pair(qa_ref[:, sl], qb_ref[:, sl], k, v,
                             bias_scr[2 * j], bias_scr[2 * j + 1], mask_row)
            mixed_ref[:, RET_W + j * LANES:RET_W + (j + 1) * LANES] = o.astype(BF16)

    @pl.when(i < 2)
    def _start():
        blk = lax.broadcasted_iota(jnp.int32, (1, width), 1) // t
        attention(jnp.where(blk + i >= 2, 0.0, MASKED))

    @pl.when(i >= 2)
    def _steady():
        attention(None)

    @pl.when(i == pl.num_programs(0) - 1)
    def _fin():
        rfin_ref[...] = r_scr[...]


def _mix_prompt_call(qr, kr, vr, sg, qa, qb, ka, va, gro, vb):
    n = qr.shape[0]
    t = TQ_PROMPT
    row = pl.BlockSpec((t, RET_W), lambda i: (i, 0))
    back = lambda d: pl.BlockSpec((t, ATT_W), lambda i: (jnp.maximum(i - d, 0), 0))
    state = (RET_HEADS, RET_DK, RET_DV)
    return pl.pallas_call(
        _mix_prompt_kernel,
        grid=(n // t,),
        in_specs=[row] * 6 + [back(2), back(1), back(0)] * 2
                 + [_const_spec((1, RET_W)), _const_spec((ATTN_HEADS, BIAS_LANES))],
        out_specs=[pl.BlockSpec((t, D_MODEL), lambda i: (i, 0)),
                   pl.BlockSpec(state, lambda i: (0, 0, 0))],
        out_shape=[jax.ShapeDtypeStruct((n, D_MODEL), BF16), jax.ShapeDtypeStruct(state, F32)],
        scratch_shapes=[pltpu.VMEM(state, F32),
                        pltpu.VMEM((ATTN_HEADS, t, 3 * t), F32),
                        pltpu.VMEM((RET_HEADS, t, t), F32),
                        pltpu.VMEM((t, RET_W), F32),
                        pltpu.VMEM((t, RET_W), F32)],
        compiler_params=pltpu.CompilerParams(dimension_semantics=("arbitrary",),
                                             vmem_limit_bytes=VMEM_LIMIT),
        name="mix_prompt",
    )(qr, kr, vr, sg, qa, qb, ka, ka, ka, va, va, va, gro, vb)


def _mix_sample_kernel(qr_ref, kr_ref, vr_ref, sg_ref, qa_ref, qb_ref, ka_ref, va_ref, ck_ref, cv_ref,
                       r0_ref, gro_ref, vb_ref,
                       mixed_ref, rnew_ref,
                       bias_scr, dmat_scr, xi_scr, zeta_scr):
    t = qr_ref.shape[0]
    w = ck_ref.shape[2]

    @pl.when(pl.program_id(0) == 0)
    def _init():
        _init_decay(dmat_scr, xi_scr, zeta_scr, t)
        _init_bias(vb_ref, bias_scr, t, w + t, band=False)

    for h in range(RET_HEADS):
        sl = slice(h * RET_DK, (h + 1) * RET_DK)
        o, r_new = _retention(qr_ref[:, sl], kr_ref[:, sl], vr_ref[:, sl], r0_ref[0, h],
                              dmat_scr[h], xi_scr[:, sl], zeta_scr[:, sl],
                              math.exp(t * LOG_DECAY[h]))
        rnew_ref[0, h] = r_new
        mixed_ref[:, sl] = (sg_ref[:, sl].astype(F32) * _rms(o, gro_ref[:, sl])).astype(BF16)

    first = lax.broadcasted_iota(jnp.int32, (t, LANES), 1) < ATTN_DH
    for j in range(ATTN_HEADS // 2):
        sl = slice(j * LANES, (j + 1) * LANES)
        k_cache = ck_ref[0, sl, :].astype(BF16)
        v_cache = cv_ref[0, sl, :].astype(BF16)
        k_new = ka_ref[:, sl]
        rhs_cache = jnp.concatenate([v_cache, jnp.ones_like(v_cache)], axis=0)
        rhs_new = jnp.concatenate([va_ref[:, sl], jnp.ones_like(k_new)], axis=1)
        outs = []
        for q_ref, h in ((qa_ref, 2 * j), (qb_ref, 2 * j + 1)):
            q = q_ref[:, sl]
            s_c = jnp.dot(q, k_cache, preferred_element_type=F32) + bias_scr[h, :, :w]
            s_n = (lax.dot_general(q, k_new, _NT, preferred_element_type=F32)
                   + bias_scr[h, :, w:w + t])
            m = jnp.maximum(jnp.max(s_c, axis=-1, keepdims=True),
                            jnp.max(s_n, axis=-1, keepdims=True))
            e_c = jnp.exp2(s_c - m).astype(BF16)
            e_n = jnp.exp2(s_n - m).astype(BF16)
            out = (lax.dot_general(e_c, rhs_cache, _NT, preferred_element_type=F32)
                   + jnp.dot(e_n, rhs_new, preferred_element_type=F32))
            outs.append(out[:, :LANES] / out[:, LANES:])
        o = jnp.where(first, outs[0], outs[1])
        mixed_ref[:, RET_W + j * LANES:RET_W + (j + 1) * LANES] = o.astype(BF16)


def _mix_sample_call(qr, kr, vr, sg, qa, qb, ka, va, ck, cv, r0, gro, vb, t):
    n = qr.shape[0]
    nb, _, w = ck.shape
    keys_padded = -(-(w + t) // LANES) * LANES
    row = pl.BlockSpec((t, RET_W), lambda b: (b, 0))
    cache = pl.BlockSpec((1, ATT_W, w), lambda b: (b, 0, 0))
    state = pl.BlockSpec((1, RET_HEADS, RET_DK, RET_DV), lambda b: (b, 0, 0, 0))
    return pl.pallas_call(
        _mix_sample_kernel,
        grid=(nb,),
        in_specs=[row] * 8 + [cache, cache, state,
                              _const_spec((1, RET_W)), _const_spec((ATTN_HEADS, BIAS_LANES))],
        out_specs=[pl.BlockSpec((t, D_MODEL), lambda b: (b, 0)), state],
        out_shape=[jax.ShapeDtypeStruct((n, D_MODEL), BF16),
                   jax.ShapeDtypeStruct((nb, RET_HEADS, RET_DK, RET_DV), F32)],
        scratch_shapes=[pltpu.VMEM((ATTN_HEADS, t, keys_padded), F32),
                        pltpu.VMEM((RET_HEADS, t, t), F32),
                        pltpu.VMEM((t, RET_W), F32),
                        pltpu.VMEM((t, RET_W), F32)],
        compiler_params=pltpu.CompilerParams(dimension_semantics=("arbitrary",),
                                             vmem_limit_bytes=VMEM_LIMIT),
        name="mix_sample",
    )(qr, kr, vr, sg, qa, qb, ka, va, ck, cv, r0, gro, vb)


FF_SPLITS = (0, 1024, 2048, D_FF)


def _tail_kernel(x_ref, mixed_ref, p_ref, wout_ref, gffn_ref, wg_ref, wu_ref, wd_ref,
                 gple_ref, wpg_ref, wpp_ref, out_ref):
    h = x_ref[...] + jnp.dot(mixed_ref[...], wout_ref[...], preferred_element_type=F32)
    u = _rms(h, gffn_ref[...]).astype(BF16)
    ffn = None
    for c0, c1 in zip(FF_SPLITS[:-1], FF_SPLITS[1:]):
        a = jnp.dot(u, wg_ref[:, c0:c1], preferred_element_type=F32)
        b = jnp.dot(u, wu_ref[:, c0:c1], preferred_element_type=F32)
        part = jnp.dot((jax.nn.silu(a) * b).astype(BF16), wd_ref[c0:c1, :],
                       preferred_element_type=F32)
        ffn = part if ffn is None else ffn + part
    h = h + ffn
    gate = jax.nn.sigmoid(jnp.dot(_rms(h, gple_ref[...]).astype(BF16), wpg_ref[...],
                                  preferred_element_type=F32))
    ple = jnp.dot(p_ref[...].astype(BF16), wpp_ref[...], preferred_element_type=F32)
    out_ref[...] = h + ple * gate


def _tail_call(x, mixed, p, w_out, g_ffn, w_gate, w_up, w_down, g_ple, w_pg, w_pp):
    n = x.shape[0]
    tm = TM_DENSE
    row = lambda w: pl.BlockSpec((tm, w), lambda i: (i, 0))
    return pl.pallas_call(
        _tail_kernel,
        grid=(n // tm,),
        in_specs=[row(D_MODEL), row(D_MODEL), row(PLE_DIM),
                  _const_spec((D_MODEL, D_MODEL)), _const_spec((1, D_MODEL)),
                  _const_spec((D_MODEL, D_FF)), _const_spec((D_MODEL, D_FF)),
                  _const_spec((D_FF, D_MODEL)), _const_spec((1, D_MODEL)),
                  _const_spec((D_MODEL, D_MODEL)), _const_spec((PLE_DIM, D_MODEL))],
        out_specs=row(D_MODEL),
        out_shape=jax.ShapeDtypeStruct((n, D_MODEL), F32),
        compiler_params=pltpu.CompilerParams(dimension_semantics=("arbitrary",),
                                             vmem_limit_bytes=VMEM_LIMIT),
        name="tail",
    )(x, mixed, p, w_out, g_ffn, w_gate, w_up, w_down, g_ple, w_pg, w_pp)


def _bias_rows(rel_bias):
    edge = jnp.broadcast_to(rel_bias[:, 2 * REL_CLIP:], (ATTN_HEADS, BIAS_LANES - 2 * REL_CLIP))
    return jnp.concatenate([edge, rel_bias[:, 2 * REL_CLIP:0:-1]], axis=1)


def kernel(x_prompt, x_sample, cache_attn_k, cache_attn_v, state_ret, p_prompt, p_sample,
           g_mix, w_in, g_ret_out, g_q_attn, g_k_attn, rel_bias, w_out, g_ffn,
           w_ffn_gate, w_ffn_up, w_ffn_down, g_ple, w_ple_gate, w_ple_proj):
    depth = w_in.shape[0]
    assert depth == 1, "single-layer trunk"
    batch, seq, _ = x_prompt.shape
    dec_batch, dec_seq, _ = x_sample.shape
    cache_w = cache_attn_k.shape[2]
    assert batch == 1 and seq % TM_DENSE == 0 and dec_seq == CHUNK and cache_w == BAND_PAST
    assert min(BAND_PAST, seq) == TM_DENSE, "the returned prompt keys are the last projection tile"

    row = lambda g: g.reshape(1, -1)
    gq = row(jnp.tile(g_q_attn[0], ATTN_HEADS))
    gk = row(jnp.tile(g_k_attn[0], ATTN_HEADS))
    gro = row(g_ret_out[0])
    vb = _bias_rows(rel_bias[0])
    half = RET_DK // 2
    inv = ROPE_THETA ** (-jnp.arange(half, dtype=F32) / half)
    inv = row(jnp.concatenate([inv, inv]))
    w_in_b = w_in[0].astype(BF16)
    tail_w_args = (w_out[0].astype(BF16), row(g_ffn[0]), w_ffn_gate[0].astype(BF16),
                   w_ffn_up[0].astype(BF16), w_ffn_down[0].astype(BF16), row(g_ple[0]),
                   w_ple_gate[0].astype(BF16), w_ple_proj[0].astype(BF16))

    xp = x_prompt.reshape(seq, D_MODEL)
    qr, kr, vr, sg, qa, qb, ka, va, k_tail, v_tail = _proj_call(
        xp, row(g_mix[0]), w_in_b, inv, gq, gk,
        pos0=0, pos_stride=TM_DENSE, period=TM_DENSE, tail_only=True)
    mixed_p, r_fin = _mix_prompt_call(qr, kr, vr, sg, qa, qb, ka, va, gro, vb)
    y_prompt = _tail_call(xp, mixed_p, p_prompt[0].reshape(seq, PLE_DIM), *tail_w_args)

    n_s = dec_batch * dec_seq
    xs = x_sample.reshape(n_s, D_MODEL)
    qr, kr, vr, sg, qa, qb, ka, va, k_new, v_new = _proj_call(
        xs, row(g_mix[0]), w_in_b, inv, gq, gk,
        pos0=PAST_LEN, pos_stride=0, period=dec_seq, tail_only=False)
    feat_major = lambda c: c.transpose(0, 2, 3, 1).reshape(dec_batch, ATT_W, cache_w)
    mixed_s, r_new = _mix_sample_call(
        qr, kr, vr, sg, qa, qb, ka, va,
        feat_major(cache_attn_k[0]), feat_major(cache_attn_v[0]),
        state_ret[0], gro, vb, dec_seq)
    y_sample = _tail_call(xs, mixed_s, p_sample[0].reshape(n_s, PLE_DIM), *tail_w_args)

    kv_p = lambda a: a.reshape(ATTN_HEADS, ATTN_DH, TM_DENSE).transpose(2, 0, 1).reshape(
        1, batch, TM_DENSE, ATTN_HEADS, ATTN_DH)
    kv_s = lambda a: a.reshape(1, dec_batch, dec_seq, ATTN_HEADS, ATTN_DH)
    return (y_prompt.reshape(batch, seq, D_MODEL),
            y_sample.reshape(dec_batch, dec_seq, D_MODEL),
            r_fin.reshape(1, batch, RET_HEADS, RET_DK, RET_DV),
            kv_p(k_tail), kv_p(v_tail),
            r_new.reshape(1, dec_batch, RET_HEADS, RET_DK, RET_DV),
            kv_s(k_new), kv_s(v_new))
```

```python
import functools
import math

import jax
import jax.numpy as jnp
from jax import lax
from jax.experimental import pallas as pl
from jax.experimental.pallas import tpu as pltpu

F32 = jnp.float32
BF16 = jnp.bfloat16

D_MODEL = 1024
CHUNK = 64
PAST_LEN = 2048
RET_HEADS = 4
RET_DK = 128
RET_DV = 128
ATTN_HEADS = 8
ATTN_DH = 64
BAND_PAST_CHUNKS = 8
BAND_PAST = BAND_PAST_CHUNKS * CHUNK
REL_CLIP = 256
D_FF = 2816
PLE_DIM = 256
ROPE_THETA = 10000.0
EPS = 1e-6
RET_W = RET_HEADS * RET_DK
ATT_W = ATTN_HEADS * ATTN_DH
IN_COLS = 4 * RET_W + 3 * ATT_W
LOG_DECAY = tuple(math.log1p(-(2.0 ** (-5 - h))) for h in range(RET_HEADS))
LOG2E = math.log2(math.e)
MASKED = -1e30

LANES = 128
SUBLANES = 8
BIAS_LANES = 1024
TM_DENSE = 512
TQ_PROMPT = 256
DENOM_ROWS = 16
SCORE_LOOKAHEAD = 3
VMEM_LIMIT = 56 * 1024 * 1024

_NT = (((1,), (1,)), ((), ()))
_TN = (((0,), (0,)), ((), ()))


def _rms(x, g):
    return x * lax.rsqrt(jnp.mean(x * x, axis=-1, keepdims=True) + EPS) * g


def _const_spec(shape):
    zeros = (0,) * len(shape)
    return pl.BlockSpec(shape, lambda i: zeros, pipeline_mode=pl.Buffered(1))


def _proj_kernel(x_ref, gmix_ref, w_ref, inv_ref, gq_ref, gk_ref,
                 qr_ref, kr_ref, vr_ref, sg_ref, qa_ref, qb_ref, ka_ref, vt_ref, ka32_ref, va32_ref,
                 cos_scr, sin_scr, *, pos0, pos_stride, period, prompt):
    i = pl.program_id(0)
    tm = x_ref.shape[0]
    inv = inv_ref[...]

    @pl.when(i == 0)
    def _init():
        r = lax.broadcasted_iota(jnp.int32, (tm, LANES), 0) % period
        ang = r.astype(F32) * inv
        cos_scr[...] = jnp.cos(ang)
        sin_scr[...] = jnp.sin(ang)

    base = jnp.full((1, LANES), pos0 + i * pos_stride, jnp.int32).astype(F32) * inv
    cb, sb = jnp.cos(base), jnp.sin(base)
    cos = cb * cos_scr[...] - sb * sin_scr[...]
    sin = sb * cos_scr[...] + cb * sin_scr[...]
    lane = lax.broadcasted_iota(jnp.int32, (tm, LANES), 1)
    first_half = lane < LANES // 2
    sin = jnp.where(first_half, -sin, sin)

    xn = _rms(x_ref[...], gmix_ref[...]).astype(BF16)

    def proj(group):
        return jnp.dot(xn, w_ref[:, group * RET_W:(group + 1) * RET_W], preferred_element_type=F32)

    def rope_store(z, out_ref, scale):
        for h in range(RET_HEADS):
            sl = slice(h * RET_DK, (h + 1) * RET_DK)
            zh = z[:, sl]
            r = zh * cos + pltpu.roll(zh, RET_DK // 2, 1) * sin
            if scale is not None:
                r = r * scale
            out_ref[:, sl] = r.astype(out_ref.dtype)

    def head_rms(z, g_ref):
        outs = []
        for j in range(ATT_W // LANES):
            sl = slice(j * LANES, (j + 1) * LANES)
            zj = z[:, sl]
            zz = zj * zj
            s0 = jnp.sum(jnp.where(first_half, zz, 0.0), axis=-1, keepdims=True)
            s1 = jnp.sum(jnp.where(first_half, 0.0, zz), axis=-1, keepdims=True)
            ms = jnp.where(first_half, s0, s1) * (1.0 / ATTN_DH)
            outs.append(zj * lax.rsqrt(ms + EPS) * g_ref[:, sl])
        return jnp.concatenate(outs, axis=1)

    rope_store(proj(0), qr_ref, None)
    rope_store(proj(1), kr_ref, RET_DK ** -0.5)
    vr_ref[...] = proj(2).astype(BF16)
    sg_ref[...] = jax.nn.silu(proj(3)).astype(BF16)
    qs = head_rms(proj(4), gq_ref) * (ATTN_DH ** -0.5 * LOG2E)
    if prompt:
        qs = qs.T
        pair_first = lax.broadcasted_iota(jnp.int32, qs.shape, 0) % LANES < ATTN_DH
    else:
        pair_first = jnp.concatenate([first_half] * (ATT_W // LANES), axis=1)
    qa_ref[...] = jnp.where(pair_first, qs, 0.0).astype(BF16)
    qb_ref[...] = jnp.where(pair_first, 0.0, qs).astype(BF16)
    ka = head_rms(proj(5), gk_ref)
    ka_ref[...] = ka.astype(BF16)
    va = proj(6)
    va_t = va.T
    vt_ref[...] = va_t.astype(BF16)

    if prompt:
        @pl.when(i == pl.num_programs(0) - 1)
        def _tail():
            ka32_ref[...] = ka.T
            va32_ref[...] = va_t
    else:
        ka32_ref[...] = ka
        va32_ref[...] = va


def _proj_call(x, g_mix, w_in, inv, gq, gk, *, pos0, pos_stride, period, prompt):
    n = x.shape[0]
    tm = TM_DENSE
    row = lambda w: pl.BlockSpec((tm, w), lambda i: (i, 0))
    col_spec = pl.BlockSpec((ATT_W, tm), lambda i: (0, i))
    bf = jax.ShapeDtypeStruct((n, RET_W), BF16)
    bf_t = jax.ShapeDtypeStruct((ATT_W, n), BF16)
    if prompt:
        assert tm == ATT_W
        q_shape, q_spec = bf_t, col_spec
        kv32 = jax.ShapeDtypeStruct((ATT_W, tm), F32)
        kv32_spec = pl.BlockSpec((ATT_W, tm), lambda i: (0, 0))
    else:
        q_shape, q_spec = bf, row(ATT_W)
        kv32 = jax.ShapeDtypeStruct((n, ATT_W), F32)
        kv32_spec = row(ATT_W)
    body = functools.partial(_proj_kernel, pos0=pos0, pos_stride=pos_stride, period=period,
                             prompt=prompt)
    return pl.pallas_call(
        body,
        grid=(n // tm,),
        in_specs=[row(D_MODEL), _const_spec((1, D_MODEL)), _const_spec((D_MODEL, IN_COLS)),
                  _const_spec((1, LANES)), _const_spec((1, ATT_W)), _const_spec((1, ATT_W))],
        out_specs=[row(RET_W)] * 4 + [q_spec] * 2 + [row(ATT_W), col_spec] + [kv32_spec] * 2,
        out_shape=[bf] * 4 + [q_shape] * 2 + [bf, bf_t] + [kv32] * 2,
        scratch_shapes=[pltpu.VMEM((tm, LANES), F32), pltpu.VMEM((tm, LANES), F32)],
        compiler_params=pltpu.CompilerParams(dimension_semantics=("arbitrary",),
                                             vmem_limit_bytes=VMEM_LIMIT),
        name="proj",
    )(x, g_mix, w_in, inv, gq, gk)


def _init_decay(dmat_scr, xi_scr, zeta_scr, t):
    n = lax.broadcasted_iota(jnp.int32, (t, t), 0)
    m = lax.broadcasted_iota(jnp.int32, (t, t), 1)
    diff = (n - m).astype(F32)
    pos = lax.broadcasted_iota(jnp.int32, (t, RET_DK), 0).astype(F32)
    for h in range(RET_HEADS):
        lg = LOG_DECAY[h]
        dmat_scr[h] = jnp.where(diff >= 0, jnp.exp(jnp.maximum(diff, 0.0) * lg), 0.0)
        sl = slice(h * RET_DK, (h + 1) * RET_DK)
        xi_scr[:, sl] = jnp.exp((pos + 1.0) * lg)
        zeta_scr[:, sl] = jnp.exp((t - 1.0 - pos) * lg)


def _toeplitz_rows(vec_row, first_shift):
    r = lax.broadcasted_iota(jnp.int32, (SUBLANES, BIAS_LANES), 0)
    base = jnp.broadcast_to(vec_row, (SUBLANES, BIAS_LANES)) * LOG2E
    base = pltpu.roll(base, first_shift, 1)
    for shift in (1, 2, 4):
        base = jnp.where((r & shift) != 0, pltpu.roll(base, shift, 1), base)
    return base


def _init_bias_keys(vb_ref, bias_scr):
    _, n_keys, t = bias_scr.shape
    lane = lax.broadcasted_iota(jnp.int32, (SUBLANES, t), 1)
    for h in range(ATTN_HEADS):
        base = _toeplitz_rows(vb_ref[h:h + 1, :], BIAS_LANES - n_keys)
        for g in range(n_keys // SUBLANES):
            rows = (pltpu.roll(base, SUBLANES * g, 1) if g else base)[:, :t]
            kc = SUBLANES * g // CHUNK
            lo = max(kc - BAND_PAST_CHUNKS, 0) * CHUNK
            hi = (min(kc, t // CHUNK - 1) + 1) * CHUNK
            rows = jnp.where(jnp.logical_and(lane >= lo, lane < hi), rows, MASKED)
            bias_scr[h, g * SUBLANES:(g + 1) * SUBLANES, :] = rows


def _init_bias_queries(vb_ref, bias_scr, t, w):
    first = lax.broadcasted_iota(jnp.int32, (SUBLANES, LANES), 1) < t
    for h in range(ATTN_HEADS):
        base = _toeplitz_rows(vb_ref[h:h + 1, :], BIAS_LANES - REL_CLIP)
        for g in range(t // SUBLANES):
            rows = pltpu.roll(base, SUBLANES * g, 1) if g else base
            new = rows[:, w:w + LANES]
            r0 = (h % 2) * t + g * SUBLANES
            bias_scr[h // 2, r0:r0 + SUBLANES, :w] = rows[:, :w]
            bias_scr[h // 2, r0:r0 + SUBLANES, w:w + LANES] = jnp.where(first, new, MASKED)
            bias_scr[h // 2, r0:r0 + SUBLANES, w + LANES:] = jnp.where(
                first, MASKED, pltpu.roll(new, t, 1))


def _retention(q, k, v, r, dmat, xi, zeta, decay_t):
    s = lax.dot_general(q, k, _NT, preferred_element_type=F32) * dmat
    inner = jnp.dot(s.astype(BF16), v, preferred_element_type=F32)
    cross = jnp.dot(q, r.astype(BF16), preferred_element_type=F32) * xi
    kz = (k.astype(F32) * zeta).astype(BF16)
    r_new = r * decay_t + lax.dot_general(kz, v, _TN, preferred_element_type=F32)
    return inner + cross, r_new


def _column_max(s):
    rows = s.shape[0]
    while rows % (2 * SUBLANES) == 0:
        rows //= 2
        s = jnp.maximum(s[:rows], s[rows:])
    return jnp.max(s, axis=0, keepdims=True)


def _mix_prompt_kernel(qr_ref, kr_ref, vr_ref, sg_ref, qa_ref, qb_ref, k0_ref, k1_ref, k2_ref,
                       v0_ref, v1_ref, v2_ref, gro_ref, vb_ref,
                       mixed_ref, rfin_ref,
                       r_scr, bias_scr, dmat_scr, xi_scr, zeta_scr):
    i = pl.program_id(0)
    t = TQ_PROMPT
    n_keys = bias_scr.shape[1]

    @pl.when(i == 0)
    def _init():
        r_scr[...] = jnp.zeros_like(r_scr)
        _init_decay(dmat_scr, xi_scr, zeta_scr, t)
        _init_bias_keys(vb_ref, bias_scr)

    def retention(h):
        sl = slice(h * RET_DK, (h + 1) * RET_DK)
        o, r_new = _retention(qr_ref[:, sl], kr_ref[:, sl], vr_ref[:, sl], r_scr[h],
                              dmat_scr[h], xi_scr[:, sl], zeta_scr[:, sl],
                              math.exp(t * LOG_DECAY[h]))
        r_scr[h] = r_new
        mixed_ref[:, sl] = (sg_ref[:, sl].astype(F32) * _rms(o, gro_ref[:, sl])).astype(BF16)

    ones = jnp.ones((DENOM_ROWS, n_keys), BF16)

    def scores(h, start_mask):
        sl = slice(h // 2 * LANES, (h // 2 + 1) * LANES)
        k = jnp.concatenate([k0_ref[:, sl], k1_ref[:, sl], k2_ref[:, sl]], axis=0)
        q_ref = qb_ref if h % 2 else qa_ref
        s = jnp.dot(k, q_ref[sl, :], preferred_element_type=F32) + bias_scr[h]
        return s if start_mask is None else s + start_mask

    def attention(start_mask):
        pending = [scores(h, start_mask) for h in range(SCORE_LOOKAHEAD)]
        halves = []
        for h in range(ATTN_HEADS):
            if h < RET_HEADS:
                retention(h)
            if h + SCORE_LOOKAHEAD < ATTN_HEADS:
                pending.append(scores(h + SCORE_LOOKAHEAD, start_mask))
            s = pending.pop(0)
            e = jnp.exp2(s - _column_max(s)).astype(BF16)
            rows = slice(h * ATTN_DH, (h + 1) * ATTN_DH)
            v_t = jnp.concatenate([v0_ref[rows, :], v1_ref[rows, :], v2_ref[rows, :]], axis=1)
            out = jnp.dot(jnp.concatenate([v_t, ones], axis=0), e, preferred_element_type=F32)
            halves.append(out[:ATTN_DH] / out[ATTN_DH:ATTN_DH + 1])
            if h % 2:
                j = h // 2
                o_t = jnp.concatenate(halves, axis=0)
                halves = []
                mixed_ref[:, RET_W + j * LANES:RET_W + (j + 1) * LANES] = o_t.T.astype(BF16)

    @pl.when(i < 2)
    def _start():
        blk = lax.broadcasted_iota(jnp.int32, (n_keys, t), 0) // t
        attention(jnp.where(blk + i >= 2, 0.0, MASKED))

    @pl.when(i >= 2)
    def _steady():
        attention(None)

    @pl.when(i == pl.num_programs(0) - 1)
    def _fin():
        rfin_ref[...] = r_scr[...]


def _mix_prompt_call(qr, kr, vr, sg, qa_t, qb_t, ka, va_t, gro, vb):
    n = qr.shape[0]
    t = TQ_PROMPT
    row = pl.BlockSpec((t, RET_W), lambda i: (i, 0))
    col = pl.BlockSpec((ATT_W, t), lambda i: (0, i))
    back = lambda d: pl.BlockSpec((t, ATT_W), lambda i: (jnp.maximum(i - d, 0), 0))
    back_t = lambda d: pl.BlockSpec((ATT_W, t), lambda i: (0, jnp.maximum(i - d, 0)))
    state = (RET_HEADS, RET_DK, RET_DV)
    return pl.pallas_call(
        _mix_prompt_kernel,
        grid=(n // t,),
        in_specs=[row] * 4 + [col] * 2 + [back(2), back(1), back(0)]
                 + [back_t(2), back_t(1), back_t(0)]
                 + [_const_spec((1, RET_W)), _const_spec((ATTN_HEADS, BIAS_LANES))],
        out_specs=[pl.BlockSpec((t, D_MODEL), lambda i: (i, 0)),
                   pl.BlockSpec(state, lambda i: (0, 0, 0))],
        out_shape=[jax.ShapeDtypeStruct((n, D_MODEL), BF16), jax.ShapeDtypeStruct(state, F32)],
        scratch_shapes=[pltpu.VMEM(state, F32),
                        pltpu.VMEM((ATTN_HEADS, 3 * t, t), F32),
                        pltpu.VMEM((RET_HEADS, t, t), F32),
                        pltpu.VMEM((t, RET_W), F32),
                        pltpu.VMEM((t, RET_W), F32)],
        compiler_params=pltpu.CompilerParams(dimension_semantics=("arbitrary",),
                                             vmem_limit_bytes=VMEM_LIMIT),
        name="mix_prompt",
    )(qr, kr, vr, sg, qa_t, qb_t, ka, ka, ka, va_t, va_t, va_t, gro, vb)


def _mix_sample_kernel(qr_ref, kr_ref, vr_ref, sg_ref, qa_ref, qb_ref, ka_ref, vt_ref, ck_ref, cv_ref,
                       r0_ref, gro_ref, vb_ref,
                       mixed_ref, rnew_ref,
                       bias_scr, dmat_scr, xi_scr, zeta_scr, *, t):
    streams, _, w = ck_ref.shape
    assert streams * t == LANES

    @pl.when(pl.program_id(0) == 0)
    def _init():
        _init_decay(dmat_scr, xi_scr, zeta_scr, t)
        _init_bias_queries(vb_ref, bias_scr, t, w)

    ones = jnp.ones((DENOM_ROWS, w), BF16)
    own_head = ((lax.broadcasted_iota(jnp.int32, (LANES, 2 * t), 0) < ATTN_DH)
                == (lax.broadcasted_iota(jnp.int32, (LANES, 2 * t), 1) < t))
    for b in range(streams):
        rows = slice(b * t, (b + 1) * t)
        for h in range(RET_HEADS):
            sl = slice(h * RET_DK, (h + 1) * RET_DK)
            o, r_new = _retention(qr_ref[rows, sl], kr_ref[rows, sl], vr_ref[rows, sl], r0_ref[b, h],
                                  dmat_scr[h], xi_scr[:, sl], zeta_scr[:, sl],
                                  math.exp(t * LOG_DECAY[h]))
            rnew_ref[b, h] = r_new
            mixed_ref[rows, sl] = (sg_ref[rows, sl].astype(F32)
                                   * _rms(o, gro_ref[:, sl])).astype(BF16)

        for j in range(ATTN_HEADS // 2):
            sl = slice(j * LANES, (j + 1) * LANES)
            q2 = jnp.concatenate([qa_ref[rows, sl], qb_ref[rows, sl]], axis=0)
            s_c = (jnp.dot(q2, ck_ref[b, sl, :].astype(BF16), preferred_element_type=F32)
                   + bias_scr[j, :, :w])
            s_n = (lax.dot_general(q2, ka_ref[:, sl], _NT, preferred_element_type=F32)
                   + bias_scr[j, :, w + b * LANES:w + (b + 1) * LANES])
            m = jnp.maximum(jnp.max(s_c, axis=-1, keepdims=True),
                            jnp.max(s_n, axis=-1, keepdims=True))
            e_c = jnp.exp2(s_c - m).astype(BF16)
            e_n = jnp.exp2(s_n - m).astype(BF16)
            lhs_c = jnp.concatenate([cv_ref[b, sl, :].astype(BF16), ones], axis=0)
            lhs_n = jnp.concatenate([vt_ref[sl, :], ones[:, :LANES]], axis=0)
            out = (lax.dot_general(lhs_c, e_c, _NT, preferred_element_type=F32)
                   + lax.dot_general(lhs_n, e_n, _NT, preferred_element_type=F32))
            x = out[:LANES] / out[LANES:LANES + 1]
            x_t = jnp.where(own_head, x, 0.0).T
            mixed_ref[rows, RET_W + j * LANES:RET_W + (j + 1) * LANES] = (
                x_t[:t] + x_t[t:]).astype(BF16)


def _mix_sample_call(qr, kr, vr, sg, qa, qb, ka, va_t, ck, cv, r0, gro, vb, t):
    n = qr.shape[0]
    nb, _, w = ck.shape
    g = LANES // t
    assert nb % g == 0
    row = pl.BlockSpec((g * t, RET_W), lambda b: (b, 0))
    col = pl.BlockSpec((ATT_W, g * t), lambda b: (0, b))
    cache = pl.BlockSpec((g, ATT_W, w), lambda b: (b, 0, 0))
    state = pl.BlockSpec((g, RET_HEADS, RET_DK, RET_DV), lambda b: (b, 0, 0, 0))
    return pl.pallas_call(
        functools.partial(_mix_sample_kernel, t=t),
        grid=(nb // g,),
        in_specs=[row] * 7 + [col, cache, cache, state,
                              _const_spec((1, RET_W)), _const_spec((ATTN_HEADS, BIAS_LANES))],
        out_specs=[pl.BlockSpec((g * t, D_MODEL), lambda b: (b, 0)), state],
        out_shape=[jax.ShapeDtypeStruct((n, D_MODEL), BF16),
                   jax.ShapeDtypeStruct((nb, RET_HEADS, RET_DK, RET_DV), F32)],
        scratch_shapes=[pltpu.VMEM((ATTN_HEADS // 2, 2 * t, w + g * LANES), F32),
                        pltpu.VMEM((RET_HEADS, t, t), F32),
                        pltpu.VMEM((t, RET_W), F32),
                        pltpu.VMEM((t, RET_W), F32)],
        compiler_params=pltpu.CompilerParams(dimension_semantics=("arbitrary",),
                                             vmem_limit_bytes=VMEM_LIMIT),
        name="mix_sample",
    )(qr, kr, vr, sg, qa, qb, ka, va_t, ck, cv, r0, gro, vb)


FF_SPLITS = (0, 1024, 2048, D_FF)


def _tail_kernel(x_ref, mixed_ref, p_ref, wout_ref, gffn_ref, wg_ref, wu_ref, wd_ref,
                 gple_ref, wpg_ref, wpp_ref, out_ref):
    h = x_ref[...] + jnp.dot(mixed_ref[...], wout_ref[...], preferred_element_type=F32)
    u = _rms(h, gffn_ref[...]).astype(BF16)
    ffn = None
    for c0, c1 in zip(FF_SPLITS[:-1], FF_SPLITS[1:]):
        a = jnp.dot(u, wg_ref[:, c0:c1], preferred_element_type=F32)
        b = jnp.dot(u, wu_ref[:, c0:c1], preferred_element_type=F32)
        part = jnp.dot((jax.nn.silu(a) * b).astype(BF16), wd_ref[c0:c1, :],
                       preferred_element_type=F32)
        ffn = part if ffn is None else ffn + part
    h = h + ffn
    gate = jax.nn.sigmoid(jnp.dot(_rms(h, gple_ref[...]).astype(BF16), wpg_ref[...],
                                  preferred_element_type=F32))
    ple = jnp.dot(p_ref[...].astype(BF16), wpp_ref[...], preferred_element_type=F32)
    out_ref[...] = h + ple * gate


def _tail_call(x, mixed, p, w_out, g_ffn, w_gate, w_up, w_down, g_ple, w_pg, w_pp):
    n = x.shape[0]
    tm = TM_DENSE
    row = lambda w: pl.BlockSpec((tm, w), lambda i: (i, 0))
    return pl.pallas_call(
        _tail_kernel,
        grid=(n // tm,),
        in_specs=[row(D_MODEL), row(D_MODEL), row(PLE_DIM),
                  _const_spec((D_MODEL, D_MODEL)), _const_spec((1, D_MODEL)),
                  _const_spec((D_MODEL, D_FF)), _const_spec((D_MODEL, D_FF)),
                  _const_spec((D_FF, D_MODEL)), _const_spec((1, D_MODEL)),
                  _const_spec((D_MODEL, D_MODEL)), _const_spec((PLE_DIM, D_MODEL))],
        out_specs=row(D_MODEL),
        out_shape=jax.ShapeDtypeStruct((n, D_MODEL), F32),
        compiler_params=pltpu.CompilerParams(dimension_semantics=("arbitrary",),
                                             vmem_limit_bytes=VMEM_LIMIT),
        name="tail",
    )(x, mixed, p, w_out, g_ffn, w_gate, w_up, w_down, g_ple, w_pg, w_pp)


def _bias_rows(rel_bias):
    edge = jnp.broadcast_to(rel_bias[:, 2 * REL_CLIP:], (ATTN_HEADS, BIAS_LANES - 2 * REL_CLIP))
    by_query = jnp.concatenate([edge, rel_bias[:, 2 * REL_CLIP:0:-1]], axis=1)
    by_key = jnp.concatenate([rel_bias[:, :2 * REL_CLIP], edge], axis=1)
    return by_query, by_key


def kernel(x_prompt, x_sample, cache_attn_k, cache_attn_v, state_ret, p_prompt, p_sample,
           g_mix, w_in, g_ret_out, g_q_attn, g_k_attn, rel_bias, w_out, g_ffn,
           w_ffn_gate, w_ffn_up, w_ffn_down, g_ple, w_ple_gate, w_ple_proj):
    depth = w_in.shape[0]
    assert depth == 1, "single-layer trunk"
    batch, seq, _ = x_prompt.shape
    dec_batch, dec_seq, _ = x_sample.shape
    cache_w = cache_attn_k.shape[2]
    assert batch == 1 and seq % TM_DENSE == 0 and dec_seq == CHUNK and cache_w == BAND_PAST
    assert min(BAND_PAST, seq) == TM_DENSE, "the returned prompt keys are the last projection tile"

    row = lambda g: g.reshape(1, -1)
    gq = row(jnp.tile(g_q_attn[0], ATTN_HEADS))
    gk = row(jnp.tile(g_k_attn[0], ATTN_HEADS))
    gro = row(g_ret_out[0])
    vb_query, vb_key = _bias_rows(rel_bias[0])
    half = RET_DK // 2
    inv = ROPE_THETA ** (-jnp.arange(half, dtype=F32) / half)
    inv = row(jnp.concatenate([inv, inv]))
    w_in_b = w_in[0].astype(BF16)
    tail_w_args = (w_out[0].astype(BF16), row(g_ffn[0]), w_ffn_gate[0].astype(BF16),
                   w_ffn_up[0].astype(BF16), w_ffn_down[0].astype(BF16), row(g_ple[0]),
                   w_ple_gate[0].astype(BF16), w_ple_proj[0].astype(BF16))

    xp = x_prompt.reshape(seq, D_MODEL)
    qr, kr, vr, sg, qa_t, qb_t, ka, va_t, k_tail, v_tail = _proj_call(
        xp, row(g_mix[0]), w_in_b, inv, gq, gk,
        pos0=0, pos_stride=TM_DENSE, period=TM_DENSE, prompt=True)
    mixed_p, r_fin = _mix_prompt_call(qr, kr, vr, sg, qa_t, qb_t, ka, va_t, gro, vb_key)
    y_prompt = _tail_call(xp, mixed_p, p_prompt[0].reshape(seq, PLE_DIM), *tail_w_args)

    n_s = dec_batch * dec_seq
    xs = x_sample.reshape(n_s, D_MODEL)
    qr, kr, vr, sg, qa, qb, ka, va_t, k_new, v_new = _proj_call(
        xs, row(g_mix[0]), w_in_b, inv, gq, gk,
        pos0=PAST_LEN, pos_stride=0, period=dec_seq, prompt=False)
    feat_major = lambda c: c.transpose(0, 2, 3, 1).reshape(dec_batch, ATT_W, cache_w)
    mixed_s, r_new = _mix_sample_call(
        qr, kr, vr, sg, qa, qb, ka, va_t,
        feat_major(cache_attn_k[0]), feat_major(cache_attn_v[0]),
        state_ret[0], gro, vb_query, dec_seq)
    y_sample = _tail_call(xs, mixed_s, p_sample[0].reshape(n_s, PLE_DIM), *tail_w_args)

    kv_p = lambda a: a.reshape(ATTN_HEADS, ATTN_DH, TM_DENSE).transpose(2, 0, 1).reshape(
        1, batch, TM_DENSE, ATTN_HEADS, ATTN_DH)
    kv_s = lambda a: a.reshape(1, dec_batch, dec_seq, ATTN_HEADS, ATTN_DH)
    return (y_prompt.reshape(batch, seq, D_MODEL),
            y_sample.reshape(dec_batch, dec_seq, D_MODEL),
            r_fin.reshape(1, batch, RET_HEADS, RET_DK, RET_DV),
            kv_p(k_tail), kv_p(v_tail),
            r_new.reshape(1, dec_batch, RET_HEADS, RET_DK, RET_DV),
            kv_s(k_new), kv_s(v_new))
```

```python
import functools
import math

import jax
import jax.numpy as jnp
from jax import lax
from jax.experimental import pallas as pl
from jax.experimental.pallas import tpu as pltpu

F32 = jnp.float32
BF16 = jnp.bfloat16

D_MODEL = 1024
CHUNK = 64
PAST_LEN = 2048
RET_HEADS = 4
RET_DK = 128
RET_DV = 128
ATTN_HEADS = 8
ATTN_DH = 64
BAND_PAST_CHUNKS = 8
BAND_PAST = BAND_PAST_CHUNKS * CHUNK
REL_CLIP = 256
D_FF = 2816
PLE_DIM = 256
ROPE_THETA = 10000.0
EPS = 1e-6
RET_W = RET_HEADS * RET_DK
ATT_W = ATTN_HEADS * ATTN_DH
IN_COLS = 4 * RET_W + 3 * ATT_W
LOG_DECAY = tuple(math.log1p(-(2.0 ** (-5 - h))) for h in range(RET_HEADS))
LOG2E = math.log2(math.e)
MASKED = -1e30

LANES = 128
SUBLANES = 8
BF16_ROWS = 16
BIAS_LANES = 1024
TM_DENSE = 512
TQ_PROMPT = 256
DENOM_ROWS = BF16_ROWS
SCORE_LOOKAHEAD = 3
VMEM_LIMIT = 56 * 1024 * 1024

_NT = (((1,), (1,)), ((), ()))
_TN = (((0,), (0,)), ((), ()))


def _rms(x, g):
    return x * lax.rsqrt(jnp.mean(x * x, axis=-1, keepdims=True) + EPS) * g


def _const_spec(shape):
    zeros = (0,) * len(shape)
    return pl.BlockSpec(shape, lambda i: zeros, pipeline_mode=pl.Buffered(1))


def _proj_kernel(*refs, n_cast, pos0, pos_stride, period, prompt):
    x_ref, gmix_ref, w_ref, inv_ref, gq_ref, gk_ref = refs[:6]
    cast_in = refs[6:6 + n_cast]
    (qr_ref, kr_ref, vr_ref, sg_ref, qa_ref, qb_ref, ka_ref, vt_ref,
     ka32_ref, va32_ref) = refs[6 + n_cast:16 + n_cast]
    cast_out = refs[16 + n_cast:16 + 2 * n_cast]
    cos_scr, sin_scr = refs[16 + 2 * n_cast:]

    i = pl.program_id(0)
    tm = x_ref.shape[0]
    inv = inv_ref[...]

    @pl.when(i == 0)
    def _init():
        r = lax.broadcasted_iota(jnp.int32, (tm, LANES), 0) % period
        ang = r.astype(F32) * inv
        cos_scr[...] = jnp.cos(ang)
        sin_scr[...] = jnp.sin(ang)

    base = jnp.full((1, LANES), pos0 + i * pos_stride, jnp.int32).astype(F32) * inv
    cb, sb = jnp.cos(base), jnp.sin(base)
    cos = cb * cos_scr[...] - sb * sin_scr[...]
    sin = sb * cos_scr[...] + cb * sin_scr[...]
    lane = lax.broadcasted_iota(jnp.int32, (tm, LANES), 1)
    first_half = lane < LANES // 2
    sin = jnp.where(first_half, -sin, sin)

    xn = _rms(x_ref[...], gmix_ref[...]).astype(BF16)
    for src_ref, dst_ref in zip(cast_in, cast_out):
        dst_ref[...] = src_ref[...].astype(BF16)

    def proj(group):
        return jnp.dot(xn, w_ref[:, group * RET_W:(group + 1) * RET_W], preferred_element_type=F32)

    def rope_store(z, out_ref, scale):
        for h in range(RET_HEADS):
            sl = slice(h * RET_DK, (h + 1) * RET_DK)
            zh = z[:, sl]
            r = zh * cos + pltpu.roll(zh, RET_DK // 2, 1) * sin
            if scale is not None:
                r = r * scale
            out_ref[:, sl] = r.astype(out_ref.dtype)

    def head_rms(z, g_ref):
        outs = []
        for j in range(ATT_W // LANES):
            sl = slice(j * LANES, (j + 1) * LANES)
            zj = z[:, sl]
            zz = zj * zj
            s0 = jnp.sum(jnp.where(first_half, zz, 0.0), axis=-1, keepdims=True)
            s1 = jnp.sum(jnp.where(first_half, 0.0, zz), axis=-1, keepdims=True)
            ms = jnp.where(first_half, s0, s1) * (1.0 / ATTN_DH)
            outs.append(zj * lax.rsqrt(ms + EPS) * g_ref[:, sl])
        return jnp.concatenate(outs, axis=1)

    qs = head_rms(proj(4), gq_ref) * (ATTN_DH ** -0.5 * LOG2E)
    if prompt:
        qs = qs.T
        pair_first = lax.broadcasted_iota(jnp.int32, qs.shape, 0) % LANES < ATTN_DH
    else:
        pair_first = jnp.concatenate([first_half] * (ATT_W // LANES), axis=1)
    qa_ref[...] = jnp.where(pair_first, qs, 0.0).astype(BF16)
    qb_ref[...] = jnp.where(pair_first, 0.0, qs).astype(BF16)
    ka = head_rms(proj(5), gk_ref)
    ka_ref[...] = ka.astype(BF16)
    va = proj(6)
    va_t = va.T
    vt_ref[...] = va_t.astype(BF16)
    rope_store(proj(0), qr_ref, None)
    rope_store(proj(1), kr_ref, RET_DK ** -0.5)
    sg_ref[...] = jax.nn.silu(proj(3)).astype(BF16)
    vr_ref[...] = proj(2).astype(BF16)

    if prompt:
        @pl.when(i == pl.num_programs(0) - 1)
        def _tail():
            ka32_ref[...] = ka.T
            va32_ref[...] = va_t
    else:
        ka32_ref[...] = ka
        va32_ref[...] = va


def _cast_block_rows(rows, steps):
    block = BF16_ROWS
    while rows % block or block * steps < rows:
        block += BF16_ROWS
    return block


def _proj_call(x, g_mix, w_in, inv, gq, gk, *, pos0, pos_stride, period, prompt, cast_weights=()):
    n = x.shape[0]
    tm = TM_DENSE
    cast_specs, cast_shapes = [], []
    for w in cast_weights:
        rows, cols = w.shape
        block = _cast_block_rows(rows, n // tm)
        last = rows // block - 1
        cast_specs.append(pl.BlockSpec((block, cols), lambda i, last=last: (jnp.minimum(i, last), 0)))
        cast_shapes.append(jax.ShapeDtypeStruct(w.shape, BF16))
    row = lambda w: pl.BlockSpec((tm, w), lambda i: (i, 0))
    col_spec = pl.BlockSpec((ATT_W, tm), lambda i: (0, i))
    bf = jax.ShapeDtypeStruct((n, RET_W), BF16)
    bf_t = jax.ShapeDtypeStruct((ATT_W, n), BF16)
    if prompt:
        assert tm == ATT_W
        q_shape, q_spec = bf_t, col_spec
        kv32 = jax.ShapeDtypeStruct((ATT_W, tm), F32)
        kv32_spec = pl.BlockSpec((ATT_W, tm), lambda i: (0, 0))
    else:
        q_shape, q_spec = bf, row(ATT_W)
        kv32 = jax.ShapeDtypeStruct((n, ATT_W), F32)
        kv32_spec = row(ATT_W)
    body = functools.partial(_proj_kernel, n_cast=len(cast_weights), pos0=pos0,
                             pos_stride=pos_stride, period=period, prompt=prompt)
    return pl.pallas_call(
        body,
        grid=(n // tm,),
        in_specs=[row(D_MODEL), _const_spec((1, D_MODEL)), _const_spec((D_MODEL, IN_COLS)),
                  _const_spec((1, LANES)), _const_spec((1, ATT_W)), _const_spec((1, ATT_W))]
                 + cast_specs,
        out_specs=[row(RET_W)] * 4 + [q_spec] * 2 + [row(ATT_W), col_spec] + [kv32_spec] * 2
                  + cast_specs,
        out_shape=[bf] * 4 + [q_shape] * 2 + [bf, bf_t] + [kv32] * 2 + cast_shapes,
        scratch_shapes=[pltpu.VMEM((tm, LANES), F32), pltpu.VMEM((tm, LANES), F32)],
        compiler_params=pltpu.CompilerParams(dimension_semantics=("arbitrary",),
                                             vmem_limit_bytes=VMEM_LIMIT),
        name="proj",
    )(x, g_mix, w_in, inv, gq, gk, *cast_weights)


def _init_decay(dmat_scr, xi_scr, zeta_scr, t):
    n = lax.broadcasted_iota(jnp.int32, (t, t), 0)
    m = lax.broadcasted_iota(jnp.int32, (t, t), 1)
    diff = (n - m).astype(F32)
    pos = lax.broadcasted_iota(jnp.int32, (t, RET_DK), 0).astype(F32)
    for h in range(RET_HEADS):
        lg = LOG_DECAY[h]
        dmat_scr[h] = jnp.where(diff >= 0, jnp.exp(jnp.maximum(diff, 0.0) * lg), 0.0)
        sl = slice(h * RET_DK, (h + 1) * RET_DK)
        xi_scr[:, sl] = jnp.exp((pos + 1.0) * lg)
        zeta_scr[:, sl] = jnp.exp((t - 1.0 - pos) * lg)


def _toeplitz_rows(vec_row, first_shift):
    r = lax.broadcasted_iota(jnp.int32, (SUBLANES, BIAS_LANES), 0)
    base = jnp.broadcast_to(vec_row, (SUBLANES, BIAS_LANES)) * LOG2E
    base = pltpu.roll(base, first_shift, 1)
    for shift in (1, 2, 4):
        base = jnp.where((r & shift) != 0, pltpu.roll(base, shift, 1), base)
    return base


def _init_bias_keys(vb_ref, bias_scr):
    _, n_keys, t = bias_scr.shape
    lane = lax.broadcasted_iota(jnp.int32, (SUBLANES, t), 1)
    for h in range(ATTN_HEADS):
        base = _toeplitz_rows(vb_ref[h:h + 1, :], BIAS_LANES - n_keys)
        for g in range(n_keys // SUBLANES):
            rows = (pltpu.roll(base, SUBLANES * g, 1) if g else base)[:, :t]
            kc = SUBLANES * g // CHUNK
            lo = max(kc - BAND_PAST_CHUNKS, 0) * CHUNK
            hi = (min(kc, t // CHUNK - 1) + 1) * CHUNK
            rows = jnp.where(jnp.logical_and(lane >= lo, lane < hi), rows, MASKED)
            bias_scr[h, g * SUBLANES:(g + 1) * SUBLANES, :] = rows


def _init_bias_queries(vb_ref, bias_scr, t, w):
    first = lax.broadcasted_iota(jnp.int32, (SUBLANES, LANES), 1) < t
    for h in range(ATTN_HEADS):
        base = _toeplitz_rows(vb_ref[h:h + 1, :], BIAS_LANES - REL_CLIP)
        for g in range(t // SUBLANES):
            rows = pltpu.roll(base, SUBLANES * g, 1) if g else base
            new = rows[:, w:w + LANES]
            r0 = (h % 2) * t + g * SUBLANES
            bias_scr[h // 2, r0:r0 + SUBLANES, :w] = rows[:, :w]
            bias_scr[h // 2, r0:r0 + SUBLANES, w:w + LANES] = jnp.where(first, new, MASKED)
            bias_scr[h // 2, r0:r0 + SUBLANES, w + LANES:] = jnp.where(
                first, MASKED, pltpu.roll(new, t, 1))


def _retention(q, k, v, r, dmat, xi, zeta, decay_t):
    s = lax.dot_general(q, k, _NT, preferred_element_type=F32) * dmat
    inner = jnp.dot(s.astype(BF16), v, preferred_element_type=F32)
    cross = jnp.dot(q, r.astype(BF16), preferred_element_type=F32) * xi
    kz = (k.astype(F32) * zeta).astype(BF16)
    r_new = r * decay_t + lax.dot_general(kz, v, _TN, preferred_element_type=F32)
    return inner + cross, r_new


def _column_max(s):
    rows = s.shape[0]
    while rows % (2 * SUBLANES) == 0:
        rows //= 2
        s = jnp.maximum(s[:rows], s[rows:])
    return jnp.max(s, axis=0, keepdims=True)


def _mix_prompt_kernel(qr_ref, kr_ref, vr_ref, sg_ref, qa_ref, qb_ref, k0_ref, k1_ref, k2_ref,
                       v0_ref, v1_ref, v2_ref, gro_ref, vb_ref,
                       mixed_ref, rfin_ref,
                       r_scr, bias_scr, dmat_scr, xi_scr, zeta_scr):
    i = pl.program_id(0)
    t = TQ_PROMPT
    n_keys = bias_scr.shape[1]

    @pl.when(i == 0)
    def _init():
        r_scr[...] = jnp.zeros_like(r_scr)
        _init_decay(dmat_scr, xi_scr, zeta_scr, t)
        _init_bias_keys(vb_ref, bias_scr)

    def retention(h):
        sl = slice(h * RET_DK, (h + 1) * RET_DK)
        o, r_new = _retention(qr_ref[:, sl], kr_ref[:, sl], vr_ref[:, sl], r_scr[h],
                              dmat_scr[h], xi_scr[:, sl], zeta_scr[:, sl],
                              math.exp(t * LOG_DECAY[h]))
        r_scr[h] = r_new
        mixed_ref[:, sl] = (sg_ref[:, sl].astype(F32) * _rms(o, gro_ref[:, sl])).astype(BF16)

    ones = jnp.ones((DENOM_ROWS, n_keys), BF16)

    def scores(h, start_mask):
        sl = slice(h // 2 * LANES, (h // 2 + 1) * LANES)
        k = jnp.concatenate([k0_ref[:, sl], k1_ref[:, sl], k2_ref[:, sl]], axis=0)
        q_ref = qb_ref if h % 2 else qa_ref
        s = jnp.dot(k, q_ref[sl, :], preferred_element_type=F32) + bias_scr[h]
        return s if start_mask is None else s + start_mask

    def attention(start_mask):
        pending = [scores(h, start_mask) for h in range(SCORE_LOOKAHEAD)]
        halves = []
        for h in range(ATTN_HEADS):
            if h < RET_HEADS:
                retention(h)
            if h + SCORE_LOOKAHEAD < ATTN_HEADS:
                pending.append(scores(h + SCORE_LOOKAHEAD, start_mask))
            s = pending.pop(0)
            e = jnp.exp2(s - _column_max(s)).astype(BF16)
            rows = slice(h * ATTN_DH, (h + 1) * ATTN_DH)
            v_t = jnp.concatenate([v0_ref[rows, :], v1_ref[rows, :], v2_ref[rows, :]], axis=1)
            out = jnp.dot(jnp.concatenate([v_t, ones], axis=0), e, preferred_element_type=F32)
            halves.append(out[:ATTN_DH] / out[ATTN_DH:ATTN_DH + 1])
            if h % 2:
                j = h // 2
                o_t = jnp.concatenate(halves, axis=0)
                halves = []
                mixed_ref[:, RET_W + j * LANES:RET_W + (j + 1) * LANES] = o_t.T.astype(BF16)

    @pl.when(i < 2)
    def _start():
        blk = lax.broadcasted_iota(jnp.int32, (n_keys, t), 0) // t
        attention(jnp.where(blk + i >= 2, 0.0, MASKED))

    @pl.when(i >= 2)
    def _steady():
        attention(None)

    @pl.when(i == pl.num_programs(0) - 1)
    def _fin():
        rfin_ref[...] = r_scr[...]


def _mix_prompt_call(qr, kr, vr, sg, qa_t, qb_t, ka, va_t, gro, vb):
    n = qr.shape[0]
    t = TQ_PROMPT
    row = pl.BlockSpec((t, RET_W), lambda i: (i, 0))
    col = pl.BlockSpec((ATT_W, t), lambda i: (0, i))
    back = lambda d: pl.BlockSpec((t, ATT_W), lambda i: (jnp.maximum(i - d, 0), 0))
    back_t = lambda d: pl.BlockSpec((ATT_W, t), lambda i: (0, jnp.maximum(i - d, 0)))
    state = (RET_HEADS, RET_DK, RET_DV)
    return pl.pallas_call(
        _mix_prompt_kernel,
        grid=(n // t,),
        in_specs=[row] * 4 + [col] * 2 + [back(2), back(1), back(0)]
                 + [back_t(2), back_t(1), back_t(0)]
                 + [_const_spec((1, RET_W)), _const_spec((ATTN_HEADS, BIAS_LANES))],
        out_specs=[pl.BlockSpec((t, D_MODEL), lambda i: (i, 0)),
                   pl.BlockSpec(state, lambda i: (0, 0, 0))],
        out_shape=[jax.ShapeDtypeStruct((n, D_MODEL), BF16), jax.ShapeDtypeStruct(state, F32)],
        scratch_shapes=[pltpu.VMEM(state, F32),
                        pltpu.VMEM((ATTN_HEADS, 3 * t, t), F32),
                        pltpu.VMEM((RET_HEADS, t, t), F32),
                        pltpu.VMEM((t, RET_W), F32),
                        pltpu.VMEM((t, RET_W), F32)],
        compiler_params=pltpu.CompilerParams(dimension_semantics=("arbitrary",),
                                             vmem_limit_bytes=VMEM_LIMIT),
        name="mix_prompt",
    )(qr, kr, vr, sg, qa_t, qb_t, ka, ka, ka, va_t, va_t, va_t, gro, vb)


def _mix_sample_kernel(qr_ref, kr_ref, vr_ref, sg_ref, qa_ref, qb_ref, ka_ref, vt_ref, ck_ref, cv_ref,
                       r0_ref, gro_ref, vb_ref,
                       mixed_ref, rnew_ref,
                       bias_scr, dmat_scr, xi_scr, zeta_scr, *, t):
    streams, _, w = ck_ref.shape
    assert streams * t == LANES

    @pl.when(pl.program_id(0) == 0)
    def _init():
        _init_decay(dmat_scr, xi_scr, zeta_scr, t)
        _init_bias_queries(vb_ref, bias_scr, t, w)

    ones = jnp.ones((DENOM_ROWS, w), BF16)
    own_head = ((lax.broadcasted_iota(jnp.int32, (LANES, 2 * t), 0) < ATTN_DH)
                == (lax.broadcasted_iota(jnp.int32, (LANES, 2 * t), 1) < t))
    for b in range(streams):
        rows = slice(b * t, (b + 1) * t)
        for h in range(RET_HEADS):
            sl = slice(h * RET_DK, (h + 1) * RET_DK)
            o, r_new = _retention(qr_ref[rows, sl], kr_ref[rows, sl], vr_ref[rows, sl], r0_ref[b, h],
                                  dmat_scr[h], xi_scr[:, sl], zeta_scr[:, sl],
                                  math.exp(t * LOG_DECAY[h]))
            rnew_ref[b, h] = r_new
            mixed_ref[rows, sl] = (sg_ref[rows, sl].astype(F32)
                                   * _rms(o, gro_ref[:, sl])).astype(BF16)

        for j in range(ATTN_HEADS // 2):
            sl = slice(j * LANES, (j + 1) * LANES)
            q2 = jnp.concatenate([qa_ref[rows, sl], qb_ref[rows, sl]], axis=0)
            s_c = (jnp.dot(q2, ck_ref[b, sl, :].astype(BF16), preferred_element_type=F32)
                   + bias_scr[j, :, :w])
            s_n = (lax.dot_general(q2, ka_ref[:, sl], _NT, preferred_element_type=F32)
                   + bias_scr[j, :, w + b * LANES:w + (b + 1) * LANES])
            m = jnp.maximum(jnp.max(s_c, axis=-1, keepdims=True),
                            jnp.max(s_n, axis=-1, keepdims=True))
            e_c = jnp.exp2(s_c - m).astype(BF16)
            e_n = jnp.exp2(s_n - m).astype(BF16)
            lhs_c = jnp.concatenate([cv_ref[b, sl, :].astype(BF16), ones], axis=0)
            lhs_n = jnp.concatenate([vt_ref[sl, :], ones[:, :LANES]], axis=0)
            out = (lax.dot_general(lhs_c, e_c, _NT, preferred_element_type=F32)
                   + lax.dot_general(lhs_n, e_n, _NT, preferred_element_type=F32))
            x = out[:LANES] / out[LANES:LANES + 1]
            x_t = jnp.where(own_head, x, 0.0).T
            mixed_ref[rows, RET_W + j * LANES:RET_W + (j + 1) * LANES] = (
                x_t[:t] + x_t[t:]).astype(BF16)


def _mix_sample_call(qr, kr, vr, sg, qa, qb, ka, va_t, ck, cv, r0, gro, vb, t):
    n = qr.shape[0]
    nb, _, w = ck.shape
    g = LANES // t
    assert nb % g == 0
    row = pl.BlockSpec((g * t, RET_W), lambda b: (b, 0))
    col = pl.BlockSpec((ATT_W, g * t), lambda b: (0, b))
    cache = pl.BlockSpec((g, ATT_W, w), lambda b: (b, 0, 0))
    state = pl.BlockSpec((g, RET_HEADS, RET_DK, RET_DV), lambda b: (b, 0, 0, 0))
    return pl.pallas_call(
        functools.partial(_mix_sample_kernel, t=t),
        grid=(nb // g,),
        in_specs=[row] * 7 + [col, cache, cache, state,
                              _const_spec((1, RET_W)), _const_spec((ATTN_HEADS, BIAS_LANES))],
        out_specs=[pl.BlockSpec((g * t, D_MODEL), lambda b: (b, 0)), state],
        out_shape=[jax.ShapeDtypeStruct((n, D_MODEL), BF16),
                   jax.ShapeDtypeStruct((nb, RET_HEADS, RET_DK, RET_DV), F32)],
        scratch_shapes=[pltpu.VMEM((ATTN_HEADS // 2, 2 * t, w + g * LANES), F32),
                        pltpu.VMEM((RET_HEADS, t, t), F32),
                        pltpu.VMEM((t, RET_W), F32),
                        pltpu.VMEM((t, RET_W), F32)],
        compiler_params=pltpu.CompilerParams(dimension_semantics=("arbitrary",),
                                             vmem_limit_bytes=VMEM_LIMIT),
        name="mix_sample",
    )(qr, kr, vr, sg, qa, qb, ka, va_t, ck, cv, r0, gro, vb)


FF_SPLITS = (0, 1024, 2048, D_FF)


def _tail_kernel(x_ref, mixed_ref, p_ref, wout_ref, gffn_ref, wg_ref, wu_ref, wd_ref,
                 gple_ref, wpg_ref, wpp_ref, out_ref):
    h = x_ref[...] + jnp.dot(mixed_ref[...], wout_ref[...], preferred_element_type=F32)
    u = _rms(h, gffn_ref[...]).astype(BF16)
    ffn = None
    for c0, c1 in zip(FF_SPLITS[:-1], FF_SPLITS[1:]):
        a = jnp.dot(u, wg_ref[:, c0:c1], preferred_element_type=F32)
        b = jnp.dot(u, wu_ref[:, c0:c1], preferred_element_type=F32)
        part = jnp.dot((jax.nn.silu(a) * b).astype(BF16), wd_ref[c0:c1, :],
                       preferred_element_type=F32)
        ffn = part if ffn is None else ffn + part
    h = h + ffn
    gate = jax.nn.sigmoid(jnp.dot(_rms(h, gple_ref[...]).astype(BF16), wpg_ref[...],
                                  preferred_element_type=F32))
    ple = jnp.dot(p_ref[...].astype(BF16), wpp_ref[...], preferred_element_type=F32)
    out_ref[...] = h + ple * gate


def _tail_call(x, mixed, p, w_out, g_ffn, w_gate, w_up, w_down, g_ple, w_pg, w_pp):
    n = x.shape[0]
    tm = TM_DENSE
    row = lambda w: pl.BlockSpec((tm, w), lambda i: (i, 0))
    return pl.pallas_call(
        _tail_kernel,
        grid=(n // tm,),
        in_specs=[row(D_MODEL), row(D_MODEL), row(PLE_DIM),
                  _const_spec((D_MODEL, D_MODEL)), _const_spec((1, D_MODEL)),
                  _const_spec((D_MODEL, D_FF)), _const_spec((D_MODEL, D_FF)),
                  _const_spec((D_FF, D_MODEL)), _const_spec((1, D_MODEL)),
                  _const_spec((D_MODEL, D_MODEL)), _const_spec((PLE_DIM, D_MODEL))],
        out_specs=row(D_MODEL),
        out_shape=jax.ShapeDtypeStruct((n, D_MODEL), F32),
        compiler_params=pltpu.CompilerParams(dimension_semantics=("arbitrary",),
                                             vmem_limit_bytes=VMEM_LIMIT),
        name="tail",
    )(x, mixed, p, w_out, g_ffn, w_gate, w_up, w_down, g_ple, w_pg, w_pp)


def _bias_rows(rel_bias):
    edge = jnp.broadcast_to(rel_bias[:, 2 * REL_CLIP:], (ATTN_HEADS, BIAS_LANES - 2 * REL_CLIP))
    by_query = jnp.concatenate([edge, rel_bias[:, 2 * REL_CLIP:0:-1]], axis=1)
    by_key = jnp.concatenate([rel_bias[:, :2 * REL_CLIP], edge], axis=1)
    return by_query, by_key


def kernel(x_prompt, x_sample, cache_attn_k, cache_attn_v, state_ret, p_prompt, p_sample,
           g_mix, w_in, g_ret_out, g_q_attn, g_k_attn, rel_bias, w_out, g_ffn,
           w_ffn_gate, w_ffn_up, w_ffn_down, g_ple, w_ple_gate, w_ple_proj):
    depth = w_in.shape[0]
    assert depth == 1, "single-layer trunk"
    batch, seq, _ = x_prompt.shape
    dec_batch, dec_seq, _ = x_sample.shape
    cache_w = cache_attn_k.shape[2]
    assert batch == 1 and seq % TM_DENSE == 0 and dec_seq == CHUNK and cache_w == BAND_PAST
    assert min(BAND_PAST, seq) == TM_DENSE, "the returned prompt keys are the last projection tile"

    row = lambda g: g.reshape(1, -1)
    gq = row(jnp.tile(g_q_attn[0], ATTN_HEADS))
    gk = row(jnp.tile(g_k_attn[0], ATTN_HEADS))
    gro = row(g_ret_out[0])
    vb_query, vb_key = _bias_rows(rel_bias[0])
    half = RET_DK // 2
    inv = ROPE_THETA ** (-jnp.arange(half, dtype=F32) / half)
    inv = row(jnp.concatenate([inv, inv]))
    w_in_b = w_in[0].astype(BF16)

    xp = x_prompt.reshape(seq, D_MODEL)
    (qr, kr, vr, sg, qa_t, qb_t, ka, va_t, k_tail, v_tail,
     w_out_b, w_gate_b, w_up_b, w_down_b, w_pg_b, w_pp_b) = _proj_call(
        xp, row(g_mix[0]), w_in_b, inv, gq, gk,
        pos0=0, pos_stride=TM_DENSE, period=TM_DENSE, prompt=True,
        cast_weights=(w_out[0], w_ffn_gate[0], w_ffn_up[0], w_ffn_down[0],
                      w_ple_gate[0], w_ple_proj[0]))
    tail_w_args = (w_out_b, row(g_ffn[0]), w_gate_b, w_up_b, w_down_b, row(g_ple[0]),
                   w_pg_b, w_pp_b)
    mixed_p, r_fin = _mix_prompt_call(qr, kr, vr, sg, qa_t, qb_t, ka, va_t, gro, vb_key)
    y_prompt = _tail_call(xp, mixed_p, p_prompt[0].reshape(seq, PLE_DIM), *tail_w_args)

    n_s = dec_batch * dec_seq
    xs = x_sample.reshape(n_s, D_MODEL)
    qr, kr, vr, sg, qa, qb, ka, va_t, k_new, v_new = _proj_call(
        xs, row(g_mix[0]), w_in_b, inv, gq, gk,
        pos0=PAST_LEN, pos_stride=0, period=dec_seq, prompt=False)
    feat_major = lambda c: c.transpose(0, 2, 3, 1).reshape(dec_batch, ATT_W, cache_w)
    mixed_s, r_new = _mix_sample_call(
        qr, kr, vr, sg, qa, qb, ka, va_t,
        feat_major(cache_attn_k[0]), feat_major(cache_attn_v[0]),
        state_ret[0], gro, vb_query, dec_seq)
    y_sample = _tail_call(xs, mixed_s, p_sample[0].reshape(n_s, PLE_DIM), *tail_w_args)

    kv_p = lambda a: a.reshape(ATTN_HEADS, ATTN_DH, TM_DENSE).transpose(2, 0, 1).reshape(
        1, batch, TM_DENSE, ATTN_HEADS, ATTN_DH)
    kv_s = lambda a: a.reshape(1, dec_batch, dec_seq, ATTN_HEADS, ATTN_DH)
    return (y_prompt.reshape(batch, seq, D_MODEL),
            y_sample.reshape(dec_batch, dec_seq, D_MODEL),
            r_fin.reshape(1, batch, RET_HEADS, RET_DK, RET_DV),
            kv_p(k_tail), kv_p(v_tail),
            r_new.reshape(1, dec_batch, RET_HEADS, RET_DK, RET_DV),
            kv_s(k_new), kv_s(v_new))
```

```python
import functools
import math

import jax
import jax.numpy as jnp
from jax import lax
from jax.experimental import pallas as pl
from jax.experimental.pallas import tpu as pltpu

F32 = jnp.float32
BF16 = jnp.bfloat16

D_MODEL = 1024
CHUNK = 64
PAST_LEN = 2048
RET_HEADS = 4
RET_DK = 128
RET_DV = 128
ATTN_HEADS = 8
ATTN_DH = 64
BAND_PAST_CHUNKS = 8
BAND_PAST = BAND_PAST_CHUNKS * CHUNK
REL_CLIP = 256
D_FF = 2816
PLE_DIM = 256
ROPE_THETA = 10000.0
EPS = 1e-6
RET_W = RET_HEADS * RET_DK
ATT_W = ATTN_HEADS * ATTN_DH
IN_COLS = 4 * RET_W + 3 * ATT_W
LOG_DECAY = tuple(math.log1p(-(2.0 ** (-5 - h))) for h in range(RET_HEADS))
LOG2E = math.log2(math.e)
MASKED = -1e30

LANES = 128
SUBLANES = 8
BF16_ROWS = 16
BIAS_LANES = 1024
TM_DENSE = 512
TQ_PROMPT = 256
DENOM_ROWS = BF16_ROWS
SAMPLE_LOOKAHEAD = 3
SCORE_LOOKAHEAD = 3
VMEM_LIMIT = 56 * 1024 * 1024

_NT = (((1,), (1,)), ((), ()))
_TN = (((0,), (0,)), ((), ()))


def _rms(x, g):
    return x * lax.rsqrt(jnp.mean(x * x, axis=-1, keepdims=True) + EPS) * g


def _const_spec(shape):
    zeros = (0,) * len(shape)
    return pl.BlockSpec(shape, lambda i: zeros, pipeline_mode=pl.Buffered(1))


def _proj_kernel(*refs, n_cast, pos0, pos_stride, period, prompt):
    x_ref, gmix_ref, w_ref, inv_ref, gq_ref, gk_ref = refs[:6]
    cast_in = refs[6:6 + n_cast]
    (qr_ref, kr_ref, vr_ref, sg_ref, qa_ref, qb_ref, ka_ref, vt_ref,
     ka32_ref, va32_ref) = refs[6 + n_cast:16 + n_cast]
    cast_out = refs[16 + n_cast:16 + 2 * n_cast]
    cos_scr, sin_scr = refs[16 + 2 * n_cast:]

    i = pl.program_id(0)
    tm = x_ref.shape[0]
    inv = inv_ref[...]

    @pl.when(i == 0)
    def _init():
        r = lax.broadcasted_iota(jnp.int32, (tm, LANES), 0) % period
        ang = r.astype(F32) * inv
        cos_scr[...] = jnp.cos(ang)
        sin_scr[...] = jnp.sin(ang)

    base = jnp.full((1, LANES), pos0 + i * pos_stride, jnp.int32).astype(F32) * inv
    cb, sb = jnp.cos(base), jnp.sin(base)
    cos = cb * cos_scr[...] - sb * sin_scr[...]
    sin = sb * cos_scr[...] + cb * sin_scr[...]
    lane = lax.broadcasted_iota(jnp.int32, (tm, LANES), 1)
    first_half = lane < LANES // 2
    sin = jnp.where(first_half, -sin, sin)

    xn = _rms(x_ref[...], gmix_ref[...]).astype(BF16)
    for src_ref, dst_ref in zip(cast_in, cast_out):
        dst_ref[...] = src_ref[...].astype(BF16)

    def proj(group):
        return jnp.dot(xn, w_ref[:, group * RET_W:(group + 1) * RET_W], preferred_element_type=F32)

    def rope_store(z, out_ref, scale):
        for h in range(RET_HEADS):
            sl = slice(h * RET_DK, (h + 1) * RET_DK)
            zh = z[:, sl]
            r = zh * cos + pltpu.roll(zh, RET_DK // 2, 1) * sin
            if scale is not None:
                r = r * scale
            out_ref[:, sl] = r.astype(out_ref.dtype)

    def head_rms(z, g_ref):
        outs = []
        for j in range(ATT_W // LANES):
            sl = slice(j * LANES, (j + 1) * LANES)
            zj = z[:, sl]
            zz = zj * zj
            s0 = jnp.sum(jnp.where(first_half, zz, 0.0), axis=-1, keepdims=True)
            s1 = jnp.sum(jnp.where(first_half, 0.0, zz), axis=-1, keepdims=True)
            ms = jnp.where(first_half, s0, s1) * (1.0 / ATTN_DH)
            outs.append(zj * lax.rsqrt(ms + EPS) * g_ref[:, sl])
        return jnp.concatenate(outs, axis=1)

    qs = head_rms(proj(4), gq_ref) * (ATTN_DH ** -0.5 * LOG2E)
    if prompt:
        qs = qs.T
        pair_first = lax.broadcasted_iota(jnp.int32, qs.shape, 0) % LANES < ATTN_DH
    else:
        pair_first = jnp.concatenate([first_half] * (ATT_W // LANES), axis=1)
    qa_ref[...] = jnp.where(pair_first, qs, 0.0).astype(BF16)
    qb_ref[...] = jnp.where(pair_first, 0.0, qs).astype(BF16)
    ka = head_rms(proj(5), gk_ref)
    ka_ref[...] = ka.astype(BF16)
    va = proj(6)
    va_t = va.T
    vt_ref[...] = va_t.astype(BF16)
    rope_store(proj(0), qr_ref, None)
    rope_store(proj(1), kr_ref, RET_DK ** -0.5)
    sg_ref[...] = jax.nn.silu(proj(3)).astype(BF16)
    vr_ref[...] = proj(2).astype(BF16)

    if prompt:
        @pl.when(i == pl.num_programs(0) - 1)
        def _tail():
            ka32_ref[...] = ka.T
            va32_ref[...] = va_t
    else:
        ka32_ref[...] = ka
        va32_ref[...] = va


def _cast_block_rows(rows, steps):
    block = BF16_ROWS
    while rows % block or block * steps < rows:
        block += BF16_ROWS
    return block


def _proj_call(x, g_mix, w_in, inv, gq, gk, *, pos0, pos_stride, period, prompt, cast_weights=()):
    n = x.shape[0]
    tm = TM_DENSE
    cast_specs, cast_shapes = [], []
    for w in cast_weights:
        rows, cols = w.shape
        block = _cast_block_rows(rows, n // tm)
        last = rows // block - 1
        cast_specs.append(pl.BlockSpec((block, cols), lambda i, last=last: (jnp.minimum(i, last), 0)))
        cast_shapes.append(jax.ShapeDtypeStruct(w.shape, BF16))
    row = lambda w: pl.BlockSpec((tm, w), lambda i: (i, 0))
    col_spec = pl.BlockSpec((ATT_W, tm), lambda i: (0, i))
    bf = jax.ShapeDtypeStruct((n, RET_W), BF16)
    bf_t = jax.ShapeDtypeStruct((ATT_W, n), BF16)
    if prompt:
        assert tm == ATT_W
        q_shape, q_spec = bf_t, col_spec
        kv32 = jax.ShapeDtypeStruct((ATT_W, tm), F32)
        kv32_spec = pl.BlockSpec((ATT_W, tm), lambda i: (0, 0))
    else:
        q_shape, q_spec = bf, row(ATT_W)
        kv32 = jax.ShapeDtypeStruct((n, ATT_W), F32)
        kv32_spec = row(ATT_W)
    body = functools.partial(_proj_kernel, n_cast=len(cast_weights), pos0=pos0,
                             pos_stride=pos_stride, period=period, prompt=prompt)
    return pl.pallas_call(
        body,
        grid=(n // tm,),
        in_specs=[row(D_MODEL), _const_spec((1, D_MODEL)), _const_spec((D_MODEL, IN_COLS)),
                  _const_spec((1, LANES)), _const_spec((1, ATT_W)), _const_spec((1, ATT_W))]
                 + cast_specs,
        out_specs=[row(RET_W)] * 4 + [q_spec] * 2 + [row(ATT_W), col_spec] + [kv32_spec] * 2
                  + cast_specs,
        out_shape=[bf] * 4 + [q_shape] * 2 + [bf, bf_t] + [kv32] * 2 + cast_shapes,
        scratch_shapes=[pltpu.VMEM((tm, LANES), F32), pltpu.VMEM((tm, LANES), F32)],
        compiler_params=pltpu.CompilerParams(dimension_semantics=("arbitrary",),
                                             vmem_limit_bytes=VMEM_LIMIT),
        name="proj",
    )(x, g_mix, w_in, inv, gq, gk, *cast_weights)


def _init_decay(dmat_scr, xi_scr, zeta_scr, t):
    n = lax.broadcasted_iota(jnp.int32, (t, t), 0)
    m = lax.broadcasted_iota(jnp.int32, (t, t), 1)
    diff = (n - m).astype(F32)
    pos = lax.broadcasted_iota(jnp.int32, (t, RET_DK), 0).astype(F32)
    for h in range(RET_HEADS):
        lg = LOG_DECAY[h]
        dmat_scr[h] = jnp.where(diff >= 0, jnp.exp(jnp.maximum(diff, 0.0) * lg), 0.0)
        sl = slice(h * RET_DK, (h + 1) * RET_DK)
        xi_scr[:, sl] = jnp.exp((pos + 1.0) * lg)
        zeta_scr[:, sl] = jnp.exp((t - 1.0 - pos) * lg)


def _toeplitz_rows(vb_ref, first_shift):
    r = lax.broadcasted_iota(jnp.int32, (ATTN_HEADS * SUBLANES, BIAS_LANES), 0)
    base = jnp.concatenate([jnp.broadcast_to(vb_ref[h:h + 1, :], (SUBLANES, BIAS_LANES))
                            for h in range(ATTN_HEADS)], axis=0) * LOG2E
    base = pltpu.roll(base, first_shift, 1)
    for shift in (1, 2, 4):
        base = jnp.where((r & shift) != 0, pltpu.roll(base, shift, 1), base)
    return [base[h * SUBLANES:(h + 1) * SUBLANES] for h in range(ATTN_HEADS)]


def _init_bias_keys(vb_ref, bias_scr):
    _, n_keys, t = bias_scr.shape
    lane = lax.broadcasted_iota(jnp.int32, (SUBLANES, t), 1)
    for h, base in enumerate(_toeplitz_rows(vb_ref, BIAS_LANES - n_keys)):
        for g in range(n_keys // SUBLANES):
            rows = (pltpu.roll(base, SUBLANES * g, 1) if g else base)[:, :t]
            kc = SUBLANES * g // CHUNK
            lo = max(kc - BAND_PAST_CHUNKS, 0) * CHUNK
            hi = (min(kc, t // CHUNK - 1) + 1) * CHUNK
            rows = jnp.where(jnp.logical_and(lane >= lo, lane < hi), rows, MASKED)
            bias_scr[h, g * SUBLANES:(g + 1) * SUBLANES, :] = rows


def _init_bias_queries(vb_ref, bias_scr, t, w):
    first = lax.broadcasted_iota(jnp.int32, (SUBLANES, LANES), 1) < t
    for h, base in enumerate(_toeplitz_rows(vb_ref, BIAS_LANES - REL_CLIP)):
        for g in range(t // SUBLANES):
            rows = pltpu.roll(base, SUBLANES * g, 1) if g else base
            shifted = pltpu.roll(base, SUBLANES * g + t, 1)
            r0 = (h % 2) * t + g * SUBLANES
            bias_scr[h // 2, r0:r0 + SUBLANES, :w] = rows[:, :w]
            bias_scr[h // 2, r0:r0 + SUBLANES, w:w + LANES] = jnp.where(
                first, rows[:, w:w + LANES], MASKED)
            bias_scr[h // 2, r0:r0 + SUBLANES, w + LANES:] = jnp.where(
                first, MASKED, shifted[:, w:w + LANES])


def _retention(q, k, v, r, dmat, xi, zeta, decay_t):
    s = lax.dot_general(q, k, _NT, preferred_element_type=F32) * dmat
    inner = jnp.dot(s.astype(BF16), v, preferred_element_type=F32)
    cross = jnp.dot(q, r.astype(BF16), preferred_element_type=F32) * xi
    kz = (k.astype(F32) * zeta).astype(BF16)
    r_new = r * decay_t + lax.dot_general(kz, v, _TN, preferred_element_type=F32)
    return inner + cross, r_new


def _column_max(s):
    rows = s.shape[0]
    while rows % (2 * SUBLANES) == 0:
        rows //= 2
        s = jnp.maximum(s[:rows], s[rows:])
    return jnp.max(s, axis=0, keepdims=True)


def _mix_prompt_kernel(qr_ref, kr_ref, vr_ref, sg_ref, qa_ref, qb_ref, k0_ref, k1_ref, k2_ref,
                       v0_ref, v1_ref, v2_ref, gro_ref, vb_ref,
                       mixed_ref, rfin_ref,
                       r_scr, bias_scr, dmat_scr, xi_scr, zeta_scr):
    i = pl.program_id(0)
    t = TQ_PROMPT
    n_keys = bias_scr.shape[1]

    @pl.when(i == 0)
    def _init():
        r_scr[...] = jnp.zeros_like(r_scr)
        _init_decay(dmat_scr, xi_scr, zeta_scr, t)
        _init_bias_keys(vb_ref, bias_scr)

    def retention(h):
        sl = slice(h * RET_DK, (h + 1) * RET_DK)
        o, r_new = _retention(qr_ref[:, sl], kr_ref[:, sl], vr_ref[:, sl], r_scr[h],
                              dmat_scr[h], xi_scr[:, sl], zeta_scr[:, sl],
                              math.exp(t * LOG_DECAY[h]))
        r_scr[h] = r_new
        mixed_ref[:, sl] = (sg_ref[:, sl].astype(F32) * _rms(o, gro_ref[:, sl])).astype(BF16)

    ones = jnp.ones((DENOM_ROWS, n_keys), BF16)

    def scores(h, start_mask):
        sl = slice(h // 2 * LANES, (h // 2 + 1) * LANES)
        k = jnp.concatenate([k0_ref[:, sl], k1_ref[:, sl], k2_ref[:, sl]], axis=0)
        q_ref = qb_ref if h % 2 else qa_ref
        s = jnp.dot(k, q_ref[sl, :], preferred_element_type=F32) + bias_scr[h]
        return s if start_mask is None else s + start_mask

    def attention(start_mask):
        pending = [scores(h, start_mask) for h in range(SCORE_LOOKAHEAD)]
        halves = []
        for h in range(ATTN_HEADS):
            if h < RET_HEADS:
                retention(h)
            if h + SCORE_LOOKAHEAD < ATTN_HEADS:
                pending.append(scores(h + SCORE_LOOKAHEAD, start_mask))
            s = pending.pop(0)
            e = jnp.exp2(s - _column_max(s)).astype(BF16)
            rows = slice(h * ATTN_DH, (h + 1) * ATTN_DH)
            v_t = jnp.concatenate([v0_ref[rows, :], v1_ref[rows, :], v2_ref[rows, :]], axis=1)
            out = jnp.dot(jnp.concatenate([v_t, ones], axis=0), e, preferred_element_type=F32)
            halves.append(out[:ATTN_DH] / out[ATTN_DH:ATTN_DH + 1])
            if h % 2:
                j = h // 2
                o_t = jnp.concatenate(halves, axis=0)
                halves = []
                mixed_ref[:, RET_W + j * LANES:RET_W + (j + 1) * LANES] = o_t.T.astype(BF16)

    @pl.when(i < 2)
    def _start():
        blk = lax.broadcasted_iota(jnp.int32, (n_keys, t), 0) // t
        attention(jnp.where(blk + i >= 2, 0.0, MASKED))

    @pl.when(i >= 2)
    def _steady():
        attention(None)

    @pl.when(i == pl.num_programs(0) - 1)
    def _fin():
        rfin_ref[...] = r_scr[...]


def _mix_prompt_call(qr, kr, vr, sg, qa_t, qb_t, ka, va_t, gro, vb):
    n = qr.shape[0]
    t = TQ_PROMPT
    row = pl.BlockSpec((t, RET_W), lambda i: (i, 0))
    col = pl.BlockSpec((ATT_W, t), lambda i: (0, i))
    back = lambda d: pl.BlockSpec((t, ATT_W), lambda i: (jnp.maximum(i - d, 0), 0))
    back_t = lambda d: pl.BlockSpec((ATT_W, t), lambda i: (0, jnp.maximum(i - d, 0)))
    state = (RET_HEADS, RET_DK, RET_DV)
    return pl.pallas_call(
        _mix_prompt_kernel,
        grid=(n // t,),
        in_specs=[row] * 4 + [col] * 2 + [back(2), back(1), back(0)]
                 + [back_t(2), back_t(1), back_t(0)]
                 + [_const_spec((1, RET_W)), _const_spec((ATTN_HEADS, BIAS_LANES))],
        out_specs=[pl.BlockSpec((t, D_MODEL), lambda i: (i, 0)),
                   pl.BlockSpec(state, lambda i: (0, 0, 0))],
        out_shape=[jax.ShapeDtypeStruct((n, D_MODEL), BF16), jax.ShapeDtypeStruct(state, F32)],
        scratch_shapes=[pltpu.VMEM(state, F32),
                        pltpu.VMEM((ATTN_HEADS, 3 * t, t), F32),
                        pltpu.VMEM((RET_HEADS, t, t), F32),
                        pltpu.VMEM((t, RET_W), F32),
                        pltpu.VMEM((t, RET_W), F32)],
        compiler_params=pltpu.CompilerParams(dimension_semantics=("arbitrary",),
                                             vmem_limit_bytes=VMEM_LIMIT),
        name="mix_prompt",
    )(qr, kr, vr, sg, qa_t, qb_t, ka, ka, ka, va_t, va_t, va_t, gro, vb)


def _mix_sample_kernel(qr_ref, kr_ref, vr_ref, sg_ref, qa_ref, qb_ref, ka_ref, vt_ref, ck_ref, cv_ref,
                       r0_ref, gro_ref, vb_ref,
                       mixed_ref, rnew_ref,
                       bias_scr, dmat_scr, xi_scr, zeta_scr, *, t):
    streams, _, w = ck_ref.shape
    assert streams * t == LANES

    @pl.when(pl.program_id(0) == 0)
    def _init():
        _init_decay(dmat_scr, xi_scr, zeta_scr, t)
        _init_bias_queries(vb_ref, bias_scr, t, w)

    ones = jnp.ones((DENOM_ROWS, w), BF16)
    own_head = ((lax.broadcasted_iota(jnp.int32, (LANES, 2 * t), 0) < ATTN_DH)
                == (lax.broadcasted_iota(jnp.int32, (LANES, 2 * t), 1) < t))

    def retention(b, h):
        rows = slice(b * t, (b + 1) * t)
        sl = slice(h * RET_DK, (h + 1) * RET_DK)
        o, r_new = _retention(qr_ref[rows, sl], kr_ref[rows, sl], vr_ref[rows, sl], r0_ref[b, h],
                              dmat_scr[h], xi_scr[:, sl], zeta_scr[:, sl],
                              math.exp(t * LOG_DECAY[h]))
        rnew_ref[b, h] = r_new
        mixed_ref[rows, sl] = (sg_ref[rows, sl].astype(F32) * _rms(o, gro_ref[:, sl])).astype(BF16)

    def scores(b, j):
        rows = slice(b * t, (b + 1) * t)
        sl = slice(j * LANES, (j + 1) * LANES)
        q2 = jnp.concatenate([qa_ref[rows, sl], qb_ref[rows, sl]], axis=0)
        s_c = (jnp.dot(q2, ck_ref[b, sl, :].astype(BF16), preferred_element_type=F32)
               + bias_scr[j, :, :w])
        s_n = (lax.dot_general(q2, ka_ref[:, sl], _NT, preferred_element_type=F32)
               + bias_scr[j, :, w + b * LANES:w + (b + 1) * LANES])
        return s_c, s_n

    def finish(b, j, s_c, s_n):
        rows = slice(b * t, (b + 1) * t)
        sl = slice(j * LANES, (j + 1) * LANES)
        m = jnp.maximum(jnp.max(s_c, axis=-1, keepdims=True),
                        jnp.max(s_n, axis=-1, keepdims=True))
        e_c = jnp.exp2(s_c - m).astype(BF16)
        e_n = jnp.exp2(s_n - m).astype(BF16)
        lhs_c = jnp.concatenate([cv_ref[b, sl, :].astype(BF16), ones], axis=0)
        lhs_n = jnp.concatenate([vt_ref[sl, :], ones[:, :LANES]], axis=0)
        out = (lax.dot_general(lhs_c, e_c, _NT, preferred_element_type=F32)
               + lax.dot_general(lhs_n, e_n, _NT, preferred_element_type=F32))
        x = out[:LANES] / out[LANES:LANES + 1]
        x_t = jnp.where(own_head, x, 0.0).T
        mixed_ref[rows, RET_W + j * LANES:RET_W + (j + 1) * LANES] = (
            x_t[:t] + x_t[t:]).astype(BF16)

    items = [(b, j) for b in range(streams) for j in range(ATTN_HEADS // 2)]
    ret_items = [(b, h) for b in range(streams) for h in range(RET_HEADS)]
    pending = [scores(*it) for it in items[:SAMPLE_LOOKAHEAD]]
    for n, (b, j) in enumerate(items):
        if n < len(ret_items):
            retention(*ret_items[n])
        if n + SAMPLE_LOOKAHEAD < len(items):
            pending.append(scores(*items[n + SAMPLE_LOOKAHEAD]))
        finish(b, j, *pending.pop(0))


def _mix_sample_call(qr, kr, vr, sg, qa, qb, ka, va_t, ck, cv, r0, gro, vb, t):
    n = qr.shape[0]
    nb, _, w = ck.shape
    g = LANES // t
    assert nb % g == 0
    row = pl.BlockSpec((g * t, RET_W), lambda b: (b, 0))
    col = pl.BlockSpec((ATT_W, g * t), lambda b: (0, b))
    cache = pl.BlockSpec((g, ATT_W, w), lambda b: (b, 0, 0))
    state = pl.BlockSpec((g, RET_HEADS, RET_DK, RET_DV), lambda b: (b, 0, 0, 0))
    return pl.pallas_call(
        functools.partial(_mix_sample_kernel, t=t),
        grid=(nb // g,),
        in_specs=[row] * 7 + [col, cache, cache, state,
                              _const_spec((1, RET_W)), _const_spec((ATTN_HEADS, BIAS_LANES))],
        out_specs=[pl.BlockSpec((g * t, D_MODEL), lambda b: (b, 0)), state],
        out_shape=[jax.ShapeDtypeStruct((n, D_MODEL), BF16),
                   jax.ShapeDtypeStruct((nb, RET_HEADS, RET_DK, RET_DV), F32)],
        scratch_shapes=[pltpu.VMEM((ATTN_HEADS // 2, 2 * t, w + g * LANES), F32),
                        pltpu.VMEM((RET_HEADS, t, t), F32),
                        pltpu.VMEM((t, RET_W), F32),
                        pltpu.VMEM((t, RET_W), F32)],
        compiler_params=pltpu.CompilerParams(dimension_semantics=("arbitrary",),
                                             vmem_limit_bytes=VMEM_LIMIT),
        name="mix_sample",
    )(qr, kr, vr, sg, qa, qb, ka, va_t, ck, cv, r0, gro, vb)


FF_SPLITS = (0, 1024, 2048, D_FF)


def _tail_kernel(x_ref, mixed_ref, p_ref, wout_ref, gffn_ref, wg_ref, wu_ref, wd_ref,
                 gple_ref, wpg_ref, wpp_ref, out_ref):
    h = x_ref[...] + jnp.dot(mixed_ref[...], wout_ref[...], preferred_element_type=F32)
    u = _rms(h, gffn_ref[...]).astype(BF16)
    ffn = None
    for c0, c1 in zip(FF_SPLITS[:-1], FF_SPLITS[1:]):
        a = jnp.dot(u, wg_ref[:, c0:c1], preferred_element_type=F32)
        b = jnp.dot(u, wu_ref[:, c0:c1], preferred_element_type=F32)
        part = jnp.dot((jax.nn.silu(a) * b).astype(BF16), wd_ref[c0:c1, :],
                       preferred_element_type=F32)
        ffn = part if ffn is None else ffn + part
    h = h + ffn
    gate = jax.nn.sigmoid(jnp.dot(_rms(h, gple_ref[...]).astype(BF16), wpg_ref[...],
                                  preferred_element_type=F32))
    ple = jnp.dot(p_ref[...].astype(BF16), wpp_ref[...], preferred_element_type=F32)
    out_ref[...] = h + ple * gate


def _tail_call(x, mixed, p, w_out, g_ffn, w_gate, w_up, w_down, g_ple, w_pg, w_pp):
    n = x.shape[0]
    tm = TM_DENSE
    row = lambda w: pl.BlockSpec((tm, w), lambda i: (i, 0))
    return pl.pallas_call(
        _tail_kernel,
        grid=(n // tm,),
        in_specs=[row(D_MODEL), row(D_MODEL), row(PLE_DIM),
                  _const_spec((D_MODEL, D_MODEL)), _const_spec((1, D_MODEL)),
                  _const_spec((D_MODEL, D_FF)), _const_spec((D_MODEL, D_FF)),
                  _const_spec((D_FF, D_MODEL)), _const_spec((1, D_MODEL)),
                  _const_spec((D_MODEL, D_MODEL)), _const_spec((PLE_DIM, D_MODEL))],
        out_specs=row(D_MODEL),
        out_shape=jax.ShapeDtypeStruct((n, D_MODEL), F32),
        compiler_params=pltpu.CompilerParams(dimension_semantics=("arbitrary",),
                                             vmem_limit_bytes=VMEM_LIMIT),
        name="tail",
    )(x, mixed, p, w_out, g_ffn, w_gate, w_up, w_down, g_ple, w_pg, w_pp)


def _bias_rows(rel_bias):
    edge = jnp.broadcast_to(rel_bias[:, 2 * REL_CLIP:], (ATTN_HEADS, BIAS_LANES - 2 * REL_CLIP))
    by_query = jnp.concatenate([edge, rel_bias[:, 2 * REL_CLIP:0:-1]], axis=1)
    by_key = jnp.concatenate([rel_bias[:, :2 * REL_CLIP], edge], axis=1)
    return by_query, by_key


def kernel(x_prompt, x_sample, cache_attn_k, cache_attn_v, state_ret, p_prompt, p_sample,
           g_mix, w_in, g_ret_out, g_q_attn, g_k_attn, rel_bias, w_out, g_ffn,
           w_ffn_gate, w_ffn_up, w_ffn_down, g_ple, w_ple_gate, w_ple_proj):
    depth = w_in.shape[0]
    assert depth == 1, "single-layer trunk"
    batch, seq, _ = x_prompt.shape
    dec_batch, dec_seq, _ = x_sample.shape
    cache_w = cache_attn_k.shape[2]
    assert batch == 1 and seq % TM_DENSE == 0 and dec_seq == CHUNK and cache_w == BAND_PAST
    assert min(BAND_PAST, seq) == TM_DENSE, "the returned prompt keys are the last projection tile"

    row = lambda g: g.reshape(1, -1)
    gq = row(jnp.tile(g_q_attn[0], ATTN_HEADS))
    gk = row(jnp.tile(g_k_attn[0], ATTN_HEADS))
    gro = row(g_ret_out[0])
    vb_query, vb_key = _bias_rows(rel_bias[0])
    half = RET_DK // 2
    inv = ROPE_THETA ** (-jnp.arange(half, dtype=F32) / half)
    inv = row(jnp.concatenate([inv, inv]))
    w_in_b = w_in[0].astype(BF16)

    xp = x_prompt.reshape(seq, D_MODEL)
    (qr, kr, vr, sg, qa_t, qb_t, ka, va_t, k_tail, v_tail,
     w_out_b, w_gate_b, w_up_b, w_down_b, w_pg_b, w_pp_b) = _proj_call(
        xp, row(g_mix[0]), w_in_b, inv, gq, gk,
        pos0=0, pos_stride=TM_DENSE, period=TM_DENSE, prompt=True,
        cast_weights=(w_out[0], w_ffn_gate[0], w_ffn_up[0], w_ffn_down[0],
                      w_ple_gate[0], w_ple_proj[0]))
    tail_w_args = (w_out_b, row(g_ffn[0]), w_gate_b, w_up_b, w_down_b, row(g_ple[0]),
                   w_pg_b, w_pp_b)
    mixed_p, r_fin = _mix_prompt_call(qr, kr, vr, sg, qa_t, qb_t, ka, va_t, gro, vb_key)
    y_prompt = _tail_call(xp, mixed_p, p_prompt[0].reshape(seq, PLE_DIM), *tail_w_args)

    n_s = dec_batch * dec_seq
    xs = x_sample.reshape(n_s, D_MODEL)
    qr, kr, vr, sg, qa, qb, ka, va_t, k_new, v_new = _proj_call(
        xs, row(g_mix[0]), w_in_b, inv, gq, gk,
        pos0=PAST_LEN, pos_stride=0, period=dec_seq, prompt=False)
    feat_major = lambda c: c.transpose(0, 2, 3, 1).reshape(dec_batch, ATT_W, cache_w)
    mixed_s, r_new = _mix_sample_call(
        qr, kr, vr, sg, qa, qb, ka, va_t,
        feat_major(cache_attn_k[0]), feat_major(cache_attn_v[0]),
        state_ret[0], gro, vb_query, dec_seq)
    y_sample = _tail_call(xs, mixed_s, p_sample[0].reshape(n_s, PLE_DIM), *tail_w_args)

    kv_p = lambda a: a.reshape(ATTN_HEADS, ATTN_DH, TM_DENSE).transpose(2, 0, 1).reshape(
        1, batch, TM_DENSE, ATTN_HEADS, ATTN_DH)
    kv_s = lambda a: a.reshape(1, dec_batch, dec_seq, ATTN_HEADS, ATTN_DH)
    return (y_prompt.reshape(batch, seq, D_MODEL),
            y_sample.reshape(dec_batch, dec_seq, D_MODEL),
            r_fin.reshape(1, batch, RET_HEADS, RET_DK, RET_DV),
            kv_p(k_tail), kv_p(v_tail),
            r_new.reshape(1, dec_batch, RET_HEADS, RET_DK, RET_DV),
            kv_s(k_new), kv_s(v_new))
```

```python
import functools
import math

import jax
import jax.numpy as jnp
from jax import lax
from jax.experimental import pallas as pl
from jax.experimental.pallas import tpu as pltpu

F32 = jnp.float32
BF16 = jnp.bfloat16

D_MODEL = 1024
CHUNK = 64
PAST_LEN = 2048
RET_HEADS = 4
RET_DK = 128
RET_DV = 128
ATTN_HEADS = 8
ATTN_DH = 64
BAND_PAST_CHUNKS = 8
BAND_PAST = BAND_PAST_CHUNKS * CHUNK
REL_CLIP = 256
D_FF = 2816
PLE_DIM = 256
ROPE_THETA = 10000.0
EPS = 1e-6
RET_W = RET_HEADS * RET_DK
ATT_W = ATTN_HEADS * ATTN_DH
IN_COLS = 4 * RET_W + 3 * ATT_W
LOG_DECAY = tuple(math.log1p(-(2.0 ** (-5 - h))) for h in range(RET_HEADS))
LOG2E = math.log2(math.e)
MASKED = -1e30

LANES = 128
SUBLANES = 8
BF16_ROWS = 16
BIAS_LANES = 1024
TM_PROJ = 1024
TM_DENSE = 512
TQ_PROMPT = 256
DENOM_ROWS = BF16_ROWS
SAMPLE_LOOKAHEAD = 3
SCORE_LOOKAHEAD = 3
VMEM_LIMIT = 56 * 1024 * 1024

_NT = (((1,), (1,)), ((), ()))
_TN = (((0,), (0,)), ((), ()))


def _rms(x, g):
    return x * lax.rsqrt(jnp.mean(x * x, axis=-1, keepdims=True) + EPS) * g


def _const_spec(shape):
    zeros = (0,) * len(shape)
    return pl.BlockSpec(shape, lambda i: zeros, pipeline_mode=pl.Buffered(1))


def _proj_kernel(*refs, n_cast, pos0, pos_stride, period, prompt):
    x_ref, gmix_ref, w_ref, inv_ref, gq_ref, gk_ref = refs[:6]
    cast_in = refs[6:6 + n_cast]
    (qr_ref, kr_ref, vr_ref, sg_ref, qa_ref, qb_ref, ka_ref, vt_ref,
     ka32_ref, va32_ref) = refs[6 + n_cast:16 + n_cast]
    cast_out = refs[16 + n_cast:16 + 2 * n_cast]
    cos_scr, sin_scr = refs[16 + 2 * n_cast:]

    i = pl.program_id(0)
    tm = x_ref.shape[0]
    inv = inv_ref[...]

    @pl.when(i == 0)
    def _init():
        r = lax.broadcasted_iota(jnp.int32, (tm, LANES), 0) % period
        ang = r.astype(F32) * inv
        cos_scr[...] = jnp.cos(ang)
        sin_scr[...] = jnp.sin(ang)

    base = jnp.full((1, LANES), pos0 + i * pos_stride, jnp.int32).astype(F32) * inv
    cb, sb = jnp.cos(base), jnp.sin(base)
    cos = cb * cos_scr[...] - sb * sin_scr[...]
    sin = sb * cos_scr[...] + cb * sin_scr[...]
    lane = lax.broadcasted_iota(jnp.int32, (tm, LANES), 1)
    first_half = lane < LANES // 2
    sin = jnp.where(first_half, -sin, sin)

    xn = _rms(x_ref[...], gmix_ref[...]).astype(BF16)
    for src_ref, dst_ref in zip(cast_in, cast_out):
        dst_ref[...] = src_ref[...].astype(BF16)

    def proj(group):
        return jnp.dot(xn, w_ref[:, group * RET_W:(group + 1) * RET_W], preferred_element_type=F32)

    def rope_store(z, out_ref, scale):
        for h in range(RET_HEADS):
            sl = slice(h * RET_DK, (h + 1) * RET_DK)
            zh = z[:, sl]
            r = zh * cos + pltpu.roll(zh, RET_DK // 2, 1) * sin
            if scale is not None:
                r = r * scale
            out_ref[:, sl] = r.astype(out_ref.dtype)

    def head_rms(z, g_ref):
        outs = []
        for j in range(ATT_W // LANES):
            sl = slice(j * LANES, (j + 1) * LANES)
            zj = z[:, sl]
            zz = zj * zj
            s0 = jnp.sum(jnp.where(first_half, zz, 0.0), axis=-1, keepdims=True)
            s1 = jnp.sum(jnp.where(first_half, 0.0, zz), axis=-1, keepdims=True)
            ms = jnp.where(first_half, s0, s1) * (1.0 / ATTN_DH)
            outs.append(zj * lax.rsqrt(ms + EPS) * g_ref[:, sl])
        return jnp.concatenate(outs, axis=1)

    qs = head_rms(proj(4), gq_ref) * (ATTN_DH ** -0.5 * LOG2E)
    if prompt:
        qs = qs.T
        pair_first = lax.broadcasted_iota(jnp.int32, qs.shape, 0) % LANES < ATTN_DH
    else:
        pair_first = jnp.concatenate([first_half] * (ATT_W // LANES), axis=1)
    qa_ref[...] = jnp.where(pair_first, qs, 0.0).astype(BF16)
    qb_ref[...] = jnp.where(pair_first, 0.0, qs).astype(BF16)
    ka = head_rms(proj(5), gk_ref)
    ka_ref[...] = ka.astype(BF16)
    va = proj(6)
    va_t = va.T
    vt_ref[...] = va_t.astype(BF16)
    rope_store(proj(0), qr_ref, None)
    rope_store(proj(1), kr_ref, RET_DK ** -0.5)
    sg_ref[...] = jax.nn.silu(proj(3)).astype(BF16)
    vr_ref[...] = proj(2).astype(BF16)

    if prompt:
        @pl.when(i == pl.num_programs(0) - 1)
        def _tail():
            ka32_ref[...] = ka[tm - BAND_PAST:].T
            va32_ref[...] = va_t[:, tm - BAND_PAST:]
    else:
        ka32_ref[...] = ka
        va32_ref[...] = va


def _cast_block_rows(rows, steps):
    block = BF16_ROWS
    while rows % block or block * steps < rows:
        block += BF16_ROWS
    return block


def _proj_call(x, g_mix, w_in, inv, gq, gk, *, pos0, pos_stride, period, prompt, cast_weights=()):
    n = x.shape[0]
    tm = TM_PROJ
    cast_specs, cast_shapes = [], []
    for w in cast_weights:
        rows, cols = w.shape
        block = _cast_block_rows(rows, n // tm)
        last = rows // block - 1
        cast_specs.append(pl.BlockSpec((block, cols), lambda i, last=last: (jnp.minimum(i, last), 0)))
        cast_shapes.append(jax.ShapeDtypeStruct(w.shape, BF16))
    row = lambda w: pl.BlockSpec((tm, w), lambda i: (i, 0))
    col_spec = pl.BlockSpec((ATT_W, tm), lambda i: (0, i))
    bf = jax.ShapeDtypeStruct((n, RET_W), BF16)
    bf_t = jax.ShapeDtypeStruct((ATT_W, n), BF16)
    if prompt:
        assert tm >= BAND_PAST
        q_shape, q_spec = bf_t, col_spec
        kv32 = jax.ShapeDtypeStruct((ATT_W, BAND_PAST), F32)
        kv32_spec = pl.BlockSpec((ATT_W, BAND_PAST), lambda i: (0, 0))
    else:
        q_shape, q_spec = bf, row(ATT_W)
        kv32 = jax.ShapeDtypeStruct((n, ATT_W), F32)
        kv32_spec = row(ATT_W)
    body = functools.partial(_proj_kernel, n_cast=len(cast_weights), pos0=pos0,
                             pos_stride=pos_stride, period=period, prompt=prompt)
    return pl.pallas_call(
        body,
        grid=(n // tm,),
        in_specs=[row(D_MODEL), _const_spec((1, D_MODEL)), _const_spec((D_MODEL, IN_COLS)),
                  _const_spec((1, LANES)), _const_spec((1, ATT_W)), _const_spec((1, ATT_W))]
                 + cast_specs,
        out_specs=[row(RET_W)] * 4 + [q_spec] * 2 + [row(ATT_W), col_spec] + [kv32_spec] * 2
                  + cast_specs,
        out_shape=[bf] * 4 + [q_shape] * 2 + [bf, bf_t] + [kv32] * 2 + cast_shapes,
        scratch_shapes=[pltpu.VMEM((tm, LANES), F32), pltpu.VMEM((tm, LANES), F32)],
        compiler_params=pltpu.CompilerParams(dimension_semantics=("arbitrary",),
                                             vmem_limit_bytes=VMEM_LIMIT),
        name="proj",
    )(x, g_mix, w_in, inv, gq, gk, *cast_weights)


def _init_decay(dmat_scr, xi_scr, zeta_scr, t):
    n = lax.broadcasted_iota(jnp.int32, (t, t), 0)
    m = lax.broadcasted_iota(jnp.int32, (t, t), 1)
    diff = (n - m).astype(F32)
    pos = lax.broadcasted_iota(jnp.int32, (t, RET_DK), 0).astype(F32)
    for h in range(RET_HEADS):
        lg = LOG_DECAY[h]
        dmat_scr[h] = jnp.where(diff >= 0, jnp.exp(jnp.maximum(diff, 0.0) * lg), 0.0)
        sl = slice(h * RET_DK, (h + 1) * RET_DK)
        xi_scr[:, sl] = jnp.exp((pos + 1.0) * lg)
        zeta_scr[:, sl] = jnp.exp((t - 1.0 - pos) * lg)


def _toeplitz_rows(vb_ref, first_shift):
    r = lax.broadcasted_iota(jnp.int32, (ATTN_HEADS * SUBLANES, BIAS_LANES), 0)
    base = jnp.concatenate([jnp.broadcast_to(vb_ref[h:h + 1, :], (SUBLANES, BIAS_LANES))
                            for h in range(ATTN_HEADS)], axis=0) * LOG2E
    base = pltpu.roll(base, first_shift, 1)
    for shift in (1, 2, 4):
        base = jnp.where((r & shift) != 0, pltpu.roll(base, shift, 1), base)
    return [base[h * SUBLANES:(h + 1) * SUBLANES] for h in range(ATTN_HEADS)]


def _init_bias_keys(vb_ref, bias_scr):
    _, n_keys, t = bias_scr.shape
    lane = lax.broadcasted_iota(jnp.int32, (SUBLANES, t), 1)
    for h, base in enumerate(_toeplitz_rows(vb_ref, BIAS_LANES - n_keys)):
        for g in range(n_keys // SUBLANES):
            rows = (pltpu.roll(base, SUBLANES * g, 1) if g else base)[:, :t]
            kc = SUBLANES * g // CHUNK
            lo = max(kc - BAND_PAST_CHUNKS, 0) * CHUNK
            hi = (min(kc, t // CHUNK - 1) + 1) * CHUNK
            rows = jnp.where(jnp.logical_and(lane >= lo, lane < hi), rows, MASKED)
            bias_scr[h, g * SUBLANES:(g + 1) * SUBLANES, :] = rows


def _init_bias_queries(vb_ref, bias_scr, t, w):
    first = lax.broadcasted_iota(jnp.int32, (SUBLANES, LANES), 1) < t
    for h, base in enumerate(_toeplitz_rows(vb_ref, BIAS_LANES - REL_CLIP)):
        for g in range(t // SUBLANES):
            rows = pltpu.roll(base, SUBLANES * g, 1) if g else base
            shifted = pltpu.roll(base, SUBLANES * g + t, 1)
            r0 = (h % 2) * t + g * SUBLANES
            bias_scr[h // 2, r0:r0 + SUBLANES, :w] = rows[:, :w]
            bias_scr[h // 2, r0:r0 + SUBLANES, w:w + LANES] = jnp.where(
                first, rows[:, w:w + LANES], MASKED)
            bias_scr[h // 2, r0:r0 + SUBLANES, w + LANES:] = jnp.where(
                first, MASKED, shifted[:, w:w + LANES])


def _retention(q, k, v, r, dmat, xi, zeta, decay_t):
    s = lax.dot_general(q, k, _NT, preferred_element_type=F32) * dmat
    inner = jnp.dot(s.astype(BF16), v, preferred_element_type=F32)
    cross = jnp.dot(q, r.astype(BF16), preferred_element_type=F32) * xi
    kz = (k.astype(F32) * zeta).astype(BF16)
    r_new = r * decay_t + lax.dot_general(kz, v, _TN, preferred_element_type=F32)
    return inner + cross, r_new


def _column_max(s):
    rows = s.shape[0]
    while rows % (2 * SUBLANES) == 0:
        rows //= 2
        s = jnp.maximum(s[:rows], s[rows:])
    return jnp.max(s, axis=0, keepdims=True)


def _mix_prompt_kernel(qr_ref, kr_ref, vr_ref, sg_ref, qa_ref, qb_ref, k0_ref, k1_ref, k2_ref,
                       v0_ref, v1_ref, v2_ref, gro_ref, vb_ref,
                       mixed_ref, rfin_ref,
                       r_scr, bias_scr, dmat_scr, xi_scr, zeta_scr):
    i = pl.program_id(0)
    t = TQ_PROMPT
    n_keys = bias_scr.shape[1]

    @pl.when(i == 0)
    def _init():
        r_scr[...] = jnp.zeros_like(r_scr)
        _init_decay(dmat_scr, xi_scr, zeta_scr, t)
        _init_bias_keys(vb_ref, bias_scr)

    def retention(h):
        sl = slice(h * RET_DK, (h + 1) * RET_DK)
        o, r_new = _retention(qr_ref[:, sl], kr_ref[:, sl], vr_ref[:, sl], r_scr[h],
                              dmat_scr[h], xi_scr[:, sl], zeta_scr[:, sl],
                              math.exp(t * LOG_DECAY[h]))
        r_scr[h] = r_new
        mixed_ref[:, sl] = (sg_ref[:, sl].astype(F32) * _rms(o, gro_ref[:, sl])).astype(BF16)

    ones = jnp.ones((DENOM_ROWS, n_keys), BF16)

    def scores(h, start_mask):
        sl = slice(h // 2 * LANES, (h // 2 + 1) * LANES)
        k = jnp.concatenate([k0_ref[:, sl], k1_ref[:, sl], k2_ref[:, sl]], axis=0)
        q_ref = qb_ref if h % 2 else qa_ref
        s = jnp.dot(k, q_ref[sl, :], preferred_element_type=F32) + bias_scr[h]
        return s if start_mask is None else s + start_mask

    def attention(start_mask):
        pending = [scores(h, start_mask) for h in range(SCORE_LOOKAHEAD)]
        halves = []
        for h in range(ATTN_HEADS):
            if h < RET_HEADS:
                retention(h)
            if h + SCORE_LOOKAHEAD < ATTN_HEADS:
                pending.append(scores(h + SCORE_LOOKAHEAD, start_mask))
            s = pending.pop(0)
            e = jnp.exp2(s - _column_max(s)).astype(BF16)
            rows = slice(h * ATTN_DH, (h + 1) * ATTN_DH)
            v_t = jnp.concatenate([v0_ref[rows, :], v1_ref[rows, :], v2_ref[rows, :]], axis=1)
            out = jnp.dot(jnp.concatenate([v_t, ones], axis=0), e, preferred_element_type=F32)
            halves.append(out[:ATTN_DH] / out[ATTN_DH:ATTN_DH + 1])
            if h % 2:
                j = h // 2
                o_t = jnp.concatenate(halves, axis=0)
                halves = []
                mixed_ref[:, RET_W + j * LANES:RET_W + (j + 1) * LANES] = o_t.T.astype(BF16)

    @pl.when(i < 2)
    def _start():
        blk = lax.broadcasted_iota(jnp.int32, (n_keys, t), 0) // t
        attention(jnp.where(blk + i >= 2, 0.0, MASKED))

    @pl.when(i >= 2)
    def _steady():
        attention(None)

    @pl.when(i == pl.num_programs(0) - 1)
    def _fin():
        rfin_ref[...] = r_scr[...]


def _mix_prompt_call(qr, kr, vr, sg, qa_t, qb_t, ka, va_t, gro, vb):
    n = qr.shape[0]
    t = TQ_PROMPT
    row = pl.BlockSpec((t, RET_W), lambda i: (i, 0))
    col = pl.BlockSpec((ATT_W, t), lambda i: (0, i))
    back = lambda d: pl.BlockSpec((t, ATT_W), lambda i: (jnp.maximum(i - d, 0), 0))
    back_t = lambda d: pl.BlockSpec((ATT_W, t), lambda i: (0, jnp.maximum(i - d, 0)))
    state = (RET_HEADS, RET_DK, RET_DV)
    return pl.pallas_call(
        _mix_prompt_kernel,
        grid=(n // t,),
        in_specs=[row] * 4 + [col] * 2 + [back(2), back(1), back(0)]
                 + [back_t(2), back_t(1), back_t(0)]
                 + [_const_spec((1, RET_W)), _const_spec((ATTN_HEADS, BIAS_LANES))],
        out_specs=[pl.BlockSpec((t, D_MODEL), lambda i: (i, 0)),
                   pl.BlockSpec(state, lambda i: (0, 0, 0))],
        out_shape=[jax.ShapeDtypeStruct((n, D_MODEL), BF16), jax.ShapeDtypeStruct(state, F32)],
        scratch_shapes=[pltpu.VMEM(state, F32),
                        pltpu.VMEM((ATTN_HEADS, 3 * t, t), F32),
                        pltpu.VMEM((RET_HEADS, t, t), F32),
                        pltpu.VMEM((t, RET_W), F32),
                        pltpu.VMEM((t, RET_W), F32)],
        compiler_params=pltpu.CompilerParams(dimension_semantics=("arbitrary",),
                                             vmem_limit_bytes=VMEM_LIMIT),
        name="mix_prompt",
    )(qr, kr, vr, sg, qa_t, qb_t, ka, ka, ka, va_t, va_t, va_t, gro, vb)


def _mix_sample_kernel(qr_ref, kr_ref, vr_ref, sg_ref, qa_ref, qb_ref, ka_ref, vt_ref, ck_ref, cv_ref,
                       r0_ref, gro_ref, vb_ref,
                       mixed_ref, rnew_ref,
                       bias_scr, dmat_scr, xi_scr, zeta_scr, *, t):
    streams, _, w = ck_ref.shape
    assert streams * t == LANES

    @pl.when(pl.program_id(0) == 0)
    def _init():
        _init_decay(dmat_scr, xi_scr, zeta_scr, t)
        _init_bias_queries(vb_ref, bias_scr, t, w)

    ones = jnp.ones((DENOM_ROWS, w), BF16)
    own_head = ((lax.broadcasted_iota(jnp.int32, (LANES, 2 * t), 0) < ATTN_DH)
                == (lax.broadcasted_iota(jnp.int32, (LANES, 2 * t), 1) < t))

    def retention(b, h):
        rows = slice(b * t, (b + 1) * t)
        sl = slice(h * RET_DK, (h + 1) * RET_DK)
        o, r_new = _retention(qr_ref[rows, sl], kr_ref[rows, sl], vr_ref[rows, sl], r0_ref[b, h],
                              dmat_scr[h], xi_scr[:, sl], zeta_scr[:, sl],
                              math.exp(t * LOG_DECAY[h]))
        rnew_ref[b, h] = r_new
        mixed_ref[rows, sl] = (sg_ref[rows, sl].astype(F32) * _rms(o, gro_ref[:, sl])).astype(BF16)

    def scores(b, j):
        rows = slice(b * t, (b + 1) * t)
        sl = slice(j * LANES, (j + 1) * LANES)
        q2 = jnp.concatenate([qa_ref[rows, sl], qb_ref[rows, sl]], axis=0)
        s_c = (jnp.dot(q2, ck_ref[b, sl, :].astype(BF16), preferred_element_type=F32)
               + bias_scr[j, :, :w])
        s_n = (lax.dot_general(q2, ka_ref[:, sl], _NT, preferred_element_type=F32)
               + bias_scr[j, :, w + b * LANES:w + (b + 1) * LANES])
        return s_c, s_n

    def finish(b, j, s_c, s_n):
        rows = slice(b * t, (b + 1) * t)
        sl = slice(j * LANES, (j + 1) * LANES)
        m = jnp.maximum(jnp.max(s_c, axis=-1, keepdims=True),
                        jnp.max(s_n, axis=-1, keepdims=True))
        e_c = jnp.exp2(s_c - m).astype(BF16)
        e_n = jnp.exp2(s_n - m).astype(BF16)
        lhs_c = jnp.concatenate([cv_ref[b, sl, :].astype(BF16), ones], axis=0)
        lhs_n = jnp.concatenate([vt_ref[sl, :], ones[:, :LANES]], axis=0)
        out = (lax.dot_general(lhs_c, e_c, _NT, preferred_element_type=F32)
               + lax.dot_general(lhs_n, e_n, _NT, preferred_element_type=F32))
        x = out[:LANES] / out[LANES:LANES + 1]
        x_t = jnp.where(own_head, x, 0.0).T
        mixed_ref[rows, RET_W + j * LANES:RET_W + (j + 1) * LANES] = (
            x_t[:t] + x_t[t:]).astype(BF16)

    items = [(b, j) for b in range(streams) for j in range(ATTN_HEADS // 2)]
    ret_items = [(b, h) for b in range(streams) for h in range(RET_HEADS)]
    pending = [scores(*it) for it in items[:SAMPLE_LOOKAHEAD]]
    for n, (b, j) in enumerate(items):
        if n < len(ret_items):
            retention(*ret_items[n])
        if n + SAMPLE_LOOKAHEAD < len(items):
            pending.append(scores(*items[n + SAMPLE_LOOKAHEAD]))
        finish(b, j, *pending.pop(0))


def _mix_sample_call(qr, kr, vr, sg, qa, qb, ka, va_t, ck, cv, r0, gro, vb, t):
    n = qr.shape[0]
    nb, _, w = ck.shape
    g = LANES // t
    assert nb % g == 0
    row = pl.BlockSpec((g * t, RET_W), lambda b: (b, 0))
    col = pl.BlockSpec((ATT_W, g * t), lambda b: (0, b))
    cache = pl.BlockSpec((g, ATT_W, w), lambda b: (b, 0, 0))
    state = pl.BlockSpec((g, RET_HEADS, RET_DK, RET_DV), lambda b: (b, 0, 0, 0))
    return pl.pallas_call(
        functools.partial(_mix_sample_kernel, t=t),
        grid=(nb // g,),
        in_specs=[row] * 7 + [col, cache, cache, state,
                              _const_spec((1, RET_W)), _const_spec((ATTN_HEADS, BIAS_LANES))],
        out_specs=[pl.BlockSpec((g * t, D_MODEL), lambda b: (b, 0)), state],
        out_shape=[jax.ShapeDtypeStruct((n, D_MODEL), BF16),
                   jax.ShapeDtypeStruct((nb, RET_HEADS, RET_DK, RET_DV), F32)],
        scratch_shapes=[pltpu.VMEM((ATTN_HEADS // 2, 2 * t, w + g * LANES), F32),
                        pltpu.VMEM((RET_HEADS, t, t), F32),
                        pltpu.VMEM((t, RET_W), F32),
                        pltpu.VMEM((t, RET_W), F32)],
        compiler_params=pltpu.CompilerParams(dimension_semantics=("arbitrary",),
                                             vmem_limit_bytes=VMEM_LIMIT),
        name="mix_sample",
    )(qr, kr, vr, sg, qa, qb, ka, va_t, ck, cv, r0, gro, vb)


FF_SPLITS = (0, 1024, 2048, D_FF)


def _tail_kernel(x_ref, mixed_ref, p_ref, wout_ref, gffn_ref, wg_ref, wu_ref, wd_ref,
                 gple_ref, wpg_ref, wpp_ref, out_ref):
    tm = x_ref.shape[0]
    halves = (slice(0, tm // 2), slice(tm // 2, tm))
    ranges = list(zip(FF_SPLITS[:-1], FF_SPLITS[1:]))
    dot = functools.partial(jnp.dot, preferred_element_type=F32)

    h = [x_ref[r, :] + dot(mixed_ref[r, :], wout_ref[...]) for r in halves]
    u = [_rms(hr, gffn_ref[...]).astype(BF16) for hr in h]

    c0, c1 = ranges[0]
    act = jnp.concatenate([(jax.nn.silu(dot(ur, wg_ref[:, c0:c1])) * dot(ur, wu_ref[:, c0:c1])
                            ).astype(BF16) for ur in u], axis=0)
    ffn = dot(act, wd_ref[c0:c1, :])
    u = jnp.concatenate(u, axis=0)
    for c0, c1 in ranges[1:]:
        act = (jax.nn.silu(dot(u, wg_ref[:, c0:c1])) * dot(u, wu_ref[:, c0:c1])).astype(BF16)
        if (c0, c1) != ranges[-1]:
            ffn = ffn + dot(act, wd_ref[c0:c1, :])

    c0, c1 = ranges[-1]
    h = [hr + ffn[r, :] + dot(act[r, :], wd_ref[c0:c1, :]) for hr, r in zip(h, halves)]
    ple = dot(p_ref[...].astype(BF16), wpp_ref[...])
    for hr, r in zip(h, halves):
        gate = jax.nn.sigmoid(dot(_rms(hr, gple_ref[...]).astype(BF16), wpg_ref[...]))
        out_ref[r, :] = hr + ple[r, :] * gate


def _tail_call(x, mixed, p, w_out, g_ffn, w_gate, w_up, w_down, g_ple, w_pg, w_pp):
    n = x.shape[0]
    tm = TM_DENSE
    row = lambda w: pl.BlockSpec((tm, w), lambda i: (i, 0))
    return pl.pallas_call(
        _tail_kernel,
        grid=(n // tm,),
        in_specs=[row(D_MODEL), row(D_MODEL), row(PLE_DIM),
                  _const_spec((D_MODEL, D_MODEL)), _const_spec((1, D_MODEL)),
                  _const_spec((D_MODEL, D_FF)), _const_spec((D_MODEL, D_FF)),
                  _const_spec((D_FF, D_MODEL)), _const_spec((1, D_MODEL)),
                  _const_spec((D_MODEL, D_MODEL)), _const_spec((PLE_DIM, D_MODEL))],
        out_specs=row(D_MODEL),
        out_shape=jax.ShapeDtypeStruct((n, D_MODEL), F32),
        compiler_params=pltpu.CompilerParams(dimension_semantics=("arbitrary",),
                                             vmem_limit_bytes=VMEM_LIMIT),
        name="tail",
    )(x, mixed, p, w_out, g_ffn, w_gate, w_up, w_down, g_ple, w_pg, w_pp)


def _bias_rows(rel_bias):
    edge = jnp.broadcast_to(rel_bias[:, 2 * REL_CLIP:], (ATTN_HEADS, BIAS_LANES - 2 * REL_CLIP))
    by_query = jnp.concatenate([edge, rel_bias[:, 2 * REL_CLIP:0:-1]], axis=1)
    by_key = jnp.concatenate([rel_bias[:, :2 * REL_CLIP], edge], axis=1)
    return by_query, by_key


def kernel(x_prompt, x_sample, cache_attn_k, cache_attn_v, state_ret, p_prompt, p_sample,
           g_mix, w_in, g_ret_out, g_q_attn, g_k_attn, rel_bias, w_out, g_ffn,
           w_ffn_gate, w_ffn_up, w_ffn_down, g_ple, w_ple_gate, w_ple_proj):
    depth = w_in.shape[0]
    assert depth == 1, "single-layer trunk"
    batch, seq, _ = x_prompt.shape
    dec_batch, dec_seq, _ = x_sample.shape
    cache_w = cache_attn_k.shape[2]
    assert batch == 1 and seq % TM_PROJ == 0 and dec_seq == CHUNK and cache_w == BAND_PAST
    assert BAND_PAST <= seq, "the returned prompt keys are the tail of the last projection tile"

    row = lambda g: g.reshape(1, -1)
    gq = row(jnp.tile(g_q_attn[0], ATTN_HEADS))
    gk = row(jnp.tile(g_k_attn[0], ATTN_HEADS))
    gro = row(g_ret_out[0])
    vb_query, vb_key = _bias_rows(rel_bias[0])
    half = RET_DK // 2
    inv = ROPE_THETA ** (-jnp.arange(half, dtype=F32) / half)
    inv = row(jnp.concatenate([inv, inv]))
    w_in_b = w_in[0].astype(BF16)

    xp = x_prompt.reshape(seq, D_MODEL)
    (qr, kr, vr, sg, qa_t, qb_t, ka, va_t, k_tail, v_tail,
     w_out_b, w_gate_b, w_up_b, w_down_b, w_pg_b, w_pp_b) = _proj_call(
        xp, row(g_mix[0]), w_in_b, inv, gq, gk,
        pos0=0, pos_stride=TM_PROJ, period=TM_PROJ, prompt=True,
        cast_weights=(w_out[0], w_ffn_gate[0], w_ffn_up[0], w_ffn_down[0],
                      w_ple_gate[0], w_ple_proj[0]))
    tail_w_args = (w_out_b, row(g_ffn[0]), w_gate_b, w_up_b, w_down_b, row(g_ple[0]),
                   w_pg_b, w_pp_b)
    mixed_p, r_fin = _mix_prompt_call(qr, kr, vr, sg, qa_t, qb_t, ka, va_t, gro, vb_key)
    y_prompt = _tail_call(xp, mixed_p, p_prompt[0].reshape(seq, PLE_DIM), *tail_w_args)

    n_s = dec_batch * dec_seq
    xs = x_sample.reshape(n_s, D_MODEL)
    qr, kr, vr, sg, qa, qb, ka, va_t, k_new, v_new = _proj_call(
        xs, row(g_mix[0]), w_in_b, inv, gq, gk,
        pos0=PAST_LEN, pos_stride=0, period=dec_seq, prompt=False)
    feat_major = lambda c: c.transpose(0, 2, 3, 1).reshape(dec_batch, ATT_W, cache_w)
    mixed_s, r_new = _mix_sample_call(
        qr, kr, vr, sg, qa, qb, ka, va_t,
        feat_major(cache_attn_k[0]), feat_major(cache_attn_v[0]),
        state_ret[0], gro, vb_query, dec_seq)
    y_sample = _tail_call(xs, mixed_s, p_sample[0].reshape(n_s, PLE_DIM), *tail_w_args)

    kv_p = lambda a: a.reshape(ATTN_HEADS, ATTN_DH, BAND_PAST).transpose(2, 0, 1).reshape(
        1, batch, BAND_PAST, ATTN_HEADS, ATTN_DH)
    kv_s = lambda a: a.reshape(1, dec_batch, dec_seq, ATTN_HEADS, ATTN_DH)
    return (y_prompt.reshape(batch, seq, D_MODEL),
            y_sample.reshape(dec_batch, dec_seq, D_MODEL),
            r_fin.reshape(1, batch, RET_HEADS, RET_DK, RET_DV),
            kv_p(k_tail), kv_p(v_tail),
            r_new.reshape(1, dec_batch, RET_HEADS, RET_DK, RET_DV),
            kv_s(k_new), kv_s(v_new))
```

```python
import functools
import math

import jax
import jax.numpy as jnp
from jax import lax
from jax.experimental import pallas as pl
from jax.experimental.pallas import tpu as pltpu

F32 = jnp.float32
BF16 = jnp.bfloat16

D_MODEL = 1024
CHUNK = 64
PAST_LEN = 2048
RET_HEADS = 4
RET_DK = 128
RET_DV = 128
ATTN_HEADS = 8
ATTN_DH = 64
BAND_PAST_CHUNKS = 8
BAND_PAST = BAND_PAST_CHUNKS * CHUNK
REL_CLIP = 256
D_FF = 2816
PLE_DIM = 256
ROPE_THETA = 10000.0
EPS = 1e-6
RET_W = RET_HEADS * RET_DK
ATT_W = ATTN_HEADS * ATTN_DH
IN_COLS = 4 * RET_W + 3 * ATT_W
LOG_DECAY = tuple(math.log1p(-(2.0 ** (-5 - h))) for h in range(RET_HEADS))
LOG2E = math.log2(math.e)
MASKED = -1e30

LANES = 128
SUBLANES = 8
BF16_ROWS = 16
BIAS_LANES = 1024
TM_PROJ = 1024
TM_DENSE = 1024
TM_TAIL_SUB = 512
TQ_PROMPT = 256
DENOM_ROWS = BF16_ROWS
SAMPLE_LOOKAHEAD = 3
SCORE_LOOKAHEAD = 3
VMEM_LIMIT = 56 * 1024 * 1024

_NT = (((1,), (1,)), ((), ()))
_TN = (((0,), (0,)), ((), ()))


def _rms(x, g):
    return x * lax.rsqrt(jnp.mean(x * x, axis=-1, keepdims=True) + EPS) * g


def _const_spec(shape):
    zeros = (0,) * len(shape)
    return pl.BlockSpec(shape, lambda i: zeros, pipeline_mode=pl.Buffered(1))


def _proj_kernel(*refs, n_cast, pos0, pos_stride, period, prompt):
    x_ref, gmix_ref, w_ref, inv_ref, gq_ref, gk_ref = refs[:6]
    cast_in = refs[6:6 + n_cast]
    (qr_ref, kr_ref, vr_ref, sg_ref, qa_ref, qb_ref, ka_ref, vt_ref,
     ka32_ref, va32_ref) = refs[6 + n_cast:16 + n_cast]
    cast_out = refs[16 + n_cast:16 + 2 * n_cast]
    cos_scr, sin_scr = refs[16 + 2 * n_cast:]

    i = pl.program_id(0)
    tm = x_ref.shape[0]
    inv = inv_ref[...]

    @pl.when(i == 0)
    def _init():
        r = lax.broadcasted_iota(jnp.int32, (tm, LANES), 0) % period
        ang = r.astype(F32) * inv
        cos_scr[...] = jnp.cos(ang)
        sin_scr[...] = jnp.sin(ang)

    base = jnp.full((1, LANES), pos0 + i * pos_stride, jnp.int32).astype(F32) * inv
    cb, sb = jnp.cos(base), jnp.sin(base)
    cos = cb * cos_scr[...] - sb * sin_scr[...]
    sin = sb * cos_scr[...] + cb * sin_scr[...]
    lane = lax.broadcasted_iota(jnp.int32, (tm, LANES), 1)
    first_half = lane < LANES // 2
    sin = jnp.where(first_half, -sin, sin)

    xn = _rms(x_ref[...], gmix_ref[...]).astype(BF16)
    for src_ref, dst_ref in zip(cast_in, cast_out):
        dst_ref[...] = src_ref[...].astype(BF16)

    def proj(group):
        return jnp.dot(xn, w_ref[:, group * RET_W:(group + 1) * RET_W], preferred_element_type=F32)

    def rope_store(z, out_ref, scale):
        for h in range(RET_HEADS):
            sl = slice(h * RET_DK, (h + 1) * RET_DK)
            zh = z[:, sl]
            r = zh * cos + pltpu.roll(zh, RET_DK // 2, 1) * sin
            if scale is not None:
                r = r * scale
            out_ref[:, sl] = r.astype(out_ref.dtype)

    def head_rms(z, g_ref):
        outs = []
        for j in range(ATT_W // LANES):
            sl = slice(j * LANES, (j + 1) * LANES)
            zj = z[:, sl]
            zz = zj * zj
            s0 = jnp.sum(jnp.where(first_half, zz, 0.0), axis=-1, keepdims=True)
            s1 = jnp.sum(jnp.where(first_half, 0.0, zz), axis=-1, keepdims=True)
            ms = jnp.where(first_half, s0, s1) * (1.0 / ATTN_DH)
            outs.append(zj * lax.rsqrt(ms + EPS) * g_ref[:, sl])
        return jnp.concatenate(outs, axis=1)

    qs = head_rms(proj(4), gq_ref) * (ATTN_DH ** -0.5 * LOG2E)
    if prompt:
        qs = qs.T
        pair_first = lax.broadcasted_iota(jnp.int32, qs.shape, 0) % LANES < ATTN_DH
    else:
        pair_first = jnp.concatenate([first_half] * (ATT_W // LANES), axis=1)
    qa_ref[...] = jnp.where(pair_first, qs, 0.0).astype(BF16)
    qb_ref[...] = jnp.where(pair_first, 0.0, qs).astype(BF16)
    ka = head_rms(proj(5), gk_ref)
    ka_ref[...] = ka.astype(BF16)
    va = proj(6)
    va_t = va.T
    vt_ref[...] = va_t.astype(BF16)
    rope_store(proj(0), qr_ref, None)
    rope_store(proj(1), kr_ref, RET_DK ** -0.5)
    sg_ref[...] = jax.nn.silu(proj(3)).astype(BF16)
    vr_ref[...] = proj(2).astype(BF16)

    if prompt:
        @pl.when(i == pl.num_programs(0) - 1)
        def _tail():
            ka32_ref[...] = ka[tm - BAND_PAST:].T
            va32_ref[...] = va_t[:, tm - BAND_PAST:]
    else:
        ka32_ref[...] = ka
        va32_ref[...] = va


def _cast_block_rows(rows, steps):
    block = BF16_ROWS
    while rows % block or block * steps < rows:
        block += BF16_ROWS
    return block


def _proj_call(x, g_mix, w_in, inv, gq, gk, *, pos0, pos_stride, period, prompt, cast_weights=()):
    n = x.shape[0]
    tm = TM_PROJ
    cast_specs, cast_shapes = [], []
    for w in cast_weights:
        rows, cols = w.shape
        block = _cast_block_rows(rows, n // tm)
        last = rows // block - 1
        cast_specs.append(pl.BlockSpec((block, cols), lambda i, last=last: (jnp.minimum(i, last), 0)))
        cast_shapes.append(jax.ShapeDtypeStruct(w.shape, BF16))
    row = lambda w: pl.BlockSpec((tm, w), lambda i: (i, 0))
    col_spec = pl.BlockSpec((ATT_W, tm), lambda i: (0, i))
    bf = jax.ShapeDtypeStruct((n, RET_W), BF16)
    bf_t = jax.ShapeDtypeStruct((ATT_W, n), BF16)
    if prompt:
        assert tm >= BAND_PAST
        q_shape, q_spec = bf_t, col_spec
        kv32 = jax.ShapeDtypeStruct((ATT_W, BAND_PAST), F32)
        kv32_spec = pl.BlockSpec((ATT_W, BAND_PAST), lambda i: (0, 0))
    else:
        q_shape, q_spec = bf, row(ATT_W)
        kv32 = jax.ShapeDtypeStruct((n, ATT_W), F32)
        kv32_spec = row(ATT_W)
    body = functools.partial(_proj_kernel, n_cast=len(cast_weights), pos0=pos0,
                             pos_stride=pos_stride, period=period, prompt=prompt)
    return pl.pallas_call(
        body,
        grid=(n // tm,),
        in_specs=[row(D_MODEL), _const_spec((1, D_MODEL)), _const_spec((D_MODEL, IN_COLS)),
                  _const_spec((1, LANES)), _const_spec((1, ATT_W)), _const_spec((1, ATT_W))]
                 + cast_specs,
        out_specs=[row(RET_W)] * 4 + [q_spec] * 2 + [row(ATT_W), col_spec] + [kv32_spec] * 2
                  + cast_specs,
        out_shape=[bf] * 4 + [q_shape] * 2 + [bf, bf_t] + [kv32] * 2 + cast_shapes,
        scratch_shapes=[pltpu.VMEM((tm, LANES), F32), pltpu.VMEM((tm, LANES), F32)],
        compiler_params=pltpu.CompilerParams(dimension_semantics=("arbitrary",),
                                             vmem_limit_bytes=VMEM_LIMIT),
        name="proj",
    )(x, g_mix, w_in, inv, gq, gk, *cast_weights)


def _init_decay(dmat_scr, xi_scr, zeta_scr, t):
    n = lax.broadcasted_iota(jnp.int32, (t, t), 0)
    m = lax.broadcasted_iota(jnp.int32, (t, t), 1)
    diff = (n - m).astype(F32)
    pos = lax.broadcasted_iota(jnp.int32, (t, RET_DK), 0).astype(F32)
    for h in range(RET_HEADS):
        lg = LOG_DECAY[h]
        dmat_scr[h] = jnp.where(diff >= 0, jnp.exp(jnp.maximum(diff, 0.0) * lg), 0.0)
        sl = slice(h * RET_DK, (h + 1) * RET_DK)
        xi_scr[:, sl] = jnp.exp((pos + 1.0) * lg)
        zeta_scr[:, sl] = jnp.exp((t - 1.0 - pos) * lg)


def _toeplitz_rows(vb_ref, first_shift):
    r = lax.broadcasted_iota(jnp.int32, (ATTN_HEADS * SUBLANES, BIAS_LANES), 0)
    base = jnp.concatenate([jnp.broadcast_to(vb_ref[h:h + 1, :], (SUBLANES, BIAS_LANES))
                            for h in range(ATTN_HEADS)], axis=0) * LOG2E
    base = pltpu.roll(base, first_shift, 1)
    for shift in (1, 2, 4):
        base = jnp.where((r & shift) != 0, pltpu.roll(base, shift, 1), base)
    return [base[h * SUBLANES:(h + 1) * SUBLANES] for h in range(ATTN_HEADS)]


def _init_bias_keys(vb_ref, bias_scr):
    _, n_keys, t = bias_scr.shape
    lane = lax.broadcasted_iota(jnp.int32, (SUBLANES, t), 1)
    for h, base in enumerate(_toeplitz_rows(vb_ref, BIAS_LANES - n_keys)):
        for g in range(n_keys // SUBLANES):
            rows = (pltpu.roll(base, SUBLANES * g, 1) if g else base)[:, :t]
            kc = SUBLANES * g // CHUNK
            lo = max(kc - BAND_PAST_CHUNKS, 0) * CHUNK
            hi = (min(kc, t // CHUNK - 1) + 1) * CHUNK
            rows = jnp.where(jnp.logical_and(lane >= lo, lane < hi), rows, MASKED)
            bias_scr[h, g * SUBLANES:(g + 1) * SUBLANES, :] = rows


def _init_bias_queries(vb_ref, bias_scr, t, w):
    first = lax.broadcasted_iota(jnp.int32, (SUBLANES, LANES), 1) < t
    for h, base in enumerate(_toeplitz_rows(vb_ref, BIAS_LANES - REL_CLIP)):
        for g in range(t // SUBLANES):
            rows = pltpu.roll(base, SUBLANES * g, 1) if g else base
            shifted = pltpu.roll(base, SUBLANES * g + t, 1)
            r0 = (h % 2) * t + g * SUBLANES
            bias_scr[h // 2, r0:r0 + SUBLANES, :w] = rows[:, :w]
            bias_scr[h // 2, r0:r0 + SUBLANES, w:w + LANES] = jnp.where(
                first, rows[:, w:w + LANES], MASKED)
            bias_scr[h // 2, r0:r0 + SUBLANES, w + LANES:] = jnp.where(
                first, MASKED, shifted[:, w:w + LANES])


def _retention(q, k, v, r, dmat, xi, zeta, decay_t):
    s = lax.dot_general(q, k, _NT, preferred_element_type=F32) * dmat
    inner = jnp.dot(s.astype(BF16), v, preferred_element_type=F32)
    cross = jnp.dot(q, r.astype(BF16), preferred_element_type=F32) * xi
    kz = (k.astype(F32) * zeta).astype(BF16)
    r_new = r * decay_t + lax.dot_general(kz, v, _TN, preferred_element_type=F32)
    return inner + cross, r_new


def _column_max(s):
    rows = s.shape[0]
    while rows % (2 * SUBLANES) == 0:
        rows //= 2
        s = jnp.maximum(s[:rows], s[rows:])
    return jnp.max(s, axis=0, keepdims=True)


def _mix_prompt_kernel(qr_ref, kr_ref, vr_ref, sg_ref, qa_ref, qb_ref, k0_ref, k1_ref, k2_ref,
                       v0_ref, v1_ref, v2_ref, gro_ref, vb_ref,
                       mixed_ref, rfin_ref,
                       r_scr, bias_scr, dmat_scr, xi_scr, zeta_scr):
    i = pl.program_id(0)
    t = TQ_PROMPT
    n_keys = bias_scr.shape[1]

    @pl.when(i == 0)
    def _init():
        r_scr[...] = jnp.zeros_like(r_scr)
        _init_decay(dmat_scr, xi_scr, zeta_scr, t)
        _init_bias_keys(vb_ref, bias_scr)

    def retention(h):
        sl = slice(h * RET_DK, (h + 1) * RET_DK)
        o, r_new = _retention(qr_ref[:, sl], kr_ref[:, sl], vr_ref[:, sl], r_scr[h],
                              dmat_scr[h], xi_scr[:, sl], zeta_scr[:, sl],
                              math.exp(t * LOG_DECAY[h]))
        r_scr[h] = r_new
        mixed_ref[:, sl] = (sg_ref[:, sl].astype(F32) * _rms(o, gro_ref[:, sl])).astype(BF16)

    ones = jnp.ones((DENOM_ROWS, n_keys), BF16)

    def scores(h, start_mask):
        sl = slice(h // 2 * LANES, (h // 2 + 1) * LANES)
        k = jnp.concatenate([k0_ref[:, sl], k1_ref[:, sl], k2_ref[:, sl]], axis=0)
        q_ref = qb_ref if h % 2 else qa_ref
        s = jnp.dot(k, q_ref[sl, :], preferred_element_type=F32) + bias_scr[h]
        return s if start_mask is None else s + start_mask

    def attention(start_mask):
        pending = [scores(h, start_mask) for h in range(SCORE_LOOKAHEAD)]
        halves = []
        for h in range(ATTN_HEADS):
            if h < RET_HEADS:
                retention(h)
            if h + SCORE_LOOKAHEAD < ATTN_HEADS:
                pending.append(scores(h + SCORE_LOOKAHEAD, start_mask))
            s = pending.pop(0)
            e = jnp.exp2(s - _column_max(s)).astype(BF16)
            rows = slice(h * ATTN_DH, (h + 1) * ATTN_DH)
            v_t = jnp.concatenate([v0_ref[rows, :], v1_ref[rows, :], v2_ref[rows, :]], axis=1)
            out = jnp.dot(jnp.concatenate([v_t, ones], axis=0), e, preferred_element_type=F32)
            halves.append(out[:ATTN_DH] / out[ATTN_DH:ATTN_DH + 1])
            if h % 2:
                j = h // 2
                o_t = jnp.concatenate(halves, axis=0)
                halves = []
                mixed_ref[:, RET_W + j * LANES:RET_W + (j + 1) * LANES] = o_t.T.astype(BF16)

    @pl.when(i < 2)
    def _start():
        blk = lax.broadcasted_iota(jnp.int32, (n_keys, t), 0) // t
        attention(jnp.where(blk + i >= 2, 0.0, MASKED))

    @pl.when(i >= 2)
    def _steady():
        attention(None)

    @pl.when(i == pl.num_programs(0) - 1)
    def _fin():
        rfin_ref[...] = r_scr[...]


def _mix_prompt_call(qr, kr, vr, sg, qa_t, qb_t, ka, va_t, gro, vb):
    n = qr.shape[0]
    t = TQ_PROMPT
    row = pl.BlockSpec((t, RET_W), lambda i: (i, 0))
    col = pl.BlockSpec((ATT_W, t), lambda i: (0, i))
    back = lambda d: pl.BlockSpec((t, ATT_W), lambda i: (jnp.maximum(i - d, 0), 0))
    back_t = lambda d: pl.BlockSpec((ATT_W, t), lambda i: (0, jnp.maximum(i - d, 0)))
    state = (RET_HEADS, RET_DK, RET_DV)
    return pl.pallas_call(
        _mix_prompt_kernel,
        grid=(n // t,),
        in_specs=[row] * 4 + [col] * 2 + [back(2), back(1), back(0)]
                 + [back_t(2), back_t(1), back_t(0)]
                 + [_const_spec((1, RET_W)), _const_spec((ATTN_HEADS, BIAS_LANES))],
        out_specs=[pl.BlockSpec((t, D_MODEL), lambda i: (i, 0)),
                   pl.BlockSpec(state, lambda i: (0, 0, 0))],
        out_shape=[jax.ShapeDtypeStruct((n, D_MODEL), BF16), jax.ShapeDtypeStruct(state, F32)],
        scratch_shapes=[pltpu.VMEM(state, F32),
                        pltpu.VMEM((ATTN_HEADS, 3 * t, t), F32),
                        pltpu.VMEM((RET_HEADS, t, t), F32),
                        pltpu.VMEM((t, RET_W), F32),
                        pltpu.VMEM((t, RET_W), F32)],
        compiler_params=pltpu.CompilerParams(dimension_semantics=("arbitrary",),
                                             vmem_limit_bytes=VMEM_LIMIT),
        name="mix_prompt",
    )(qr, kr, vr, sg, qa_t, qb_t, ka, ka, ka, va_t, va_t, va_t, gro, vb)


def _mix_sample_kernel(qr_ref, kr_ref, vr_ref, sg_ref, qa_ref, qb_ref, ka_ref, vt_ref, ck_ref, cv_ref,
                       r0_ref, gro_ref, vb_ref,
                       mixed_ref, rnew_ref,
                       bias_scr, dmat_scr, xi_scr, zeta_scr, *, t):
    streams, _, w = ck_ref.shape
    assert streams * t == LANES

    @pl.when(pl.program_id(0) == 0)
    def _init():
        _init_decay(dmat_scr, xi_scr, zeta_scr, t)
        _init_bias_queries(vb_ref, bias_scr, t, w)

    ones = jnp.ones((DENOM_ROWS, w), BF16)
    own_head = ((lax.broadcasted_iota(jnp.int32, (LANES, 2 * t), 0) < ATTN_DH)
                == (lax.broadcasted_iota(jnp.int32, (LANES, 2 * t), 1) < t))

    def retention(b, h):
        rows = slice(b * t, (b + 1) * t)
        sl = slice(h * RET_DK, (h + 1) * RET_DK)
        o, r_new = _retention(qr_ref[rows, sl], kr_ref[rows, sl], vr_ref[rows, sl], r0_ref[b, h],
                              dmat_scr[h], xi_scr[:, sl], zeta_scr[:, sl],
                              math.exp(t * LOG_DECAY[h]))
        rnew_ref[b, h] = r_new
        mixed_ref[rows, sl] = (sg_ref[rows, sl].astype(F32) * _rms(o, gro_ref[:, sl])).astype(BF16)

    def scores(b, j):
        rows = slice(b * t, (b + 1) * t)
        sl = slice(j * LANES, (j + 1) * LANES)
        q2 = jnp.concatenate([qa_ref[rows, sl], qb_ref[rows, sl]], axis=0)
        s_c = (jnp.dot(q2, ck_ref[b, sl, :].astype(BF16), preferred_element_type=F32)
               + bias_scr[j, :, :w])
        s_n = (lax.dot_general(q2, ka_ref[:, sl], _NT, preferred_element_type=F32)
               + bias_scr[j, :, w + b * LANES:w + (b + 1) * LANES])
        return s_c, s_n

    def finish(b, j, s_c, s_n):
        rows = slice(b * t, (b + 1) * t)
        sl = slice(j * LANES, (j + 1) * LANES)
        m = jnp.maximum(jnp.max(s_c, axis=-1, keepdims=True),
                        jnp.max(s_n, axis=-1, keepdims=True))
        e_c = jnp.exp2(s_c - m).astype(BF16)
        e_n = jnp.exp2(s_n - m).astype(BF16)
        lhs_c = jnp.concatenate([cv_ref[b, sl, :].astype(BF16), ones], axis=0)
        lhs_n = jnp.concatenate([vt_ref[sl, :], ones[:, :LANES]], axis=0)
        out = (lax.dot_general(lhs_c, e_c, _NT, preferred_element_type=F32)
               + lax.dot_general(lhs_n, e_n, _NT, preferred_element_type=F32))
        x = out[:LANES] / out[LANES:LANES + 1]
        x_t = jnp.where(own_head, x, 0.0).T
        mixed_ref[rows, RET_W + j * LANES:RET_W + (j + 1) * LANES] = (
            x_t[:t] + x_t[t:]).astype(BF16)

    items = [(b, j) for b in range(streams) for j in range(ATTN_HEADS // 2)]
    ret_items = [(b, h) for b in range(streams) for h in range(RET_HEADS)]
    pending = [scores(*it) for it in items[:SAMPLE_LOOKAHEAD]]
    for n, (b, j) in enumerate(items):
        if n < len(ret_items):
            retention(*ret_items[n])
        if n + SAMPLE_LOOKAHEAD < len(items):
            pending.append(scores(*items[n + SAMPLE_LOOKAHEAD]))
        finish(b, j, *pending.pop(0))


def _mix_sample_call(qr, kr, vr, sg, qa, qb, ka, va_t, ck, cv, r0, gro, vb, t):
    n = qr.shape[0]
    nb, _, w = ck.shape
    g = LANES // t
    assert nb % g == 0
    row = pl.BlockSpec((g * t, RET_W), lambda b: (b, 0))
    col = pl.BlockSpec((ATT_W, g * t), lambda b: (0, b))
    cache = pl.BlockSpec((g, ATT_W, w), lambda b: (b, 0, 0))
    state = pl.BlockSpec((g, RET_HEADS, RET_DK, RET_DV), lambda b: (b, 0, 0, 0))
    return pl.pallas_call(
        functools.partial(_mix_sample_kernel, t=t),
        grid=(nb // g,),
        in_specs=[row] * 7 + [col, cache, cache, state,
                              _const_spec((1, RET_W)), _const_spec((ATTN_HEADS, BIAS_LANES))],
        out_specs=[pl.BlockSpec((g * t, D_MODEL), lambda b: (b, 0)), state],
        out_shape=[jax.ShapeDtypeStruct((n, D_MODEL), BF16),
                   jax.ShapeDtypeStruct((nb, RET_HEADS, RET_DK, RET_DV), F32)],
        scratch_shapes=[pltpu.VMEM((ATTN_HEADS // 2, 2 * t, w + g * LANES), F32),
                        pltpu.VMEM((RET_HEADS, t, t), F32),
                        pltpu.VMEM((t, RET_W), F32),
                        pltpu.VMEM((t, RET_W), F32)],
        compiler_params=pltpu.CompilerParams(dimension_semantics=("arbitrary",),
                                             vmem_limit_bytes=VMEM_LIMIT),
        name="mix_sample",
    )(qr, kr, vr, sg, qa, qb, ka, va_t, ck, cv, r0, gro, vb)


FF_SPLITS = (0, 1024, 2048, D_FF)


def _tail_kernel(x_ref, mixed_ref, p_ref, wout_ref, gffn_ref, wg_ref, wu_ref, wd_ref,
                 gple_ref, wpg_ref, wpp_ref, out_ref):
    ranges = list(zip(FF_SPLITS[:-1], FF_SPLITS[1:]))
    dot = functools.partial(jnp.dot, preferred_element_type=F32)

    for t0 in range(0, x_ref.shape[0], TM_TAIL_SUB):
        half = TM_TAIL_SUB // 2
        halves = (slice(t0, t0 + half), slice(t0 + half, t0 + TM_TAIL_SUB))
        local = (slice(0, half), slice(half, TM_TAIL_SUB))

        h = [x_ref[r, :] + dot(mixed_ref[r, :], wout_ref[...]) for r in halves]
        u = [_rms(hr, gffn_ref[...]).astype(BF16) for hr in h]

        c0, c1 = ranges[0]
        act = jnp.concatenate([(jax.nn.silu(dot(ur, wg_ref[:, c0:c1]))
                                * dot(ur, wu_ref[:, c0:c1])).astype(BF16) for ur in u], axis=0)
        ffn = dot(act, wd_ref[c0:c1, :])
        u = jnp.concatenate(u, axis=0)
        for c0, c1 in ranges[1:]:
            act = (jax.nn.silu(dot(u, wg_ref[:, c0:c1])) * dot(u, wu_ref[:, c0:c1])).astype(BF16)
            if (c0, c1) != ranges[-1]:
                ffn = ffn + dot(act, wd_ref[c0:c1, :])

        c0, c1 = ranges[-1]
        h = [hr + ffn[r, :] + dot(act[r, :], wd_ref[c0:c1, :]) for hr, r in zip(h, local)]
        ple = dot(p_ref[t0:t0 + TM_TAIL_SUB, :].astype(BF16), wpp_ref[...])
        for hr, r, rl in zip(h, halves, local):
            gate = jax.nn.sigmoid(dot(_rms(hr, gple_ref[...]).astype(BF16), wpg_ref[...]))
            out_ref[r, :] = hr + ple[rl, :] * gate


def _tail_call(x, mixed, p, w_out, g_ffn, w_gate, w_up, w_down, g_ple, w_pg, w_pp):
    n = x.shape[0]
    tm = TM_DENSE
    row = lambda w: pl.BlockSpec((tm, w), lambda i: (i, 0))
    return pl.pallas_call(
        _tail_kernel,
        grid=(n // tm,),
        in_specs=[row(D_MODEL), row(D_MODEL), row(PLE_DIM),
                  _const_spec((D_MODEL, D_MODEL)), _const_spec((1, D_MODEL)),
                  _const_spec((D_MODEL, D_FF)), _const_spec((D_MODEL, D_FF)),
                  _const_spec((D_FF, D_MODEL)), _const_spec((1, D_MODEL)),
                  _const_spec((D_MODEL, D_MODEL)), _const_spec((PLE_DIM, D_MODEL))],
        out_specs=row(D_MODEL),
        out_shape=jax.ShapeDtypeStruct((n, D_MODEL), F32),
        compiler_params=pltpu.CompilerParams(dimension_semantics=("arbitrary",),
                                             vmem_limit_bytes=VMEM_LIMIT),
        name="tail",
    )(x, mixed, p, w_out, g_ffn, w_gate, w_up, w_down, g_ple, w_pg, w_pp)


def _bias_rows(rel_bias):
    edge = jnp.broadcast_to(rel_bias[:, 2 * REL_CLIP:], (ATTN_HEADS, BIAS_LANES - 2 * REL_CLIP))
    by_query = jnp.concatenate([edge, rel_bias[:, 2 * REL_CLIP:0:-1]], axis=1)
    by_key = jnp.concatenate([rel_bias[:, :2 * REL_CLIP], edge], axis=1)
    return by_query, by_key


def kernel(x_prompt, x_sample, cache_attn_k, cache_attn_v, state_ret, p_prompt, p_sample,
           g_mix, w_in, g_ret_out, g_q_attn, g_k_attn, rel_bias, w_out, g_ffn,
           w_ffn_gate, w_ffn_up, w_ffn_down, g_ple, w_ple_gate, w_ple_proj):
    depth = w_in.shape[0]
    assert depth == 1, "single-layer trunk"
    batch, seq, _ = x_prompt.shape
    dec_batch, dec_seq, _ = x_sample.shape
    cache_w = cache_attn_k.shape[2]
    assert batch == 1 and seq % TM_PROJ == 0 and dec_seq == CHUNK and cache_w == BAND_PAST
    assert BAND_PAST <= seq, "the returned prompt keys are the tail of the last projection tile"

    row = lambda g: g.reshape(1, -1)
    gq = row(jnp.tile(g_q_attn[0], ATTN_HEADS))
    gk = row(jnp.tile(g_k_attn[0], ATTN_HEADS))
    gro = row(g_ret_out[0])
    vb_query, vb_key = _bias_rows(rel_bias[0])
    half = RET_DK // 2
    inv = ROPE_THETA ** (-jnp.arange(half, dtype=F32) / half)
    inv = row(jnp.concatenate([inv, inv]))
    w_in_b = w_in[0].astype(BF16)

    xp = x_prompt.reshape(seq, D_MODEL)
    (qr, kr, vr, sg, qa_t, qb_t, ka, va_t, k_tail, v_tail,
     w_out_b, w_gate_b, w_up_b, w_down_b, w_pg_b, w_pp_b) = _proj_call(
        xp, row(g_mix[0]), w_in_b, inv, gq, gk,
        pos0=0, pos_stride=TM_PROJ, period=TM_PROJ, prompt=True,
        cast_weights=(w_out[0], w_ffn_gate[0], w_ffn_up[0], w_ffn_down[0],
                      w_ple_gate[0], w_ple_proj[0]))
    tail_w_args = (w_out_b, row(g_ffn[0]), w_gate_b, w_up_b, w_down_b, row(g_ple[0]),
                   w_pg_b, w_pp_b)
    mixed_p, r_fin = _mix_prompt_call(qr, kr, vr, sg, qa_t, qb_t, ka, va_t, gro, vb_key)
    y_prompt = _tail_call(xp, mixed_p, p_prompt[0].reshape(seq, PLE_DIM), *tail_w_args)

    n_s = dec_batch * dec_seq
    xs = x_sample.reshape(n_s, D_MODEL)
    qr, kr, vr, sg, qa, qb, ka, va_t, k_new, v_new = _proj_call(
        xs, row(g_mix[0]), w_in_b, inv, gq, gk,
        pos0=PAST_LEN, pos_stride=0, period=dec_seq, prompt=False)
    feat_major = lambda c: c.transpose(0, 2, 3, 1).reshape(dec_batch, ATT_W, cache_w)
    mixed_s, r_new = _mix_sample_call(
        qr, kr, vr, sg, qa, qb, ka, va_t,
        feat_major(cache_attn_k[0]), feat_major(cache_attn_v[0]),
        state_ret[0], gro, vb_query, dec_seq)
    y_sample = _tail_call(xs, mixed_s, p_sample[0].reshape(n_s, PLE_DIM), *tail_w_args)

    kv_p = lambda a: a.reshape(ATTN_HEADS, ATTN_DH, BAND_PAST).transpose(2, 0, 1).reshape(
        1, batch, BAND_PAST, ATTN_HEADS, ATTN_DH)
    kv_s = lambda a: a.reshape(1, dec_batch, dec_seq, ATTN_HEADS, ATTN_DH)
    return (y_prompt.reshape(batch, seq, D_MODEL),
            y_sample.reshape(dec_batch, dec_seq, D_MODEL),
            r_fin.reshape(1, batch, RET_HEADS, RET_DK, RET_DV),
            kv_p(k_tail), kv_p(v_tail),
            r_new.reshape(1, dec_batch, RET_HEADS, RET_DK, RET_DV),
            kv_s(k_new), kv_s(v_new))
```

```python
import functools
import math

import jax
import jax.numpy as jnp
from jax import lax
from jax.experimental import pallas as pl
from jax.experimental.pallas import tpu as pltpu

F32 = jnp.float32
BF16 = jnp.bfloat16

D_MODEL = 1024
CHUNK = 64
PAST_LEN = 2048
RET_HEADS = 4
RET_DK = 128
RET_DV = 128
ATTN_HEADS = 8
ATTN_DH = 64
BAND_PAST_CHUNKS = 8
BAND_PAST = BAND_PAST_CHUNKS * CHUNK
REL_CLIP = 256
D_FF = 2816
PLE_DIM = 256
ROPE_THETA = 10000.0
EPS = 1e-6
RET_W = RET_HEADS * RET_DK
ATT_W = ATTN_HEADS * ATTN_DH
IN_COLS = 4 * RET_W + 3 * ATT_W
LOG_DECAY = tuple(math.log1p(-(2.0 ** (-5 - h))) for h in range(RET_HEADS))
LOG2E = math.log2(math.e)
MASKED = -1e30

LANES = 128
SUBLANES = 8
BF16_ROWS = 16
BIAS_LANES = 1024
TM_PROJ = 1024
TM_DENSE = 512
TQ_PROMPT = 256
DENOM_ROWS = BF16_ROWS
SAMPLE_LOOKAHEAD = 3
SCORE_LOOKAHEAD = 1
VMEM_LIMIT = 56 * 1024 * 1024

_NT = (((1,), (1,)), ((), ()))
_TN = (((0,), (0,)), ((), ()))


def _rms(x, g):
    return x * lax.rsqrt(jnp.mean(x * x, axis=-1, keepdims=True) + EPS) * g


def _const_spec(shape):
    zeros = (0,) * len(shape)
    return pl.BlockSpec(shape, lambda i: zeros, pipeline_mode=pl.Buffered(1))


def _proj_kernel(*refs, n_cast, pos0, pos_stride, period, prompt):
    x_ref, gmix_ref, w_ref, inv_ref, gq_ref, gk_ref = refs[:6]
    cast_in = refs[6:6 + n_cast]
    (qr_ref, kr_ref, vr_ref, sg_ref, qa_ref, qb_ref, ka_ref, vt_ref,
     ka32_ref, va32_ref) = refs[6 + n_cast:16 + n_cast]
    cast_out = refs[16 + n_cast:16 + 2 * n_cast]
    cos_scr, sin_scr = refs[16 + 2 * n_cast:]

    i = pl.program_id(0)
    tm = x_ref.shape[0]
    inv = inv_ref[...]

    @pl.when(i == 0)
    def _init():
        r = lax.broadcasted_iota(jnp.int32, (tm, LANES), 0) % period
        ang = r.astype(F32) * inv
        cos_scr[...] = jnp.cos(ang)
        sin_scr[...] = jnp.sin(ang)

    base = jnp.full((1, LANES), pos0 + i * pos_stride, jnp.int32).astype(F32) * inv
    cb, sb = jnp.cos(base), jnp.sin(base)
    cos = cb * cos_scr[...] - sb * sin_scr[...]
    sin = sb * cos_scr[...] + cb * sin_scr[...]
    lane = lax.broadcasted_iota(jnp.int32, (tm, LANES), 1)
    first_half = lane < LANES // 2
    sin = jnp.where(first_half, -sin, sin)

    xn = _rms(x_ref[...], gmix_ref[...]).astype(BF16)
    for src_ref, dst_ref in zip(cast_in, cast_out):
        dst_ref[...] = src_ref[...].astype(BF16)

    def proj(group):
        return jnp.dot(xn, w_ref[:, group * RET_W:(group + 1) * RET_W], preferred_element_type=F32)

    def rope_store(z, out_ref, scale):
        for h in range(RET_HEADS):
            sl = slice(h * RET_DK, (h + 1) * RET_DK)
            zh = z[:, sl]
            r = zh * cos + pltpu.roll(zh, RET_DK // 2, 1) * sin
            if scale is not None:
                r = r * scale
            out_ref[:, sl] = r.astype(out_ref.dtype)

    def head_rms(z, g_ref):
        outs = []
        for j in range(ATT_W // LANES):
            sl = slice(j * LANES, (j + 1) * LANES)
            zj = z[:, sl]
            zz = zj * zj
            s0 = jnp.sum(jnp.where(first_half, zz, 0.0), axis=-1, keepdims=True)
            s1 = jnp.sum(jnp.where(first_half, 0.0, zz), axis=-1, keepdims=True)
            ms = jnp.where(first_half, s0, s1) * (1.0 / ATTN_DH)
            outs.append(zj * lax.rsqrt(ms + EPS) * g_ref[:, sl])
        return jnp.concatenate(outs, axis=1)

    qs = head_rms(proj(4), gq_ref) * (ATTN_DH ** -0.5 * LOG2E)
    if prompt:
        qs = qs.T
        pair_first = lax.broadcasted_iota(jnp.int32, qs.shape, 0) % LANES < ATTN_DH
    else:
        pair_first = jnp.concatenate([first_half] * (ATT_W // LANES), axis=1)
    qa_ref[...] = jnp.where(pair_first, qs, 0.0).astype(BF16)
    qb_ref[...] = jnp.where(pair_first, 0.0, qs).astype(BF16)
    ka = head_rms(proj(5), gk_ref)
    ka_ref[...] = ka.astype(BF16)
    va = proj(6)
    va_t = va.T
    vt_ref[...] = va_t.astype(BF16)
    rope_store(proj(0), qr_ref, None)
    rope_store(proj(1), kr_ref, RET_DK ** -0.5)
    sg_ref[...] = jax.nn.silu(proj(3)).astype(BF16)
    vr_ref[...] = proj(2).astype(BF16)

    if prompt:
        @pl.when(i == pl.num_programs(0) - 1)
        def _tail():
            ka32_ref[...] = ka[tm - BAND_PAST:].T
            va32_ref[...] = va_t[:, tm - BAND_PAST:]
    else:
        ka32_ref[...] = ka
        va32_ref[...] = va


def _cast_block_rows(rows, steps):
    block = BF16_ROWS
    while rows % block or block * steps < rows:
        block += BF16_ROWS
    return block


def _proj_call(x, g_mix, w_in, inv, gq, gk, *, pos0, pos_stride, period, prompt, cast_weights=()):
    n = x.shape[0]
    tm = TM_PROJ
    cast_specs, cast_shapes = [], []
    for w in cast_weights:
        rows, cols = w.shape
        block = _cast_block_rows(rows, n // tm)
        last = rows // block - 1
        cast_specs.append(pl.BlockSpec((block, cols), lambda i, last=last: (jnp.minimum(i, last), 0)))
        cast_shapes.append(jax.ShapeDtypeStruct(w.shape, BF16))
    row = lambda w: pl.BlockSpec((tm, w), lambda i: (i, 0))
    col_spec = pl.BlockSpec((ATT_W, tm), lambda i: (0, i))
    bf = jax.ShapeDtypeStruct((n, RET_W), BF16)
    bf_t = jax.ShapeDtypeStruct((ATT_W, n), BF16)
    if prompt:
        assert tm >= BAND_PAST
        q_shape, q_spec = bf_t, col_spec
        kv32 = jax.ShapeDtypeStruct((ATT_W, BAND_PAST), F32)
        kv32_spec = pl.BlockSpec((ATT_W, BAND_PAST), lambda i: (0, 0))
    else:
        q_shape, q_spec = bf, row(ATT_W)
        kv32 = jax.ShapeDtypeStruct((n, ATT_W), F32)
        kv32_spec = row(ATT_W)
    body = functools.partial(_proj_kernel, n_cast=len(cast_weights), pos0=pos0,
                             pos_stride=pos_stride, period=period, prompt=prompt)
    return pl.pallas_call(
        body,
        grid=(n // tm,),
        in_specs=[row(D_MODEL), _const_spec((1, D_MODEL)), _const_spec((D_MODEL, IN_COLS)),
                  _const_spec((1, LANES)), _const_spec((1, ATT_W)), _const_spec((1, ATT_W))]
                 + cast_specs,
        out_specs=[row(RET_W)] * 4 + [q_spec] * 2 + [row(ATT_W), col_spec] + [kv32_spec] * 2
                  + cast_specs,
        out_shape=[bf] * 4 + [q_shape] * 2 + [bf, bf_t] + [kv32] * 2 + cast_shapes,
        scratch_shapes=[pltpu.VMEM((tm, LANES), F32), pltpu.VMEM((tm, LANES), F32)],
        compiler_params=pltpu.CompilerParams(dimension_semantics=("arbitrary",),
                                             vmem_limit_bytes=VMEM_LIMIT),
        name="proj",
    )(x, g_mix, w_in, inv, gq, gk, *cast_weights)


def _init_decay(dmat_scr, xi_scr, zeta_scr, t):
    n = lax.broadcasted_iota(jnp.int32, (t, t), 0)
    m = lax.broadcasted_iota(jnp.int32, (t, t), 1)
    diff = (n - m).astype(F32)
    pos = lax.broadcasted_iota(jnp.int32, (t, RET_DK), 0).astype(F32)
    for h in range(RET_HEADS):
        lg = LOG_DECAY[h]
        dmat_scr[h] = jnp.where(diff >= 0, jnp.exp(jnp.maximum(diff, 0.0) * lg), 0.0)
        sl = slice(h * RET_DK, (h + 1) * RET_DK)
        xi_scr[:, sl] = jnp.exp((pos + 1.0) * lg)
        zeta_scr[:, sl] = jnp.exp((t - 1.0 - pos) * lg)


def _toeplitz_rows(vb_ref, first_shift):
    r = lax.broadcasted_iota(jnp.int32, (ATTN_HEADS * SUBLANES, BIAS_LANES), 0)
    base = jnp.concatenate([jnp.broadcast_to(vb_ref[h:h + 1, :], (SUBLANES, BIAS_LANES))
                            for h in range(ATTN_HEADS)], axis=0) * LOG2E
    base = pltpu.roll(base, first_shift, 1)
    for shift in (1, 2, 4):
        base = jnp.where((r & shift) != 0, pltpu.roll(base, shift, 1), base)
    return [base[h * SUBLANES:(h + 1) * SUBLANES] for h in range(ATTN_HEADS)]


def _init_bias_keys(vb_ref, bias_scr):
    _, n_keys, t = bias_scr.shape
    lane = lax.broadcasted_iota(jnp.int32, (SUBLANES, t), 1)
    for h, base in enumerate(_toeplitz_rows(vb_ref, BIAS_LANES - n_keys)):
        for g in range(n_keys // SUBLANES):
            rows = (pltpu.roll(base, SUBLANES * g, 1) if g else base)[:, :t]
            kc = SUBLANES * g // CHUNK
            lo = max(kc - BAND_PAST_CHUNKS, 0) * CHUNK
            hi = (min(kc, t // CHUNK - 1) + 1) * CHUNK
            rows = jnp.where(jnp.logical_and(lane >= lo, lane < hi), rows, MASKED)
            bias_scr[h, g * SUBLANES:(g + 1) * SUBLANES, :] = rows


def _init_bias_queries(vb_ref, bias_scr, t, w):
    first = lax.broadcasted_iota(jnp.int32, (SUBLANES, LANES), 1) < t
    for h, base in enumerate(_toeplitz_rows(vb_ref, BIAS_LANES - REL_CLIP)):
        for g in range(t // SUBLANES):
            rows = pltpu.roll(base, SUBLANES * g, 1) if g else base
            shifted = pltpu.roll(base, SUBLANES * g + t, 1)
            r0 = (h % 2) * t + g * SUBLANES
            bias_scr[h // 2, r0:r0 + SUBLANES, :w] = rows[:, :w]
            bias_scr[h // 2, r0:r0 + SUBLANES, w:w + LANES] = jnp.where(
                first, rows[:, w:w + LANES], MASKED)
            bias_scr[h // 2, r0:r0 + SUBLANES, w + LANES:] = jnp.where(
                first, MASKED, shifted[:, w:w + LANES])


def _retention(q, k, v, r, dmat, xi, zeta, decay_t):
    s = lax.dot_general(q, k, _NT, preferred_element_type=F32) * dmat
    inner = jnp.dot(s.astype(BF16), v, preferred_element_type=F32)
    cross = jnp.dot(q, r.astype(BF16), preferred_element_type=F32) * xi
    kz = (k.astype(F32) * zeta).astype(BF16)
    r_new = r * decay_t + lax.dot_general(kz, v, _TN, preferred_element_type=F32)
    return inner + cross, r_new


def _column_max(s):
    rows = s.shape[0]
    while rows % (2 * SUBLANES) == 0:
        rows //= 2
        s = jnp.maximum(s[:rows], s[rows:])
    return jnp.max(s, axis=0, keepdims=True)


def _mix_prompt_kernel(qr_ref, kr_ref, vr_ref, sg_ref, qa_ref, qb_ref, k0_ref, k1_ref, k2_ref,
                       v0_ref, v1_ref, v2_ref, gro_ref, vb_ref,
                       mixed_ref, rfin_ref,
                       r_scr, bias_scr, dmat_scr, xi_scr, zeta_scr):
    i = pl.program_id(0)
    t = TQ_PROMPT
    n_keys = bias_scr.shape[1]

    @pl.when(i == 0)
    def _init():
        r_scr[...] = jnp.zeros_like(r_scr)
        _init_decay(dmat_scr, xi_scr, zeta_scr, t)
        _init_bias_keys(vb_ref, bias_scr)

    def retention(h):
        sl = slice(h * RET_DK, (h + 1) * RET_DK)
        o, r_new = _retention(qr_ref[:, sl], kr_ref[:, sl], vr_ref[:, sl], r_scr[h],
                              dmat_scr[h], xi_scr[:, sl], zeta_scr[:, sl],
                              math.exp(t * LOG_DECAY[h]))
        r_scr[h] = r_new
        mixed_ref[:, sl] = (sg_ref[:, sl].astype(F32) * _rms(o, gro_ref[:, sl])).astype(BF16)

    ones = jnp.ones((DENOM_ROWS, n_keys), BF16)

    def scores(j, start_mask):
        sl = slice(j * LANES, (j + 1) * LANES)
        k = jnp.concatenate([k0_ref[:, sl], k1_ref[:, sl], k2_ref[:, sl]], axis=0)
        q = jnp.concatenate([qa_ref[sl, :], qb_ref[sl, :]], axis=1)
        bias = jnp.concatenate([bias_scr[2 * j], bias_scr[2 * j + 1]], axis=1)
        s = jnp.dot(k, q, preferred_element_type=F32) + bias
        if start_mask is not None:
            s = s + jnp.concatenate([start_mask, start_mask], axis=1)
        return s

    def attention(start_mask):
        pending = [scores(j, start_mask) for j in range(SCORE_LOOKAHEAD)]
        for j in range(ATTN_HEADS // 2):
            retention(j)
            if j + SCORE_LOOKAHEAD < ATTN_HEADS // 2:
                pending.append(scores(j + SCORE_LOOKAHEAD, start_mask))
            s = pending.pop(0)
            e = jnp.exp2(s - _column_max(s)).astype(BF16)
            halves = []
            for half in range(2):
                rows = slice((2 * j + half) * ATTN_DH, (2 * j + half + 1) * ATTN_DH)
                v_t = jnp.concatenate([v0_ref[rows, :], v1_ref[rows, :], v2_ref[rows, :]], axis=1)
                out = jnp.dot(jnp.concatenate([v_t, ones], axis=0), e[:, half * t:(half + 1) * t],
                              preferred_element_type=F32)
                halves.append(out[:ATTN_DH] / out[ATTN_DH:ATTN_DH + 1])
            o_t = jnp.concatenate(halves, axis=0)
            mixed_ref[:, RET_W + j * LANES:RET_W + (j + 1) * LANES] = o_t.T.astype(BF16)

    @pl.when(i < 2)
    def _start():
        blk = lax.broadcasted_iota(jnp.int32, (n_keys, t), 0) // t
        attention(jnp.where(blk + i >= 2, 0.0, MASKED))

    @pl.when(i >= 2)
    def _steady():
        attention(None)

    @pl.when(i == pl.num_programs(0) - 1)
    def _fin():
        rfin_ref[...] = r_scr[...]


def _mix_prompt_call(qr, kr, vr, sg, qa_t, qb_t, ka, va_t, gro, vb):
    n = qr.shape[0]
    t = TQ_PROMPT
    row = pl.BlockSpec((t, RET_W), lambda i: (i, 0))
    col = pl.BlockSpec((ATT_W, t), lambda i: (0, i))
    back = lambda d: pl.BlockSpec((t, ATT_W), lambda i: (jnp.maximum(i - d, 0), 0))
    back_t = lambda d: pl.BlockSpec((ATT_W, t), lambda i: (0, jnp.maximum(i - d, 0)))
    state = (RET_HEADS, RET_DK, RET_DV)
    return pl.pallas_call(
        _mix_prompt_kernel,
        grid=(n // t,),
        in_specs=[row] * 4 + [col] * 2 + [back(2), back(1), back(0)]
                 + [back_t(2), back_t(1), back_t(0)]
                 + [_const_spec((1, RET_W)), _const_spec((ATTN_HEADS, BIAS_LANES))],
        out_specs=[pl.BlockSpec((t, D_MODEL), lambda i: (i, 0)),
                   pl.BlockSpec(state, lambda i: (0, 0, 0))],
        out_shape=[jax.ShapeDtypeStruct((n, D_MODEL), BF16), jax.ShapeDtypeStruct(state, F32)],
        scratch_shapes=[pltpu.VMEM(state, F32),
                        pltpu.VMEM((ATTN_HEADS, 3 * t, t), F32),
                        pltpu.VMEM((RET_HEADS, t, t), F32),
                        pltpu.VMEM((t, RET_W), F32),
                        pltpu.VMEM((t, RET_W), F32)],
        compiler_params=pltpu.CompilerParams(dimension_semantics=("arbitrary",),
                                             vmem_limit_bytes=VMEM_LIMIT),
        name="mix_prompt",
    )(qr, kr, vr, sg, qa_t, qb_t, ka, ka, ka, va_t, va_t, va_t, gro, vb)


def _mix_sample_kernel(qr_ref, kr_ref, vr_ref, sg_ref, qa_ref, qb_ref, ka_ref, vt_ref, ck_ref, cv_ref,
                       r0_ref, gro_ref, vb_ref,
                       mixed_ref, rnew_ref,
                       bias_scr, dmat_scr, xi_scr, zeta_scr, *, t):
    streams, _, w = ck_ref.shape
    assert streams * t == LANES

    @pl.when(pl.program_id(0) == 0)
    def _init():
        _init_decay(dmat_scr, xi_scr, zeta_scr, t)
        _init_bias_queries(vb_ref, bias_scr, t, w)

    ones = jnp.ones((DENOM_ROWS, w), BF16)
    own_head = ((lax.broadcasted_iota(jnp.int32, (LANES, 2 * t), 0) < ATTN_DH)
                == (lax.broadcasted_iota(jnp.int32, (LANES, 2 * t), 1) < t))

    def retention(b, h):
        rows = slice(b * t, (b + 1) * t)
        sl = slice(h * RET_DK, (h + 1) * RET_DK)
        o, r_new = _retention(qr_ref[rows, sl], kr_ref[rows, sl], vr_ref[rows, sl], r0_ref[b, h],
                              dmat_scr[h], xi_scr[:, sl], zeta_scr[:, sl],
                              math.exp(t * LOG_DECAY[h]))
        rnew_ref[b, h] = r_new
        mixed_ref[rows, sl] = (sg_ref[rows, sl].astype(F32) * _rms(o, gro_ref[:, sl])).astype(BF16)

    def scores(b, j):
        rows = slice(b * t, (b + 1) * t)
        sl = slice(j * LANES, (j + 1) * LANES)
        q2 = jnp.concatenate([qa_ref[rows, sl], qb_ref[rows, sl]], axis=0)
        s_c = (jnp.dot(q2, ck_ref[b, sl, :].astype(BF16), preferred_element_type=F32)
               + bias_scr[j, :, :w])
        s_n = (lax.dot_general(q2, ka_ref[:, sl], _NT, preferred_element_type=F32)
               + bias_scr[j, :, w + b * LANES:w + (b + 1) * LANES])
        return s_c, s_n

    def finish(b, j, s_c, s_n):
        rows = slice(b * t, (b + 1) * t)
        sl = slice(j * LANES, (j + 1) * LANES)
        m = jnp.maximum(jnp.max(s_c, axis=-1, keepdims=True),
                        jnp.max(s_n, axis=-1, keepdims=True))
        e_c = jnp.exp2(s_c - m).astype(BF16)
        e_n = jnp.exp2(s_n - m).astype(BF16)
        lhs_c = jnp.concatenate([cv_ref[b, sl, :].astype(BF16), ones], axis=0)
        lhs_n = jnp.concatenate([vt_ref[sl, :], ones[:, :LANES]], axis=0)
        out = (lax.dot_general(lhs_c, e_c, _NT, preferred_element_type=F32)
               + lax.dot_general(lhs_n, e_n, _NT, preferred_element_type=F32))
        x = out[:LANES] / out[LANES:LANES + 1]
        x_t = jnp.where(own_head, x, 0.0).T
        mixed_ref[rows, RET_W + j * LANES:RET_W + (j + 1) * LANES] = (
            x_t[:t] + x_t[t:]).astype(BF16)

    items = [(b, j) for b in range(streams) for j in range(ATTN_HEADS // 2)]
    ret_items = [(b, h) for b in range(streams) for h in range(RET_HEADS)]
    pending = [scores(*it) for it in items[:SAMPLE_LOOKAHEAD]]
    for n, (b, j) in enumerate(items):
        if n < len(ret_items):
            retention(*ret_items[n])
        if n + SAMPLE_LOOKAHEAD < len(items):
            pending.append(scores(*items[n + SAMPLE_LOOKAHEAD]))
        finish(b, j, *pending.pop(0))


def _mix_sample_call(qr, kr, vr, sg, qa, qb, ka, va_t, ck, cv, r0, gro, vb, t):
    n = qr.shape[0]
    nb, _, w = ck.shape
    g = LANES // t
    assert nb % g == 0
    row = pl.BlockSpec((g * t, RET_W), lambda b: (b, 0))
    col = pl.BlockSpec((ATT_W, g * t), lambda b: (0, b))
    cache = pl.BlockSpec((g, ATT_W, w), lambda b: (b, 0, 0))
    state = pl.BlockSpec((g, RET_HEADS, RET_DK, RET_DV), lambda b: (b, 0, 0, 0))
    return pl.pallas_call(
        functools.partial(_mix_sample_kernel, t=t),
        grid=(nb // g,),
        in_specs=[row] * 7 + [col, cache, cache, state,
                              _const_spec((1, RET_W)), _const_spec((ATTN_HEADS, BIAS_LANES))],
        out_specs=[pl.BlockSpec((g * t, D_MODEL), lambda b: (b, 0)), state],
        out_shape=[jax.ShapeDtypeStruct((n, D_MODEL), BF16),
                   jax.ShapeDtypeStruct((nb, RET_HEADS, RET_DK, RET_DV), F32)],
        scratch_shapes=[pltpu.VMEM((ATTN_HEADS // 2, 2 * t, w + g * LANES), F32),
                        pltpu.VMEM((RET_HEADS, t, t), F32),
                        pltpu.VMEM((t, RET_W), F32),
                        pltpu.VMEM((t, RET_W), F32)],
        compiler_params=pltpu.CompilerParams(dimension_semantics=("arbitrary",),
                                             vmem_limit_bytes=VMEM_LIMIT),
        name="mix_sample",
    )(qr, kr, vr, sg, qa, qb, ka, va_t, ck, cv, r0, gro, vb)


FF_SPLITS = (0, 1024, 2048, D_FF)


def _tail_kernel(x_ref, mixed_ref, p_ref, wout_ref, gffn_ref, wg_ref, wu_ref, wd_ref,
                 gple_ref, wpg_ref, wpp_ref, out_ref):
    tm = x_ref.shape[0]
    halves = (slice(0, tm // 2), slice(tm // 2, tm))
    ranges = list(zip(FF_SPLITS[:-1], FF_SPLITS[1:]))
    dot = functools.partial(jnp.dot, preferred_element_type=F32)

    h = [x_ref[r, :] + dot(mixed_ref[r, :], wout_ref[...]) for r in halves]
    u = [_rms(hr, gffn_ref[...]).astype(BF16) for hr in h]

    c0, c1 = ranges[0]
    act = jnp.concatenate([(jax.nn.silu(dot(ur, wg_ref[:, c0:c1])) * dot(ur, wu_ref[:, c0:c1])
                            ).astype(BF16) for ur in u], axis=0)
    ffn = dot(act, wd_ref[c0:c1, :])
    u = jnp.concatenate(u, axis=0)
    for c0, c1 in ranges[1:]:
        act = (jax.nn.silu(dot(u, wg_ref[:, c0:c1])) * dot(u, wu_ref[:, c0:c1])).astype(BF16)
        if (c0, c1) != ranges[-1]:
            ffn = ffn + dot(act, wd_ref[c0:c1, :])

    c0, c1 = ranges[-1]
    h = [hr + ffn[r, :] + dot(act[r, :], wd_ref[c0:c1, :]) for hr, r in zip(h, halves)]
    ple = dot(p_ref[...].astype(BF16), wpp_ref[...])
    for hr, r in zip(h, halves):
        gate = jax.nn.sigmoid(dot(_rms(hr, gple_ref[...]).astype(BF16), wpg_ref[...]))
        out_ref[r, :] = hr + ple[r, :] * gate


def _tail_call(x, mixed, p, w_out, g_ffn, w_gate, w_up, w_down, g_ple, w_pg, w_pp):
    n = x.shape[0]
    tm = TM_DENSE
    row = lambda w: pl.BlockSpec((tm, w), lambda i: (i, 0))
    return pl.pallas_call(
        _tail_kernel,
        grid=(n // tm,),
        in_specs=[row(D_MODEL), row(D_MODEL), row(PLE_DIM),
                  _const_spec((D_MODEL, D_MODEL)), _const_spec((1, D_MODEL)),
                  _const_spec((D_MODEL, D_FF)), _const_spec((D_MODEL, D_FF)),
                  _const_spec((D_FF, D_MODEL)), _const_spec((1, D_MODEL)),
                  _const_spec((D_MODEL, D_MODEL)), _const_spec((PLE_DIM, D_MODEL))],
        out_specs=row(D_MODEL),
        out_shape=jax.ShapeDtypeStruct((n, D_MODEL), F32),
        compiler_params=pltpu.CompilerParams(dimension_semantics=("arbitrary",),
                                             vmem_limit_bytes=VMEM_LIMIT),
        name="tail",
    )(x, mixed, p, w_out, g_ffn, w_gate, w_up, w_down, g_ple, w_pg, w_pp)


def _bias_rows(rel_bias):
    edge = jnp.broadcast_to(rel_bias[:, 2 * REL_CLIP:], (ATTN_HEADS, BIAS_LANES - 2 * REL_CLIP))
    by_query = jnp.concatenate([edge, rel_bias[:, 2 * REL_CLIP:0:-1]], axis=1)
    by_key = jnp.concatenate([rel_bias[:, :2 * REL_CLIP], edge], axis=1)
    return by_query, by_key


def kernel(x_prompt, x_sample, cache_attn_k, cache_attn_v, state_ret, p_prompt, p_sample,
           g_mix, w_in, g_ret_out, g_q_attn, g_k_attn, rel_bias, w_out, g_ffn,
           w_ffn_gate, w_ffn_up, w_ffn_down, g_ple, w_ple_gate, w_ple_proj):
    depth = w_in.shape[0]
    assert depth == 1, "single-layer trunk"
    batch, seq, _ = x_prompt.shape
    dec_batch, dec_seq, _ = x_sample.shape
    cache_w = cache_attn_k.shape[2]
    assert batch == 1 and seq % TM_PROJ == 0 and dec_seq == CHUNK and cache_w == BAND_PAST
    assert BAND_PAST <= seq, "the returned prompt keys are the tail of the last projection tile"

    row = lambda g: g.reshape(1, -1)
    gq = row(jnp.tile(g_q_attn[0], ATTN_HEADS))
    gk = row(jnp.tile(g_k_attn[0], ATTN_HEADS))
    gro = row(g_ret_out[0])
    vb_query, vb_key = _bias_rows(rel_bias[0])
    half = RET_DK // 2
    inv = ROPE_THETA ** (-jnp.arange(half, dtype=F32) / half)
    inv = row(jnp.concatenate([inv, inv]))
    w_in_b = w_in[0].astype(BF16)

    xp = x_prompt.reshape(seq, D_MODEL)
    (qr, kr, vr, sg, qa_t, qb_t, ka, va_t, k_tail, v_tail,
     w_out_b, w_gate_b, w_up_b, w_down_b, w_pg_b, w_pp_b) = _proj_call(
        xp, row(g_mix[0]), w_in_b, inv, gq, gk,
        pos0=0, pos_stride=TM_PROJ, period=TM_PROJ, prompt=True,
        cast_weights=(w_out[0], w_ffn_gate[0], w_ffn_up[0], w_ffn_down[0],
                      w_ple_gate[0], w_ple_proj[0]))
    tail_w_args = (w_out_b, row(g_ffn[0]), w_gate_b, w_up_b, w_down_b, row(g_ple[0]),
                   w_pg_b, w_pp_b)
    mixed_p, r_fin = _mix_prompt_call(qr, kr, vr, sg, qa_t, qb_t, ka, va_t, gro, vb_key)
    y_prompt = _tail_call(xp, mixed_p, p_prompt[0].reshape(seq, PLE_DIM), *tail_w_args)

    n_s = dec_batch * dec_seq
    xs = x_sample.reshape(n_s, D_MODEL)
    qr, kr, vr, sg, qa, qb, ka, va_t, k_new, v_new = _proj_call(
        xs, row(g_mix[0]), w_in_b, inv, gq, gk,
        pos0=PAST_LEN, pos_stride=0, period=dec_seq, prompt=False)
    feat_major = lambda c: c.transpose(0, 2, 3, 1).reshape(dec_batch, ATT_W, cache_w)
    mixed_s, r_new = _mix_sample_call(
        qr, kr, vr, sg, qa, qb, ka, va_t,
        feat_major(cache_attn_k[0]), feat_major(cache_attn_v[0]),
        state_ret[0], gro, vb_query, dec_seq)
    y_sample = _tail_call(xs, mixed_s, p_sample[0].reshape(n_s, PLE_DIM), *tail_w_args)

    kv_p = lambda a: a.reshape(ATTN_HEADS, ATTN_DH, BAND_PAST).transpose(2, 0, 1).reshape(
        1, batch, BAND_PAST, ATTN_HEADS, ATTN_DH)
    kv_s = lambda a: a.reshape(1, dec_batch, dec_seq, ATTN_HEADS, ATTN_DH)
    return (y_prompt.reshape(batch, seq, D_MODEL),
            y_sample.reshape(dec_batch, dec_seq, D_MODEL),
            r_fin.reshape(1, batch, RET_HEADS, RET_DK, RET_DV),
            kv_p(k_tail), kv_p(v_tail),
            r_new.reshape(1, dec_batch, RET_HEADS, RET_DK, RET_DV),
            kv_s(k_new), kv_s(v_new))
```

```python
import functools
import math

import jax
import jax.numpy as jnp
from jax import lax
from jax.experimental import pallas as pl
from jax.experimental.pallas import tpu as pltpu

F32 = jnp.float32
BF16 = jnp.bfloat16

D_MODEL = 1024
CHUNK = 64
PAST_LEN = 2048
RET_HEADS = 4
RET_DK = 128
RET_DV = 128
ATTN_HEADS = 8
ATTN_DH = 64
BAND_PAST_CHUNKS = 8
BAND_PAST = BAND_PAST_CHUNKS * CHUNK
REL_CLIP = 256
D_FF = 2816
PLE_DIM = 256
ROPE_THETA = 10000.0
EPS = 1e-6
RET_W = RET_HEADS * RET_DK
ATT_W = ATTN_HEADS * ATTN_DH
IN_COLS = 4 * RET_W + 3 * ATT_W
LOG_DECAY = tuple(math.log1p(-(2.0 ** (-5 - h))) for h in range(RET_HEADS))
LOG2E = math.log2(math.e)
MASKED = -1e30

LANES = 128
SUBLANES = 8
BF16_ROWS = 16
BIAS_LANES = 1024
TM_PROJ = 1024
TM_DENSE = 512
TQ_PROMPT = 256
DENOM_ROWS = BF16_ROWS
SAMPLE_LOOKAHEAD = 3
SCORE_LOOKAHEAD = 1
VMEM_LIMIT = 56 * 1024 * 1024

_NT = (((1,), (1,)), ((), ()))
_TN = (((0,), (0,)), ((), ()))


def _rms(x, g):
    return x * lax.rsqrt(jnp.mean(x * x, axis=-1, keepdims=True) + EPS) * g


def _const_spec(shape):
    zeros = (0,) * len(shape)
    return pl.BlockSpec(shape, lambda i: zeros, pipeline_mode=pl.Buffered(1))


def _proj_kernel(*refs, n_cast, pos0, pos_stride, period, prompt):
    x_ref, gmix_ref, w_ref, inv_ref, gq_ref, gk_ref = refs[:6]
    cast_in = refs[6:6 + n_cast]
    (qr_ref, kr_ref, vr_ref, sg_ref, qa_ref, qb_ref, ka_ref, vt_ref,
     ka32_ref, va32_ref) = refs[6 + n_cast:16 + n_cast]
    cast_out = refs[16 + n_cast:16 + 2 * n_cast]
    cos_scr, sin_scr = refs[16 + 2 * n_cast:]

    i = pl.program_id(0)
    tm = x_ref.shape[0]
    inv = inv_ref[...]

    @pl.when(i == 0)
    def _init():
        r = lax.broadcasted_iota(jnp.int32, (tm, LANES), 0) % period
        ang = r.astype(F32) * inv
        cos_scr[...] = jnp.cos(ang)
        sin_scr[...] = jnp.sin(ang)

    base = jnp.full((1, LANES), pos0 + i * pos_stride, jnp.int32).astype(F32) * inv
    cb, sb = jnp.cos(base), jnp.sin(base)
    cos = cb * cos_scr[...] - sb * sin_scr[...]
    sin = sb * cos_scr[...] + cb * sin_scr[...]
    lane = lax.broadcasted_iota(jnp.int32, (tm, LANES), 1)
    first_half = lane < LANES // 2
    sin = jnp.where(first_half, -sin, sin)

    xn = _rms(x_ref[...], gmix_ref[...]).astype(BF16)
    for src_ref, dst_ref in zip(cast_in, cast_out):
        dst_ref[...] = src_ref[...].astype(BF16)

    def proj(group):
        return jnp.dot(xn, w_ref[:, group * RET_W:(group + 1) * RET_W], preferred_element_type=F32)

    def rope_store(z, out_ref, scale):
        for h in range(RET_HEADS):
            sl = slice(h * RET_DK, (h + 1) * RET_DK)
            zh = z[:, sl]
            r = zh * cos + pltpu.roll(zh, RET_DK // 2, 1) * sin
            if scale is not None:
                r = r * scale
            out_ref[:, sl] = r.astype(out_ref.dtype)

    def head_rms(z, g_ref):
        outs = []
        for j in range(ATT_W // LANES):
            sl = slice(j * LANES, (j + 1) * LANES)
            zj = z[:, sl]
            zz = zj * zj
            s0 = jnp.sum(jnp.where(first_half, zz, 0.0), axis=-1, keepdims=True)
            s1 = jnp.sum(jnp.where(first_half, 0.0, zz), axis=-1, keepdims=True)
            ms = jnp.where(first_half, s0, s1) * (1.0 / ATTN_DH)
            outs.append(zj * lax.rsqrt(ms + EPS) * g_ref[:, sl])
        return jnp.concatenate(outs, axis=1)

    def store_blocks(ref, val_t):
        width = ref.shape[2]
        for b in range(ref.shape[0]):
            ref[b] = val_t[:, b * width:(b + 1) * width]

    qs = head_rms(proj(4), gq_ref) * (ATTN_DH ** -0.5 * LOG2E)
    if prompt:
        qs = qs.T
        pair_first = lax.broadcasted_iota(jnp.int32, qs.shape, 0) % LANES < ATTN_DH
        store_blocks(qa_ref, jnp.where(pair_first, qs, 0.0).astype(BF16))
        store_blocks(qb_ref, jnp.where(pair_first, 0.0, qs).astype(BF16))
    else:
        pair_first = jnp.concatenate([first_half] * (ATT_W // LANES), axis=1)
        qa_ref[...] = jnp.where(pair_first, qs, 0.0).astype(BF16)
        qb_ref[...] = jnp.where(pair_first, 0.0, qs).astype(BF16)
    ka = head_rms(proj(5), gk_ref)
    ka_ref[...] = ka.astype(BF16)
    va = proj(6)
    va_t = va.T
    store_blocks(vt_ref, va_t.astype(BF16))
    rope_store(proj(0), qr_ref, None)
    rope_store(proj(1), kr_ref, RET_DK ** -0.5)
    sg_ref[...] = jax.nn.silu(proj(3)).astype(BF16)
    vr_ref[...] = proj(2).astype(BF16)

    if prompt:
        @pl.when(i == pl.num_programs(0) - 1)
        def _tail():
            ka32_ref[...] = ka[tm - BAND_PAST:].T
            va32_ref[...] = va_t[:, tm - BAND_PAST:]
    else:
        ka32_ref[...] = ka
        va32_ref[...] = va


def _cast_block_rows(rows, steps):
    block = BF16_ROWS
    while rows % block or block * steps < rows:
        block += BF16_ROWS
    return block


def _proj_call(x, g_mix, w_in, inv, gq, gk, *, pos0, pos_stride, period, prompt, feat_block,
               cast_weights=()):
    n = x.shape[0]
    tm = TM_PROJ
    cast_specs, cast_shapes = [], []
    for w in cast_weights:
        rows, cols = w.shape
        block = _cast_block_rows(rows, n // tm)
        last = rows // block - 1
        cast_specs.append(pl.BlockSpec((block, cols), lambda i, last=last: (jnp.minimum(i, last), 0)))
        cast_shapes.append(jax.ShapeDtypeStruct(w.shape, BF16))
    row = lambda w: pl.BlockSpec((tm, w), lambda i: (i, 0))
    col_spec = pl.BlockSpec((tm // feat_block, ATT_W, feat_block), lambda i: (i, 0, 0))
    bf = jax.ShapeDtypeStruct((n, RET_W), BF16)
    bf_t = jax.ShapeDtypeStruct((n // feat_block, ATT_W, feat_block), BF16)
    if prompt:
        assert tm >= BAND_PAST
        q_shape, q_spec = bf_t, col_spec
        kv32 = jax.ShapeDtypeStruct((ATT_W, BAND_PAST), F32)
        kv32_spec = pl.BlockSpec((ATT_W, BAND_PAST), lambda i: (0, 0))
    else:
        q_shape, q_spec = bf, row(ATT_W)
        kv32 = jax.ShapeDtypeStruct((n, ATT_W), F32)
        kv32_spec = row(ATT_W)
    body = functools.partial(_proj_kernel, n_cast=len(cast_weights), pos0=pos0,
                             pos_stride=pos_stride, period=period, prompt=prompt)
    return pl.pallas_call(
        body,
        grid=(n // tm,),
        in_specs=[row(D_MODEL), _const_spec((1, D_MODEL)), _const_spec((D_MODEL, IN_COLS)),
                  _const_spec((1, LANES)), _const_spec((1, ATT_W)), _const_spec((1, ATT_W))]
                 + cast_specs,
        out_specs=[row(RET_W)] * 4 + [q_spec] * 2 + [row(ATT_W), col_spec] + [kv32_spec] * 2
                  + cast_specs,
        out_shape=[bf] * 4 + [q_shape] * 2 + [bf, bf_t] + [kv32] * 2 + cast_shapes,
        scratch_shapes=[pltpu.VMEM((tm, LANES), F32), pltpu.VMEM((tm, LANES), F32)],
        compiler_params=pltpu.CompilerParams(dimension_semantics=("arbitrary",),
                                             vmem_limit_bytes=VMEM_LIMIT),
        name="proj",
    )(x, g_mix, w_in, inv, gq, gk, *cast_weights)


def _init_decay(dmat_scr, xi_scr, zeta_scr, t):
    n = lax.broadcasted_iota(jnp.int32, (t, t), 0)
    m = lax.broadcasted_iota(jnp.int32, (t, t), 1)
    diff = (n - m).astype(F32)
    pos = lax.broadcasted_iota(jnp.int32, (t, RET_DK), 0).astype(F32)
    for h in range(RET_HEADS):
        lg = LOG_DECAY[h]
        dmat_scr[h] = jnp.where(diff >= 0, jnp.exp(jnp.maximum(diff, 0.0) * lg), 0.0)
        sl = slice(h * RET_DK, (h + 1) * RET_DK)
        xi_scr[:, sl] = jnp.exp((pos + 1.0) * lg)
        zeta_scr[:, sl] = jnp.exp((t - 1.0 - pos) * lg)


def _toeplitz_rows(vb_ref, first_shift):
    r = lax.broadcasted_iota(jnp.int32, (ATTN_HEADS * SUBLANES, BIAS_LANES), 0)
    base = jnp.concatenate([jnp.broadcast_to(vb_ref[h:h + 1, :], (SUBLANES, BIAS_LANES))
                            for h in range(ATTN_HEADS)], axis=0) * LOG2E
    base = pltpu.roll(base, first_shift, 1)
    for shift in (1, 2, 4):
        base = jnp.where((r & shift) != 0, pltpu.roll(base, shift, 1), base)
    return [base[h * SUBLANES:(h + 1) * SUBLANES] for h in range(ATTN_HEADS)]


def _init_bias_keys(vb_ref, bias_scr):
    _, n_keys, t = bias_scr.shape
    lane = lax.broadcasted_iota(jnp.int32, (SUBLANES, t), 1)
    for h, base in enumerate(_toeplitz_rows(vb_ref, BIAS_LANES - n_keys)):
        for g in range(n_keys // SUBLANES):
            rows = (pltpu.roll(base, SUBLANES * g, 1) if g else base)[:, :t]
            kc = SUBLANES * g // CHUNK
            lo = max(kc - BAND_PAST_CHUNKS, 0) * CHUNK
            hi = (min(kc, t // CHUNK - 1) + 1) * CHUNK
            rows = jnp.where(jnp.logical_and(lane >= lo, lane < hi), rows, MASKED)
            bias_scr[h, g * SUBLANES:(g + 1) * SUBLANES, :] = rows


def _init_bias_queries(vb_ref, bias_scr, t, w):
    first = lax.broadcasted_iota(jnp.int32, (SUBLANES, LANES), 1) < t
    for h, base in enumerate(_toeplitz_rows(vb_ref, BIAS_LANES - REL_CLIP)):
        for g in range(t // SUBLANES):
            rows = pltpu.roll(base, SUBLANES * g, 1) if g else base
            shifted = pltpu.roll(base, SUBLANES * g + t, 1)
            r0 = (h % 2) * t + g * SUBLANES
            bias_scr[h // 2, r0:r0 + SUBLANES, :w] = rows[:, :w]
            bias_scr[h // 2, r0:r0 + SUBLANES, w:w + LANES] = jnp.where(
                first, rows[:, w:w + LANES], MASKED)
            bias_scr[h // 2, r0:r0 + SUBLANES, w + LANES:] = jnp.where(
                first, MASKED, shifted[:, w:w + LANES])


def _retention(q, k, v, r, dmat, xi, zeta, decay_t):
    s = lax.dot_general(q, k, _NT, preferred_element_type=F32) * dmat
    inner = jnp.dot(s.astype(BF16), v, preferred_element_type=F32)
    cross = jnp.dot(q, r.astype(BF16), preferred_element_type=F32) * xi
    kz = (k.astype(F32) * zeta).astype(BF16)
    r_new = r * decay_t + lax.dot_general(kz, v, _TN, preferred_element_type=F32)
    return inner + cross, r_new


def _column_max(s):
    rows = s.shape[0]
    while rows % (2 * SUBLANES) == 0:
        rows //= 2
        s = jnp.maximum(s[:rows], s[rows:])
    return jnp.max(s, axis=0, keepdims=True)


def _mix_prompt_kernel(qr_ref, kr_ref, vr_ref, sg_ref, qa_ref, qb_ref, k0_ref, k1_ref, k2_ref,
                       v0_ref, v1_ref, v2_ref, gro_ref, vb_ref,
                       mixed_ref, rfin_ref,
                       r_scr, bias_scr, dmat_scr, xi_scr, zeta_scr):
    i = pl.program_id(0)
    t = TQ_PROMPT
    n_keys = bias_scr.shape[1]

    @pl.when(i == 0)
    def _init():
        r_scr[...] = jnp.zeros_like(r_scr)
        _init_decay(dmat_scr, xi_scr, zeta_scr, t)
        _init_bias_keys(vb_ref, bias_scr)

    def retention(h):
        sl = slice(h * RET_DK, (h + 1) * RET_DK)
        o, r_new = _retention(qr_ref[:, sl], kr_ref[:, sl], vr_ref[:, sl], r_scr[h],
                              dmat_scr[h], xi_scr[:, sl], zeta_scr[:, sl],
                              math.exp(t * LOG_DECAY[h]))
        r_scr[h] = r_new
        mixed_ref[:, sl] = (sg_ref[:, sl].astype(F32) * _rms(o, gro_ref[:, sl])).astype(BF16)

    ones = jnp.ones((DENOM_ROWS, n_keys), BF16)

    def scores(j, start_mask):
        sl = slice(j * LANES, (j + 1) * LANES)
        k = jnp.concatenate([k0_ref[:, sl], k1_ref[:, sl], k2_ref[:, sl]], axis=0)
        q = jnp.concatenate([qa_ref[0, sl, :], qb_ref[0, sl, :]], axis=1)
        bias = jnp.concatenate([bias_scr[2 * j], bias_scr[2 * j + 1]], axis=1)
        s = jnp.dot(k, q, preferred_element_type=F32) + bias
        if start_mask is not None:
            s = s + jnp.concatenate([start_mask, start_mask], axis=1)
        return s

    def attention(start_mask):
        pending = [scores(j, start_mask) for j in range(SCORE_LOOKAHEAD)]
        for j in range(ATTN_HEADS // 2):
            retention(j)
            if j + SCORE_LOOKAHEAD < ATTN_HEADS // 2:
                pending.append(scores(j + SCORE_LOOKAHEAD, start_mask))
            s = pending.pop(0)
            e = jnp.exp2(s - _column_max(s)).astype(BF16)
            halves = []
            for half in range(2):
                rows = slice((2 * j + half) * ATTN_DH, (2 * j + half + 1) * ATTN_DH)
                v_t = jnp.concatenate([v0_ref[0, rows, :], v1_ref[0, rows, :], v2_ref[0, rows, :]],
                                      axis=1)
                out = jnp.dot(jnp.concatenate([v_t, ones], axis=0), e[:, half * t:(half + 1) * t],
                              preferred_element_type=F32)
                halves.append(out[:ATTN_DH] / out[ATTN_DH:ATTN_DH + 1])
            o_t = jnp.concatenate(halves, axis=0)
            mixed_ref[:, RET_W + j * LANES:RET_W + (j + 1) * LANES] = o_t.T.astype(BF16)

    @pl.when(i < 2)
    def _start():
        blk = lax.broadcasted_iota(jnp.int32, (n_keys, t), 0) // t
        attention(jnp.where(blk + i >= 2, 0.0, MASKED))

    @pl.when(i >= 2)
    def _steady():
        attention(None)

    @pl.when(i == pl.num_programs(0) - 1)
    def _fin():
        rfin_ref[...] = r_scr[...]


def _mix_prompt_call(qr, kr, vr, sg, qa_t, qb_t, ka, va_t, gro, vb):
    n = qr.shape[0]
    t = TQ_PROMPT
    row = pl.BlockSpec((t, RET_W), lambda i: (i, 0))
    col = pl.BlockSpec((1, ATT_W, t), lambda i: (i, 0, 0))
    back = lambda d: pl.BlockSpec((t, ATT_W), lambda i: (jnp.maximum(i - d, 0), 0))
    back_t = lambda d: pl.BlockSpec((1, ATT_W, t), lambda i: (jnp.maximum(i - d, 0), 0, 0))
    state = (RET_HEADS, RET_DK, RET_DV)
    return pl.pallas_call(
        _mix_prompt_kernel,
        grid=(n // t,),
        in_specs=[row] * 4 + [col] * 2 + [back(2), back(1), back(0)]
                 + [back_t(2), back_t(1), back_t(0)]
                 + [_const_spec((1, RET_W)), _const_spec((ATTN_HEADS, BIAS_LANES))],
        out_specs=[pl.BlockSpec((t, D_MODEL), lambda i: (i, 0)),
                   pl.BlockSpec(state, lambda i: (0, 0, 0))],
        out_shape=[jax.ShapeDtypeStruct((n, D_MODEL), BF16), jax.ShapeDtypeStruct(state, F32)],
        scratch_shapes=[pltpu.VMEM(state, F32),
                        pltpu.VMEM((ATTN_HEADS, 3 * t, t), F32),
                        pltpu.VMEM((RET_HEADS, t, t), F32),
                        pltpu.VMEM((t, RET_W), F32),
                        pltpu.VMEM((t, RET_W), F32)],
        compiler_params=pltpu.CompilerParams(dimension_semantics=("arbitrary",),
                                             vmem_limit_bytes=VMEM_LIMIT),
        name="mix_prompt",
    )(qr, kr, vr, sg, qa_t, qb_t, ka, ka, ka, va_t, va_t, va_t, gro, vb)


def _mix_sample_kernel(qr_ref, kr_ref, vr_ref, sg_ref, qa_ref, qb_ref, ka_ref, vt_ref, ck_ref, cv_ref,
                       r0_ref, gro_ref, vb_ref,
                       mixed_ref, rnew_ref,
                       bias_scr, dmat_scr, xi_scr, zeta_scr, *, t):
    streams, _, w = ck_ref.shape
    assert streams * t == LANES

    @pl.when(pl.program_id(0) == 0)
    def _init():
        _init_decay(dmat_scr, xi_scr, zeta_scr, t)
        _init_bias_queries(vb_ref, bias_scr, t, w)

    ones = jnp.ones((DENOM_ROWS, w), BF16)
    own_head = ((lax.broadcasted_iota(jnp.int32, (LANES, 2 * t), 0) < ATTN_DH)
                == (lax.broadcasted_iota(jnp.int32, (LANES, 2 * t), 1) < t))

    def retention(b, h):
        rows = slice(b * t, (b + 1) * t)
        sl = slice(h * RET_DK, (h + 1) * RET_DK)
        o, r_new = _retention(qr_ref[rows, sl], kr_ref[rows, sl], vr_ref[rows, sl], r0_ref[b, h],
                              dmat_scr[h], xi_scr[:, sl], zeta_scr[:, sl],
                              math.exp(t * LOG_DECAY[h]))
        rnew_ref[b, h] = r_new
        mixed_ref[rows, sl] = (sg_ref[rows, sl].astype(F32) * _rms(o, gro_ref[:, sl])).astype(BF16)

    def scores(b, j):
        rows = slice(b * t, (b + 1) * t)
        sl = slice(j * LANES, (j + 1) * LANES)
        q2 = jnp.concatenate([qa_ref[rows, sl], qb_ref[rows, sl]], axis=0)
        s_c = (jnp.dot(q2, ck_ref[b, sl, :].astype(BF16), preferred_element_type=F32)
               + bias_scr[j, :, :w])
        s_n = (lax.dot_general(q2, ka_ref[:, sl], _NT, preferred_element_type=F32)
               + bias_scr[j, :, w + b * LANES:w + (b + 1) * LANES])
        return s_c, s_n

    def finish(b, j, s_c, s_n):
        rows = slice(b * t, (b + 1) * t)
        sl = slice(j * LANES, (j + 1) * LANES)
        m = jnp.maximum(jnp.max(s_c, axis=-1, keepdims=True),
                        jnp.max(s_n, axis=-1, keepdims=True))
        e_c = jnp.exp2(s_c - m).astype(BF16)
        e_n = jnp.exp2(s_n - m).astype(BF16)
        lhs_c = jnp.concatenate([cv_ref[b, sl, :].astype(BF16), ones], axis=0)
        lhs_n = jnp.concatenate([vt_ref[0, sl, :], ones[:, :LANES]], axis=0)
        out = (lax.dot_general(lhs_c, e_c, _NT, preferred_element_type=F32)
               + lax.dot_general(lhs_n, e_n, _NT, preferred_element_type=F32))
        x = out[:LANES] / out[LANES:LANES + 1]
        x_t = jnp.where(own_head, x, 0.0).T
        mixed_ref[rows, RET_W + j * LANES:RET_W + (j + 1) * LANES] = (
            x_t[:t] + x_t[t:]).astype(BF16)

    items = [(b, j) for b in range(streams) for j in range(ATTN_HEADS // 2)]
    ret_items = [(b, h) for b in range(streams) for h in range(RET_HEADS)]
    pending = [scores(*it) for it in items[:SAMPLE_LOOKAHEAD]]
    for n, (b, j) in enumerate(items):
        if n < len(ret_items):
            retention(*ret_items[n])
        if n + SAMPLE_LOOKAHEAD < len(items):
            pending.append(scores(*items[n + SAMPLE_LOOKAHEAD]))
        finish(b, j, *pending.pop(0))


def _mix_sample_call(qr, kr, vr, sg, qa, qb, ka, va_t, ck, cv, r0, gro, vb, t):
    n = qr.shape[0]
    nb, _, w = ck.shape
    g = LANES // t
    assert nb % g == 0
    row = pl.BlockSpec((g * t, RET_W), lambda b: (b, 0))
    col = pl.BlockSpec((1, ATT_W, g * t), lambda b: (b, 0, 0))
    cache = pl.BlockSpec((g, ATT_W, w), lambda b: (b, 0, 0))
    state = pl.BlockSpec((g, RET_HEADS, RET_DK, RET_DV), lambda b: (b, 0, 0, 0))
    return pl.pallas_call(
        functools.partial(_mix_sample_kernel, t=t),
        grid=(nb // g,),
        in_specs=[row] * 7 + [col, cache, cache, state,
                              _const_spec((1, RET_W)), _const_spec((ATTN_HEADS, BIAS_LANES))],
        out_specs=[pl.BlockSpec((g * t, D_MODEL), lambda b: (b, 0)), state],
        out_shape=[jax.ShapeDtypeStruct((n, D_MODEL), BF16),
                   jax.ShapeDtypeStruct((nb, RET_HEADS, RET_DK, RET_DV), F32)],
        scratch_shapes=[pltpu.VMEM((ATTN_HEADS // 2, 2 * t, w + g * LANES), F32),
                        pltpu.VMEM((RET_HEADS, t, t), F32),
                        pltpu.VMEM((t, RET_W), F32),
                        pltpu.VMEM((t, RET_W), F32)],
        compiler_params=pltpu.CompilerParams(dimension_semantics=("arbitrary",),
                                             vmem_limit_bytes=VMEM_LIMIT),
        name="mix_sample",
    )(qr, kr, vr, sg, qa, qb, ka, va_t, ck, cv, r0, gro, vb)


FF_SPLITS = (0, 1024, 2048, D_FF)


def _tail_kernel(x_ref, mixed_ref, p_ref, wout_ref, gffn_ref, wg_ref, wu_ref, wd_ref,
                 gple_ref, wpg_ref, wpp_ref, out_ref):
    tm = x_ref.shape[0]
    halves = (slice(0, tm // 2), slice(tm // 2, tm))
    ranges = list(zip(FF_SPLITS[:-1], FF_SPLITS[1:]))
    dot = functools.partial(jnp.dot, preferred_element_type=F32)

    h = [x_ref[r, :] + dot(mixed_ref[r, :], wout_ref[...]) for r in halves]
    u = [_rms(hr, gffn_ref[...]).astype(BF16) for hr in h]

    c0, c1 = ranges[0]
    act = jnp.concatenate([(jax.nn.silu(dot(ur, wg_ref[:, c0:c1])) * dot(ur, wu_ref[:, c0:c1])
                            ).astype(BF16) for ur in u], axis=0)
    ffn = dot(act, wd_ref[c0:c1, :])
    u = jnp.concatenate(u, axis=0)
    for c0, c1 in ranges[1:]:
        act = (jax.nn.silu(dot(u, wg_ref[:, c0:c1])) * dot(u, wu_ref[:, c0:c1])).astype(BF16)
        if (c0, c1) != ranges[-1]:
            ffn = ffn + dot(act, wd_ref[c0:c1, :])

    c0, c1 = ranges[-1]
    h = [hr + ffn[r, :] + dot(act[r, :], wd_ref[c0:c1, :]) for hr, r in zip(h, halves)]
    ple = dot(p_ref[...].astype(BF16), wpp_ref[...])
    for hr, r in zip(h, halves):
        gate = jax.nn.sigmoid(dot(_rms(hr, gple_ref[...]).astype(BF16), wpg_ref[...]))
        out_ref[r, :] = hr + ple[r, :] * gate


def _tail_call(x, mixed, p, w_out, g_ffn, w_gate, w_up, w_down, g_ple, w_pg, w_pp):
    n = x.shape[0]
    tm = TM_DENSE
    row = lambda w: pl.BlockSpec((tm, w), lambda i: (i, 0))
    return pl.pallas_call(
        _tail_kernel,
        grid=(n // tm,),
        in_specs=[row(D_MODEL), row(D_MODEL), row(PLE_DIM),
                  _const_spec((D_MODEL, D_MODEL)), _const_spec((1, D_MODEL)),
                  _const_spec((D_MODEL, D_FF)), _const_spec((D_MODEL, D_FF)),
                  _const_spec((D_FF, D_MODEL)), _const_spec((1, D_MODEL)),
                  _const_spec((D_MODEL, D_MODEL)), _const_spec((PLE_DIM, D_MODEL))],
        out_specs=row(D_MODEL),
        out_shape=jax.ShapeDtypeStruct((n, D_MODEL), F32),
        compiler_params=pltpu.CompilerParams(dimension_semantics=("arbitrary",),
                                             vmem_limit_bytes=VMEM_LIMIT),
        name="tail",
    )(x, mixed, p, w_out, g_ffn, w_gate, w_up, w_down, g_ple, w_pg, w_pp)


def _bias_rows(rel_bias):
    edge = jnp.broadcast_to(rel_bias[:, 2 * REL_CLIP:], (ATTN_HEADS, BIAS_LANES - 2 * REL_CLIP))
    by_query = jnp.concatenate([edge, rel_bias[:, 2 * REL_CLIP:0:-1]], axis=1)
    by_key = jnp.concatenate([rel_bias[:, :2 * REL_CLIP], edge], axis=1)
    return by_query, by_key


def kernel(x_prompt, x_sample, cache_attn_k, cache_attn_v, state_ret, p_prompt, p_sample,
           g_mix, w_in, g_ret_out, g_q_attn, g_k_attn, rel_bias, w_out, g_ffn,
           w_ffn_gate, w_ffn_up, w_ffn_down, g_ple, w_ple_gate, w_ple_proj):
    depth = w_in.shape[0]
    assert depth == 1, "single-layer trunk"
    batch, seq, _ = x_prompt.shape
    dec_batch, dec_seq, _ = x_sample.shape
    cache_w = cache_attn_k.shape[2]
    assert batch == 1 and seq % TM_PROJ == 0 and dec_seq == CHUNK and cache_w == BAND_PAST
    assert BAND_PAST <= seq, "the returned prompt keys are the tail of the last projection tile"

    row = lambda g: g.reshape(1, -1)
    gq = row(jnp.tile(g_q_attn[0], ATTN_HEADS))
    gk = row(jnp.tile(g_k_attn[0], ATTN_HEADS))
    gro = row(g_ret_out[0])
    vb_query, vb_key = _bias_rows(rel_bias[0])
    half = RET_DK // 2
    inv = ROPE_THETA ** (-jnp.arange(half, dtype=F32) / half)
    inv = row(jnp.concatenate([inv, inv]))
    w_in_b = w_in[0].astype(BF16)

    xp = x_prompt.reshape(seq, D_MODEL)
    (qr, kr, vr, sg, qa_t, qb_t, ka, va_t, k_tail, v_tail,
     w_out_b, w_gate_b, w_up_b, w_down_b, w_pg_b, w_pp_b) = _proj_call(
        xp, row(g_mix[0]), w_in_b, inv, gq, gk,
        pos0=0, pos_stride=TM_PROJ, period=TM_PROJ, prompt=True, feat_block=TQ_PROMPT,
        cast_weights=(w_out[0], w_ffn_gate[0], w_ffn_up[0], w_ffn_down[0],
                      w_ple_gate[0], w_ple_proj[0]))
    tail_w_args = (w_out_b, row(g_ffn[0]), w_gate_b, w_up_b, w_down_b, row(g_ple[0]),
                   w_pg_b, w_pp_b)
    mixed_p, r_fin = _mix_prompt_call(qr, kr, vr, sg, qa_t, qb_t, ka, va_t, gro, vb_key)
    y_prompt = _tail_call(xp, mixed_p, p_prompt[0].reshape(seq, PLE_DIM), *tail_w_args)

    n_s = dec_batch * dec_seq
    xs = x_sample.reshape(n_s, D_MODEL)
    qr, kr, vr, sg, qa, qb, ka, va_t, k_new, v_new = _proj_call(
        xs, row(g_mix[0]), w_in_b, inv, gq, gk,
        pos0=PAST_LEN, pos_stride=0, period=dec_seq, prompt=False, feat_block=LANES)
    feat_major = lambda c: c.transpose(0, 2, 3, 1).reshape(dec_batch, ATT_W, cache_w)
    mixed_s, r_new = _mix_sample_call(
        qr, kr, vr, sg, qa, qb, ka, va_t,
        feat_major(cache_attn_k[0]), feat_major(cache_attn_v[0]),
        state_ret[0], gro, vb_query, dec_seq)
    y_sample = _tail_call(xs, mixed_s, p_sample[0].reshape(n_s, PLE_DIM), *tail_w_args)

    kv_p = lambda a: a.reshape(ATTN_HEADS, ATTN_DH, BAND_PAST).transpose(2, 0, 1).reshape(
        1, batch, BAND_PAST, ATTN_HEADS, ATTN_DH)
    kv_s = lambda a: a.reshape(1, dec_batch, dec_seq, ATTN_HEADS, ATTN_DH)
    return (y_prompt.reshape(batch, seq, D_MODEL),
            y_sample.reshape(dec_batch, dec_seq, D_MODEL),
            r_fin.reshape(1, batch, RET_HEADS, RET_DK, RET_DV),
            kv_p(k_tail), kv_p(v_tail),
            r_new.reshape(1, dec_batch, RET_HEADS, RET_DK, RET_DV),
            kv_s(k_new), kv_s(v_new))
```

```python
import functools
import math

import jax
import jax.numpy as jnp
from jax import lax
from jax.experimental import pallas as pl
from jax.experimental.pallas import tpu as pltpu

F32 = jnp.float32
BF16 = jnp.bfloat16

D_MODEL = 1024
CHUNK = 64
PAST_LEN = 2048
RET_HEADS = 4
RET_DK = 128
RET_DV = 128
ATTN_HEADS = 8
ATTN_DH = 64
BAND_PAST_CHUNKS = 8
BAND_PAST = BAND_PAST_CHUNKS * CHUNK
REL_CLIP = 256
D_FF = 2816
PLE_DIM = 256
ROPE_THETA = 10000.0
EPS = 1e-6
RET_W = RET_HEADS * RET_DK
ATT_W = ATTN_HEADS * ATTN_DH
IN_COLS = 4 * RET_W + 3 * ATT_W
LOG_DECAY = tuple(math.log1p(-(2.0 ** (-5 - h))) for h in range(RET_HEADS))
LOG2E = math.log2(math.e)
MASKED = -1e30

LANES = 128
SUBLANES = 8
BF16_ROWS = 16
BIAS_LANES = 1024
TM_PROJ = 512
PROJ_GROUP_ORDER = (4, 5, 6, 0, 1, 3, 2)
TM_DENSE = 512
TQ_PROMPT = 256
PROMPT_BLOCKS_PER_STEP = 2
DENOM_ROWS = BF16_ROWS
SAMPLE_LOOKAHEAD = 3
SCORE_LOOKAHEAD = 1
VMEM_LIMIT = 56 * 1024 * 1024

_NT = (((1,), (1,)), ((), ()))
_TN = (((0,), (0,)), ((), ()))


def _rms(x, g):
    return x * lax.rsqrt(jnp.mean(x * x, axis=-1, keepdims=True) + EPS) * g


def _const_spec(shape):
    zeros = (0,) * len(shape)
    return pl.BlockSpec(shape, lambda i: zeros, pipeline_mode=pl.Buffered(1))


def _proj_kernel(*refs, n_cast, pos0, pos_stride, period, prompt):
    x_ref, gmix_ref, inv_ref, gq_ref, gk_ref = refs[:5]
    cast_in = refs[5:5 + n_cast]
    w_hbm = refs[5 + n_cast]
    (qr_ref, kr_ref, vr_ref, sg_ref, qa_ref, qb_ref, ka_ref, vt_ref,
     ka32_ref, va32_ref) = refs[6 + n_cast:16 + n_cast]
    cast_out = refs[16 + n_cast:16 + 2 * n_cast]
    cos_scr, sin_scr, w_ref, stage_ref, sems = refs[16 + 2 * n_cast:]

    i = pl.program_id(0)
    tm = x_ref.shape[0]
    inv = inv_ref[...]
    group_cols = lambda g: slice(g * RET_W, (g + 1) * RET_W)
    copies = {g: pltpu.make_async_copy(w_hbm.at[:, group_cols(g)], stage_ref.at[n], sems.at[n])
              for n, g in enumerate(PROJ_GROUP_ORDER)}

    def body(first_step):
        base = jnp.full((1, LANES), pos0 + i * pos_stride, jnp.int32).astype(F32) * inv
        cb, sb = jnp.cos(base), jnp.sin(base)
        cos = cb * cos_scr[...] - sb * sin_scr[...]
        sin = sb * cos_scr[...] + cb * sin_scr[...]
        lane = lax.broadcasted_iota(jnp.int32, (tm, LANES), 1)
        first_half = lane < LANES // 2
        sin = jnp.where(first_half, -sin, sin)

        xn = _rms(x_ref[...], gmix_ref[...]).astype(BF16)
        for src_ref, dst_ref in zip(cast_in, cast_out):
            dst_ref[...] = src_ref[...].astype(BF16)

        def proj(group):
            cols = group_cols(group)
            if first_step:
                copies[group].wait()
                w_ref[:, cols] = stage_ref[PROJ_GROUP_ORDER.index(group)].astype(BF16)
            return jnp.dot(xn, w_ref[:, cols], preferred_element_type=F32)

        def rope_store(z, out_ref, scale):
            for h in range(RET_HEADS):
                sl = slice(h * RET_DK, (h + 1) * RET_DK)
                zh = z[:, sl]
                r = zh * cos + pltpu.roll(zh, RET_DK // 2, 1) * sin
                if scale is not None:
                    r = r * scale
                out_ref[:, sl] = r.astype(out_ref.dtype)

        def head_rms(z, g_ref):
            outs = []
            for j in range(ATT_W // LANES):
                sl = slice(j * LANES, (j + 1) * LANES)
                zj = z[:, sl]
                zz = zj * zj
                s0 = jnp.sum(jnp.where(first_half, zz, 0.0), axis=-1, keepdims=True)
                s1 = jnp.sum(jnp.where(first_half, 0.0, zz), axis=-1, keepdims=True)
                ms = jnp.where(first_half, s0, s1) * (1.0 / ATTN_DH)
                outs.append(zj * lax.rsqrt(ms + EPS) * g_ref[:, sl])
            return jnp.concatenate(outs, axis=1)

        def store_blocks(ref, val_t):
            width = ref.shape[2]
            for b in range(ref.shape[0]):
                ref[b] = val_t[:, b * width:(b + 1) * width]

        qs = head_rms(proj(4), gq_ref) * (ATTN_DH ** -0.5 * LOG2E)
        if prompt:
            qs = qs.T
            pair_first = lax.broadcasted_iota(jnp.int32, qs.shape, 0) % LANES < ATTN_DH
            store_blocks(qa_ref, jnp.where(pair_first, qs, 0.0).astype(BF16))
            store_blocks(qb_ref, jnp.where(pair_first, 0.0, qs).astype(BF16))
        else:
            pair_first = jnp.concatenate([first_half] * (ATT_W // LANES), axis=1)
            qa_ref[...] = jnp.where(pair_first, qs, 0.0).astype(BF16)
            qb_ref[...] = jnp.where(pair_first, 0.0, qs).astype(BF16)
        ka = head_rms(proj(5), gk_ref)
        ka_ref[...] = ka.astype(BF16)
        va = proj(6)
        va_t = va.T
        store_blocks(vt_ref, va_t.astype(BF16))
        rope_store(proj(0), qr_ref, None)
        rope_store(proj(1), kr_ref, RET_DK ** -0.5)
        sg_ref[...] = jax.nn.silu(proj(3)).astype(BF16)
        vr_ref[...] = proj(2).astype(BF16)

        if prompt:
            @pl.when(i == pl.num_programs(0) - 1)
            def _tail():
                ka32_ref[...] = ka[tm - BAND_PAST:].T
                va32_ref[...] = va_t[:, tm - BAND_PAST:]
        else:
            ka32_ref[...] = ka
            va32_ref[...] = va

    @pl.when(i == 0)
    def _first():
        for g in PROJ_GROUP_ORDER:
            copies[g].start()
        r = lax.broadcasted_iota(jnp.int32, (tm, LANES), 0) % period
        ang = r.astype(F32) * inv
        cos_scr[...] = jnp.cos(ang)
        sin_scr[...] = jnp.sin(ang)
        body(True)

    @pl.when(i > 0)
    def _rest():
        body(False)


def _cast_block_rows(rows, steps):
    block = BF16_ROWS
    while rows % block or block * steps < rows:
        block += BF16_ROWS
    return block


def _proj_call(x, g_mix, w_in, inv, gq, gk, *, pos0, pos_stride, period, prompt, feat_block,
               cast_weights=()):
    n = x.shape[0]
    tm = TM_PROJ
    cast_specs, cast_shapes = [], []
    for w in cast_weights:
        rows, cols = w.shape
        block = _cast_block_rows(rows, n // tm)
        last = rows // block - 1
        cast_specs.append(pl.BlockSpec((block, cols), lambda i, last=last: (jnp.minimum(i, last), 0)))
        cast_shapes.append(jax.ShapeDtypeStruct(w.shape, BF16))
    row = lambda w: pl.BlockSpec((tm, w), lambda i: (i, 0))
    col_spec = pl.BlockSpec((tm // feat_block, ATT_W, feat_block), lambda i: (i, 0, 0))
    bf = jax.ShapeDtypeStruct((n, RET_W), BF16)
    bf_t = jax.ShapeDtypeStruct((n // feat_block, ATT_W, feat_block), BF16)
    if prompt:
        assert tm >= BAND_PAST
        q_shape, q_spec = bf_t, col_spec
        kv32 = jax.ShapeDtypeStruct((ATT_W, BAND_PAST), F32)
        kv32_spec = pl.BlockSpec((ATT_W, BAND_PAST), lambda i: (0, 0))
    else:
        q_shape, q_spec = bf, row(ATT_W)
        kv32 = jax.ShapeDtypeStruct((n, ATT_W), F32)
        kv32_spec = row(ATT_W)
    body = functools.partial(_proj_kernel, n_cast=len(cast_weights), pos0=pos0,
                             pos_stride=pos_stride, period=period, prompt=prompt)
    return pl.pallas_call(
        body,
        grid=(n // tm,),
        in_specs=[row(D_MODEL), _const_spec((1, D_MODEL)),
                  _const_spec((1, LANES)), _const_spec((1, ATT_W)), _const_spec((1, ATT_W))]
                 + cast_specs + [pl.BlockSpec(memory_space=pl.ANY)],
        out_specs=[row(RET_W)] * 4 + [q_spec] * 2 + [row(ATT_W), col_spec] + [kv32_spec] * 2
                  + cast_specs,
        out_shape=[bf] * 4 + [q_shape] * 2 + [bf, bf_t] + [kv32] * 2 + cast_shapes,
        scratch_shapes=[pltpu.VMEM((tm, LANES), F32), pltpu.VMEM((tm, LANES), F32),
                        pltpu.VMEM((D_MODEL, IN_COLS), BF16),
                        pltpu.VMEM((len(PROJ_GROUP_ORDER), D_MODEL, RET_W), F32),
                        pltpu.SemaphoreType.DMA((len(PROJ_GROUP_ORDER),))],
        compiler_params=pltpu.CompilerParams(dimension_semantics=("arbitrary",),
                                             vmem_limit_bytes=VMEM_LIMIT),
        name="proj",
    )(x, g_mix, inv, gq, gk, *cast_weights, w_in)


def _init_decay(dmat_scr, xi_scr, zeta_scr, t):
    n = lax.broadcasted_iota(jnp.int32, (t, t), 0)
    m = lax.broadcasted_iota(jnp.int32, (t, t), 1)
    diff = (n - m).astype(F32)
    pos = lax.broadcasted_iota(jnp.int32, (t, RET_DK), 0).astype(F32)
    for h in range(RET_HEADS):
        lg = LOG_DECAY[h]
        dmat_scr[h] = jnp.where(diff >= 0, jnp.exp(jnp.maximum(diff, 0.0) * lg), 0.0)
        sl = slice(h * RET_DK, (h + 1) * RET_DK)
        xi_scr[:, sl] = jnp.exp((pos + 1.0) * lg)
        zeta_scr[:, sl] = jnp.exp((t - 1.0 - pos) * lg)


def _toeplitz_rows(vb_ref, first_shift):
    r = lax.broadcasted_iota(jnp.int32, (ATTN_HEADS * SUBLANES, BIAS_LANES), 0)
    base = jnp.concatenate([jnp.broadcast_to(vb_ref[h:h + 1, :], (SUBLANES, BIAS_LANES))
                            for h in range(ATTN_HEADS)], axis=0) * LOG2E
    base = pltpu.roll(base, first_shift, 1)
    for shift in (1, 2, 4):
        base = jnp.where((r & shift) != 0, pltpu.roll(base, shift, 1), base)
    return [base[h * SUBLANES:(h + 1) * SUBLANES] for h in range(ATTN_HEADS)]


def _init_bias_keys(vb_ref, bias_scr):
    _, n_keys, t = bias_scr.shape
    lane = lax.broadcasted_iota(jnp.int32, (SUBLANES, t), 1)
    for h, base in enumerate(_toeplitz_rows(vb_ref, BIAS_LANES - n_keys)):
        for g in range(n_keys // SUBLANES):
            rows = (pltpu.roll(base, SUBLANES * g, 1) if g else base)[:, :t]
            kc = SUBLANES * g // CHUNK
            lo = max(kc - BAND_PAST_CHUNKS, 0) * CHUNK
            hi = (min(kc, t // CHUNK - 1) + 1) * CHUNK
            rows = jnp.where(jnp.logical_and(lane >= lo, lane < hi), rows, MASKED)
            bias_scr[h, g * SUBLANES:(g + 1) * SUBLANES, :] = rows


def _init_bias_queries(vb_ref, bias_scr, t, w):
    first = lax.broadcasted_iota(jnp.int32, (SUBLANES, LANES), 1) < t
    for h, base in enumerate(_toeplitz_rows(vb_ref, BIAS_LANES - REL_CLIP)):
        for g in range(t // SUBLANES):
            rows = pltpu.roll(base, SUBLANES * g, 1) if g else base
            shifted = pltpu.roll(base, SUBLANES * g + t, 1)
            r0 = (h % 2) * t + g * SUBLANES
            bias_scr[h // 2, r0:r0 + SUBLANES, :w] = rows[:, :w]
            bias_scr[h // 2, r0:r0 + SUBLANES, w:w + LANES] = jnp.where(
                first, rows[:, w:w + LANES], MASKED)
            bias_scr[h // 2, r0:r0 + SUBLANES, w + LANES:] = jnp.where(
                first, MASKED, shifted[:, w:w + LANES])


def _retention(q, k, v, r, dmat, xi, zeta, decay_t):
    s = lax.dot_general(q, k, _NT, preferred_element_type=F32) * dmat
    inner = jnp.dot(s.astype(BF16), v, preferred_element_type=F32)
    cross = jnp.dot(q, r.astype(BF16), preferred_element_type=F32) * xi
    kz = (k.astype(F32) * zeta).astype(BF16)
    r_new = r * decay_t + lax.dot_general(kz, v, _TN, preferred_element_type=F32)
    return inner + cross, r_new


def _column_max(s):
    rows = s.shape[0]
    while rows % (2 * SUBLANES) == 0:
        rows //= 2
        s = jnp.maximum(s[:rows], s[rows:])
    return jnp.max(s, axis=0, keepdims=True)


def _mix_prompt_kernel(qr_ref, kr_ref, vr_ref, sg_ref, qa_ref, qb_ref,
                       k0_ref, k1_ref, k2_ref, k3_ref, v0_ref, v1_ref, v2_ref, v3_ref,
                       gro_ref, vb_ref,
                       mixed_ref, rfin_ref,
                       r_scr, bias_scr, dmat_scr, xi_scr, zeta_scr):
    i = pl.program_id(0)
    t = TQ_PROMPT
    n_keys = bias_scr.shape[1]
    k_refs = (k0_ref, k1_ref, k2_ref, k3_ref)
    v_refs = (v0_ref, v1_ref, v2_ref, v3_ref)

    @pl.when(i == 0)
    def _init():
        r_scr[...] = jnp.zeros_like(r_scr)
        _init_decay(dmat_scr, xi_scr, zeta_scr, t)
        _init_bias_keys(vb_ref, bias_scr)

    ones = jnp.ones((DENOM_ROWS, n_keys), BF16)

    def query_block(b, start_mask):
        rows = slice(b * t, (b + 1) * t)
        ks, vs = k_refs[b:b + 3], v_refs[b:b + 3]

        def retention(h):
            sl = slice(h * RET_DK, (h + 1) * RET_DK)
            o, r_new = _retention(qr_ref[rows, sl], kr_ref[rows, sl], vr_ref[rows, sl], r_scr[h],
                                  dmat_scr[h], xi_scr[:, sl], zeta_scr[:, sl],
                                  math.exp(t * LOG_DECAY[h]))
            r_scr[h] = r_new
            mixed_ref[rows, sl] = (sg_ref[rows, sl].astype(F32)
                                   * _rms(o, gro_ref[:, sl])).astype(BF16)

        def scores(j):
            sl = slice(j * LANES, (j + 1) * LANES)
            k = jnp.concatenate([k_ref[:, sl] for k_ref in ks], axis=0)
            q = jnp.concatenate([qa_ref[b, sl, :], qb_ref[b, sl, :]], axis=1)
            bias = jnp.concatenate([bias_scr[2 * j], bias_scr[2 * j + 1]], axis=1)
            s = jnp.dot(k, q, preferred_element_type=F32) + bias
            if start_mask is not None:
                s = s + jnp.concatenate([start_mask, start_mask], axis=1)
            return s

        pending = [scores(j) for j in range(SCORE_LOOKAHEAD)]
        for j in range(ATTN_HEADS // 2):
            retention(j)
            if j + SCORE_LOOKAHEAD < ATTN_HEADS // 2:
                pending.append(scores(j + SCORE_LOOKAHEAD))
            s = pending.pop(0)
            e = jnp.exp2(s - _column_max(s)).astype(BF16)
            halves = []
            for half in range(2):
                feat = slice((2 * j + half) * ATTN_DH, (2 * j + half + 1) * ATTN_DH)
                v_t = jnp.concatenate([v_ref[0, feat, :] for v_ref in vs], axis=1)
                out = jnp.dot(jnp.concatenate([v_t, ones], axis=0), e[:, half * t:(half + 1) * t],
                              preferred_element_type=F32)
                halves.append(out[:ATTN_DH] / out[ATTN_DH:ATTN_DH + 1])
            o_t = jnp.concatenate(halves, axis=0)
            mixed_ref[rows, RET_W + j * LANES:RET_W + (j + 1) * LANES] = o_t.T.astype(BF16)

    @pl.when(i == 0)
    def _start():
        blk = lax.broadcasted_iota(jnp.int32, (n_keys, t), 0) // t
        for b in range(PROMPT_BLOCKS_PER_STEP):
            query_block(b, jnp.where(blk + b >= 2, 0.0, MASKED))

    @pl.when(i > 0)
    def _steady():
        for b in range(PROMPT_BLOCKS_PER_STEP):
            query_block(b, None)

    @pl.when(i == pl.num_programs(0) - 1)
    def _fin():
        rfin_ref[...] = r_scr[...]


def _mix_prompt_call(qr, kr, vr, sg, qa_t, qb_t, ka, va_t, gro, vb):
    n = qr.shape[0]
    t, nb = TQ_PROMPT, PROMPT_BLOCKS_PER_STEP
    assert nb == 2
    last = n // t - 1
    key_blk = lambda i, j: jnp.clip(nb * i - 2 + j, 0, last)
    row = pl.BlockSpec((nb * t, RET_W), lambda i: (i, 0))
    col = pl.BlockSpec((nb, ATT_W, t), lambda i: (i, 0, 0))
    k_specs = [pl.BlockSpec((t, ATT_W), lambda i, j=j: (key_blk(i, j), 0)) for j in range(nb + 2)]
    v_specs = [pl.BlockSpec((1, ATT_W, t), lambda i, j=j: (key_blk(i, j), 0, 0))
               for j in range(nb + 2)]
    state = (RET_HEADS, RET_DK, RET_DV)
    return pl.pallas_call(
        _mix_prompt_kernel,
        grid=(n // (nb * t),),
        in_specs=[row] * 4 + [col] * 2 + k_specs + v_specs
                 + [_const_spec((1, RET_W)), _const_spec((ATTN_HEADS, BIAS_LANES))],
        out_specs=[pl.BlockSpec((nb * t, D_MODEL), lambda i: (i, 0)),
                   pl.BlockSpec(state, lambda i: (0, 0, 0))],
        out_shape=[jax.ShapeDtypeStruct((n, D_MODEL), BF16), jax.ShapeDtypeStruct(state, F32)],
        scratch_shapes=[pltpu.VMEM(state, F32),
                        pltpu.VMEM((ATTN_HEADS, 3 * t, t), F32),
                        pltpu.VMEM((RET_HEADS, t, t), F32),
                        pltpu.VMEM((t, RET_W), F32),
                        pltpu.VMEM((t, RET_W), F32)],
        compiler_params=pltpu.CompilerParams(dimension_semantics=("arbitrary",),
                                             vmem_limit_bytes=VMEM_LIMIT),
        name="mix_prompt",
    )(qr, kr, vr, sg, qa_t, qb_t, *([ka] * (nb + 2)), *([va_t] * (nb + 2)), gro, vb)


def _mix_sample_kernel(qr_ref, kr_ref, vr_ref, sg_ref, qa_ref, qb_ref, ka_ref, vt_ref, ck_ref, cv_ref,
                       r0_ref, gro_ref, vb_ref,
                       mixed_ref, rnew_ref,
                       bias_scr, dmat_scr, xi_scr, zeta_scr, *, t):
    streams, _, w = ck_ref.shape
    assert streams * t == LANES

    @pl.when(pl.program_id(0) == 0)
    def _init():
        _init_decay(dmat_scr, xi_scr, zeta_scr, t)
        _init_bias_queries(vb_ref, bias_scr, t, w)

    ones = jnp.ones((DENOM_ROWS, w), BF16)
    own_head = ((lax.broadcasted_iota(jnp.int32, (LANES, 2 * t), 0) < ATTN_DH)
                == (lax.broadcasted_iota(jnp.int32, (LANES, 2 * t), 1) < t))

    def retention(b, h):
        rows = slice(b * t, (b + 1) * t)
        sl = slice(h * RET_DK, (h + 1) * RET_DK)
        o, r_new = _retention(qr_ref[rows, sl], kr_ref[rows, sl], vr_ref[rows, sl], r0_ref[b, h],
                              dmat_scr[h], xi_scr[:, sl], zeta_scr[:, sl],
                              math.exp(t * LOG_DECAY[h]))
        rnew_ref[b, h] = r_new
        mixed_ref[rows, sl] = (sg_ref[rows, sl].astype(F32) * _rms(o, gro_ref[:, sl])).astype(BF16)

    def scores(b, j):
        rows = slice(b * t, (b + 1) * t)
        sl = slice(j * LANES, (j + 1) * LANES)
        q2 = jnp.concatenate([qa_ref[rows, sl], qb_ref[rows, sl]], axis=0)
        s_c = (jnp.dot(q2, ck_ref[b, sl, :].astype(BF16), preferred_element_type=F32)
               + bias_scr[j, :, :w])
        s_n = (lax.dot_general(q2, ka_ref[:, sl], _NT, preferred_element_type=F32)
               + bias_scr[j, :, w + b * LANES:w + (b + 1) * LANES])
        return s_c, s_n

    def finish(b, j, s_c, s_n):
        rows = slice(b * t, (b + 1) * t)
        sl = slice(j * LANES, (j + 1) * LANES)
        m = jnp.maximum(jnp.max(s_c, axis=-1, keepdims=True),
                        jnp.max(s_n, axis=-1, keepdims=True))
        e_c = jnp.exp2(s_c - m).astype(BF16)
        e_n = jnp.exp2(s_n - m).astype(BF16)
        lhs_c = jnp.concatenate([cv_ref[b, sl, :].astype(BF16), ones], axis=0)
        lhs_n = jnp.concatenate([vt_ref[0, sl, :], ones[:, :LANES]], axis=0)
        out = (lax.dot_general(lhs_c, e_c, _NT, preferred_element_type=F32)
               + lax.dot_general(lhs_n, e_n, _NT, preferred_element_type=F32))
        x = out[:LANES] / out[LANES:LANES + 1]
        x_t = jnp.where(own_head, x, 0.0).T
        mixed_ref[rows, RET_W + j * LANES:RET_W + (j + 1) * LANES] = (
            x_t[:t] + x_t[t:]).astype(BF16)

    items = [(b, j) for b in range(streams) for j in range(ATTN_HEADS // 2)]
    ret_items = [(b, h) for b in range(streams) for h in range(RET_HEADS)]
    pending = [scores(*it) for it in items[:SAMPLE_LOOKAHEAD]]
    for n, (b, j) in enumerate(items):
        if n < len(ret_items):
            retention(*ret_items[n])
        if n + SAMPLE_LOOKAHEAD < len(items):
            pending.append(scores(*items[n + SAMPLE_LOOKAHEAD]))
        finish(b, j, *pending.pop(0))


def _mix_sample_call(qr, kr, vr, sg, qa, qb, ka, va_t, ck, cv, r0, gro, vb, t):
    n = qr.shape[0]
    nb, _, w = ck.shape
    g = LANES // t
    assert nb % g == 0
    row = pl.BlockSpec((g * t, RET_W), lambda b: (b, 0))
    col = pl.BlockSpec((1, ATT_W, g * t), lambda b: (b, 0, 0))
    cache = pl.BlockSpec((g, ATT_W, w), lambda b: (b, 0, 0))
    state = pl.BlockSpec((g, RET_HEADS, RET_DK, RET_DV), lambda b: (b, 0, 0, 0))
    return pl.pallas_call(
        functools.partial(_mix_sample_kernel, t=t),
        grid=(nb // g,),
        in_specs=[row] * 7 + [col, cache, cache, state,
                              _const_spec((1, RET_W)), _const_spec((ATTN_HEADS, BIAS_LANES))],
        out_specs=[pl.BlockSpec((g * t, D_MODEL), lambda b: (b, 0)), state],
        out_shape=[jax.ShapeDtypeStruct((n, D_MODEL), BF16),
                   jax.ShapeDtypeStruct((nb, RET_HEADS, RET_DK, RET_DV), F32)],
        scratch_shapes=[pltpu.VMEM((ATTN_HEADS // 2, 2 * t, w + g * LANES), F32),
                        pltpu.VMEM((RET_HEADS, t, t), F32),
                        pltpu.VMEM((t, RET_W), F32),
                        pltpu.VMEM((t, RET_W), F32)],
        compiler_params=pltpu.CompilerParams(dimension_semantics=("arbitrary",),
                                             vmem_limit_bytes=VMEM_LIMIT),
        name="mix_sample",
    )(qr, kr, vr, sg, qa, qb, ka, va_t, ck, cv, r0, gro, vb)


FF_SPLITS = (0, 1024, 2048, D_FF)


def _tail_kernel(x_ref, mixed_ref, p_ref, gffn_ref, gple_ref,
                 wout_hbm, wg_hbm, wu_hbm, wd_hbm, wpp_hbm, wpg_hbm,
                 out_ref,
                 wout_ref, wg_ref, wu_ref, wd_ref, wpp_ref, wpg_ref, sems):
    i = pl.program_id(0)
    tm = x_ref.shape[0]
    halves = (slice(0, tm // 2), slice(tm // 2, tm))
    ranges = list(zip(FF_SPLITS[:-1], FF_SPLITS[1:]))
    dot = functools.partial(jnp.dot, preferred_element_type=F32)
    copies = [pltpu.make_async_copy(src, dst, sems.at[n]) for n, (src, dst) in enumerate(
        ((wout_hbm, wout_ref), (wg_hbm, wg_ref), (wu_hbm, wu_ref), (wd_hbm, wd_ref),
         (wpp_hbm, wpp_ref), (wpg_hbm, wpg_ref)))]

    def body(first_step):
        def arrived(n):
            if first_step:
                copies[n].wait()

        arrived(0)
        h = [x_ref[r, :] + dot(mixed_ref[r, :], wout_ref[...]) for r in halves]
        u = [_rms(hr, gffn_ref[...]).astype(BF16) for hr in h]

        c0, c1 = ranges[0]
        arrived(1)
        arrived(2)
        act = jnp.concatenate([(jax.nn.silu(dot(ur, wg_ref[:, c0:c1]))
                                * dot(ur, wu_ref[:, c0:c1])).astype(BF16) for ur in u], axis=0)
        arrived(3)
        ffn = dot(act, wd_ref[c0:c1, :])
        u = jnp.concatenate(u, axis=0)
        for c0, c1 in ranges[1:]:
            act = (jax.nn.silu(dot(u, wg_ref[:, c0:c1])) * dot(u, wu_ref[:, c0:c1])).astype(BF16)
            if (c0, c1) != ranges[-1]:
                ffn = ffn + dot(act, wd_ref[c0:c1, :])

        c0, c1 = ranges[-1]
        h = [hr + ffn[r, :] + dot(act[r, :], wd_ref[c0:c1, :]) for hr, r in zip(h, halves)]
        arrived(4)
        ple = dot(p_ref[...].astype(BF16), wpp_ref[...])
        arrived(5)
        for hr, r in zip(h, halves):
            gate = jax.nn.sigmoid(dot(_rms(hr, gple_ref[...]).astype(BF16), wpg_ref[...]))
            out_ref[r, :] = hr + ple[r, :] * gate

    @pl.when(i == 0)
    def _first():
        for copy in copies:
            copy.start()
        body(True)

    @pl.when(i > 0)
    def _rest():
        body(False)


def _tail_call(x, mixed, p, w_out, g_ffn, w_gate, w_up, w_down, g_ple, w_pg, w_pp):
    n = x.shape[0]
    tm = TM_DENSE
    row = lambda w: pl.BlockSpec((tm, w), lambda i: (i, 0))
    weights = (w_out, w_gate, w_up, w_down, w_pp, w_pg)
    return pl.pallas_call(
        _tail_kernel,
        grid=(n // tm,),
        in_specs=[row(D_MODEL), row(D_MODEL), row(PLE_DIM),
                  _const_spec((1, D_MODEL)), _const_spec((1, D_MODEL))]
                 + [pl.BlockSpec(memory_space=pl.ANY)] * len(weights),
        out_specs=row(D_MODEL),
        out_shape=jax.ShapeDtypeStruct((n, D_MODEL), F32),
        scratch_shapes=[pltpu.VMEM(w.shape, BF16) for w in weights]
                       + [pltpu.SemaphoreType.DMA((len(weights),))],
        compiler_params=pltpu.CompilerParams(dimension_semantics=("arbitrary",),
                                             vmem_limit_bytes=VMEM_LIMIT),
        name="tail",
    )(x, mixed, p, g_ffn, g_ple, *weights)


def _bias_rows(rel_bias):
    edge = jnp.broadcast_to(rel_bias[:, 2 * REL_CLIP:], (ATTN_HEADS, BIAS_LANES - 2 * REL_CLIP))
    by_query = jnp.concatenate([edge, rel_bias[:, 2 * REL_CLIP:0:-1]], axis=1)
    by_key = jnp.concatenate([rel_bias[:, :2 * REL_CLIP], edge], axis=1)
    return by_query, by_key


def kernel(x_prompt, x_sample, cache_attn_k, cache_attn_v, state_ret, p_prompt, p_sample,
           g_mix, w_in, g_ret_out, g_q_attn, g_k_attn, rel_bias, w_out, g_ffn,
           w_ffn_gate, w_ffn_up, w_ffn_down, g_ple, w_ple_gate, w_ple_proj):
    depth = w_in.shape[0]
    assert depth == 1, "single-layer trunk"
    batch, seq, _ = x_prompt.shape
    dec_batch, dec_seq, _ = x_sample.shape
    cache_w = cache_attn_k.shape[2]
    assert batch == 1 and seq % TM_PROJ == 0 and dec_seq == CHUNK and cache_w == BAND_PAST
    assert BAND_PAST <= seq, "the returned prompt keys are the tail of the last projection tile"

    row = lambda g: g.reshape(1, -1)
    gq = row(jnp.tile(g_q_attn[0], ATTN_HEADS))
    gk = row(jnp.tile(g_k_attn[0], ATTN_HEADS))
    gro = row(g_ret_out[0])
    vb_query, vb_key = _bias_rows(rel_bias[0])
    half = RET_DK // 2
    inv = ROPE_THETA ** (-jnp.arange(half, dtype=F32) / half)
    inv = row(jnp.concatenate([inv, inv]))

    xp = x_prompt.reshape(seq, D_MODEL)
    (qr, kr, vr, sg, qa_t, qb_t, ka, va_t, k_tail, v_tail,
     w_out_b, w_gate_b, w_up_b, w_down_b, w_pg_b, w_pp_b) = _proj_call(
        xp, row(g_mix[0]), w_in[0], inv, gq, gk,
        pos0=0, pos_stride=TM_PROJ, period=TM_PROJ, prompt=True, feat_block=TQ_PROMPT,
        cast_weights=(w_out[0], w_ffn_gate[0], w_ffn_up[0], w_ffn_down[0],
                      w_ple_gate[0], w_ple_proj[0]))
    tail_w_args = (w_out_b, row(g_ffn[0]), w_gate_b, w_up_b, w_down_b, row(g_ple[0]),
                   w_pg_b, w_pp_b)
    mixed_p, r_fin = _mix_prompt_call(qr, kr, vr, sg, qa_t, qb_t, ka, va_t, gro, vb_key)
    y_prompt = _tail_call(xp, mixed_p, p_prompt[0].reshape(seq, PLE_DIM), *tail_w_args)

    n_s = dec_batch * dec_seq
    xs = x_sample.reshape(n_s, D_MODEL)
    qr, kr, vr, sg, qa, qb, ka, va_t, k_new, v_new = _proj_call(
        xs, row(g_mix[0]), w_in[0], inv, gq, gk,
        pos0=PAST_LEN, pos_stride=0, period=dec_seq, prompt=False, feat_block=LANES)
    feat_major = lambda c: c.transpose(0, 2, 3, 1).reshape(dec_batch, ATT_W, cache_w)
    mixed_s, r_new = _mix_sample_call(
        qr, kr, vr, sg, qa, qb, ka, va_t,
        feat_major(cache_attn_k[0]), feat_major(cache_attn_v[0]),
        state_ret[0], gro, vb_query, dec_seq)
    y_sample = _tail_call(xs, mixed_s, p_sample[0].reshape(n_s, PLE_DIM), *tail_w_args)

    kv_p = lambda a: a.reshape(ATTN_HEADS, ATTN_DH, BAND_PAST).transpose(2, 0, 1).reshape(
        1, batch, BAND_PAST, ATTN_HEADS, ATTN_DH)
    kv_s = lambda a: a.reshape(1, dec_batch, dec_seq, ATTN_HEADS, ATTN_DH)
    return (y_prompt.reshape(batch, seq, D_MODEL),
            y_sample.reshape(dec_batch, dec_seq, D_MODEL),
            r_fin.reshape(1, batch, RET_HEADS, RET_DK, RET_DV),
            kv_p(k_tail), kv_p(v_tail),
            r_new.reshape(1, dec_batch, RET_HEADS, RET_DK, RET_DV),
            kv_s(k_new), kv_s(v_new))
```

```python
import functools
import math

import jax
import jax.numpy as jnp
from jax import lax
from jax.experimental import pallas as pl
from jax.experimental.pallas import tpu as pltpu

F32 = jnp.float32
BF16 = jnp.bfloat16

D_MODEL = 1024
CHUNK = 64
PAST_LEN = 2048
RET_HEADS = 4
RET_DK = 128
RET_DV = 128
ATTN_HEADS = 8
ATTN_DH = 64
BAND_PAST_CHUNKS = 8
BAND_PAST = BAND_PAST_CHUNKS * CHUNK
REL_CLIP = 256
D_FF = 2816
PLE_DIM = 256
ROPE_THETA = 10000.0
EPS = 1e-6
RET_W = RET_HEADS * RET_DK
ATT_W = ATTN_HEADS * ATTN_DH
IN_COLS = 4 * RET_W + 3 * ATT_W
LOG_DECAY = tuple(math.log1p(-(2.0 ** (-5 - h))) for h in range(RET_HEADS))
LOG2E = math.log2(math.e)
MASKED = -1e30

LANES = 128
SUBLANES = 8
BF16_ROWS = 16
BIAS_LANES = 1024
TM_PROJ = 512
PROJ_GROUP_ORDER = (4, 5, 6, 0, 1, 3, 2)
TM_DENSE = 512
TQ_PROMPT = 256
PROMPT_BLOCKS_PER_STEP = 2
DENOM_ROWS = BF16_ROWS
SAMPLE_LOOKAHEAD = 3
SCORE_LOOKAHEAD = 1
VMEM_LIMIT = 56 * 1024 * 1024

_NT = (((1,), (1,)), ((), ()))
_TN = (((0,), (0,)), ((), ()))


def _rms(x, g):
    return x * lax.rsqrt(jnp.mean(x * x, axis=-1, keepdims=True) + EPS) * g


def _const_spec(shape):
    zeros = (0,) * len(shape)
    return pl.BlockSpec(shape, lambda i: zeros, pipeline_mode=pl.Buffered(1))


def _proj_kernel(*refs, n_cast, pos0, pos_stride, period, prompt):
    x_ref, gmix_ref, inv_ref, gq_ref, gk_ref = refs[:5]
    cast_in = refs[5:5 + n_cast]
    w_hbm = refs[5 + n_cast]
    (qr_ref, kr_ref, vr_ref, sg_ref, qa_ref, qb_ref, ka_ref, vt_ref,
     ka32_ref, va32_ref) = refs[6 + n_cast:16 + n_cast]
    cast_out = refs[16 + n_cast:16 + 2 * n_cast]
    cos_scr, sin_scr, w_ref, stage_ref, sems = refs[16 + 2 * n_cast:]

    i = pl.program_id(0)
    tm = x_ref.shape[0]
    inv = inv_ref[...]
    group_cols = lambda g: slice(g * RET_W, (g + 1) * RET_W)
    copies = {g: pltpu.make_async_copy(w_hbm.at[:, group_cols(g)], stage_ref.at[n], sems.at[n])
              for n, g in enumerate(PROJ_GROUP_ORDER)}

    def body(first_step):
        base = jnp.full((1, LANES), pos0 + i * pos_stride, jnp.int32).astype(F32) * inv
        cb, sb = jnp.cos(base), jnp.sin(base)
        cos = cb * cos_scr[...] - sb * sin_scr[...]
        sin = sb * cos_scr[...] + cb * sin_scr[...]
        lane = lax.broadcasted_iota(jnp.int32, (tm, LANES), 1)
        first_half = lane < LANES // 2
        sin = jnp.where(first_half, -sin, sin)

        xn = _rms(x_ref[...], gmix_ref[...]).astype(BF16)
        for src_ref, dst_ref in zip(cast_in, cast_out):
            dst_ref[...] = src_ref[...].astype(BF16)

        def proj(group):
            cols = group_cols(group)
            if first_step:
                copies[group].wait()
                w_ref[:, cols] = stage_ref[PROJ_GROUP_ORDER.index(group)].astype(BF16)
            return jnp.dot(xn, w_ref[:, cols], preferred_element_type=F32)

        def rope_store(z, out_ref, scale):
            for h in range(RET_HEADS):
                sl = slice(h * RET_DK, (h + 1) * RET_DK)
                zh = z[:, sl]
                r = zh * cos + pltpu.roll(zh, RET_DK // 2, 1) * sin
                if scale is not None:
                    r = r * scale
                out_ref[:, sl] = r.astype(out_ref.dtype)

        def head_rms(z, g_ref):
            outs = []
            for j in range(ATT_W // LANES):
                sl = slice(j * LANES, (j + 1) * LANES)
                zj = z[:, sl]
                zz = zj * zj
                s0 = jnp.sum(jnp.where(first_half, zz, 0.0), axis=-1, keepdims=True)
                s1 = jnp.sum(jnp.where(first_half, 0.0, zz), axis=-1, keepdims=True)
                ms = jnp.where(first_half, s0, s1) * (1.0 / ATTN_DH)
                outs.append(zj * lax.rsqrt(ms + EPS) * g_ref[:, sl])
            return jnp.concatenate(outs, axis=1)

        def store_blocks(ref, val_t):
            width = ref.shape[2]
            for b in range(ref.shape[0]):
                ref[b] = val_t[:, b * width:(b + 1) * width]

        qs = head_rms(proj(4), gq_ref) * (ATTN_DH ** -0.5 * LOG2E)
        if prompt:
            qs = qs.T
            pair_first = lax.broadcasted_iota(jnp.int32, qs.shape, 0) % LANES < ATTN_DH
            store_blocks(qa_ref, jnp.where(pair_first, qs, 0.0).astype(BF16))
            store_blocks(qb_ref, jnp.where(pair_first, 0.0, qs).astype(BF16))
        else:
            pair_first = jnp.concatenate([first_half] * (ATT_W // LANES), axis=1)
            qa_ref[...] = jnp.where(pair_first, qs, 0.0).astype(BF16)
            qb_ref[...] = jnp.where(pair_first, 0.0, qs).astype(BF16)
        ka = head_rms(proj(5), gk_ref)
        ka_ref[...] = ka.astype(BF16)
        va = proj(6)
        va_t = va.T
        store_blocks(vt_ref, va_t.astype(BF16))
        rope_store(proj(0), qr_ref, None)
        rope_store(proj(1), kr_ref, RET_DK ** -0.5)
        sg_ref[...] = jax.nn.silu(proj(3)).astype(BF16)
        vr_ref[...] = proj(2).astype(BF16)

        if prompt:
            @pl.when(i == pl.num_programs(0) - 1)
            def _tail():
                ka32_ref[...] = ka[tm - BAND_PAST:].T
                va32_ref[...] = va_t[:, tm - BAND_PAST:]
        else:
            ka32_ref[...] = ka
            va32_ref[...] = va

    @pl.when(i == 0)
    def _first():
        for g in PROJ_GROUP_ORDER:
            copies[g].start()
        r = lax.broadcasted_iota(jnp.int32, (tm, LANES), 0) % period
        ang = r.astype(F32) * inv
        cos_scr[...] = jnp.cos(ang)
        sin_scr[...] = jnp.sin(ang)
        body(True)

    @pl.when(i > 0)
    def _rest():
        body(False)


def _cast_block_rows(rows, steps):
    block = BF16_ROWS
    while rows % block or block * steps < rows:
        block += BF16_ROWS
    return block


def _proj_call(x, g_mix, w_in, inv, gq, gk, *, pos0, pos_stride, period, prompt, feat_block,
               cast_weights=()):
    n = x.shape[0]
    tm = TM_PROJ
    cast_specs, cast_shapes = [], []
    for w in cast_weights:
        rows, cols = w.shape
        block = _cast_block_rows(rows, n // tm)
        last = rows // block - 1
        cast_specs.append(pl.BlockSpec((block, cols), lambda i, last=last: (jnp.minimum(i, last), 0)))
        cast_shapes.append(jax.ShapeDtypeStruct(w.shape, BF16))
    row = lambda w: pl.BlockSpec((tm, w), lambda i: (i, 0))
    col_spec = pl.BlockSpec((tm // feat_block, ATT_W, feat_block), lambda i: (i, 0, 0))
    bf = jax.ShapeDtypeStruct((n, RET_W), BF16)
    bf_t = jax.ShapeDtypeStruct((n // feat_block, ATT_W, feat_block), BF16)
    if prompt:
        assert tm >= BAND_PAST
        q_shape, q_spec = bf_t, col_spec
        kv32 = jax.ShapeDtypeStruct((ATT_W, BAND_PAST), F32)
        kv32_spec = pl.BlockSpec((ATT_W, BAND_PAST), lambda i: (0, 0))
    else:
        q_shape, q_spec = bf, row(ATT_W)
        kv32 = jax.ShapeDtypeStruct((n, ATT_W), F32)
        kv32_spec = row(ATT_W)
    body = functools.partial(_proj_kernel, n_cast=len(cast_weights), pos0=pos0,
                             pos_stride=pos_stride, period=period, prompt=prompt)
    return pl.pallas_call(
        body,
        grid=(n // tm,),
        in_specs=[row(D_MODEL), _const_spec((1, D_MODEL)),
                  _const_spec((1, LANES)), _const_spec((1, ATT_W)), _const_spec((1, ATT_W))]
                 + cast_specs + [pl.BlockSpec(memory_space=pl.ANY)],
        out_specs=[row(RET_W)] * 4 + [q_spec] * 2 + [row(ATT_W), col_spec] + [kv32_spec] * 2
                  + cast_specs,
        out_shape=[bf] * 4 + [q_shape] * 2 + [bf, bf_t] + [kv32] * 2 + cast_shapes,
        scratch_shapes=[pltpu.VMEM((tm, LANES), F32), pltpu.VMEM((tm, LANES), F32),
                        pltpu.VMEM((D_MODEL, IN_COLS), BF16),
                        pltpu.VMEM((len(PROJ_GROUP_ORDER), D_MODEL, RET_W), F32),
                        pltpu.SemaphoreType.DMA((len(PROJ_GROUP_ORDER),))],
        compiler_params=pltpu.CompilerParams(dimension_semantics=("arbitrary",),
                                             vmem_limit_bytes=VMEM_LIMIT),
        name="proj",
    )(x, g_mix, inv, gq, gk, *cast_weights, w_in)


def _init_decay(dmat_scr, xi_scr, zeta_scr, t):
    n = lax.broadcasted_iota(jnp.int32, (t, t), 0)
    m = lax.broadcasted_iota(jnp.int32, (t, t), 1)
    diff = (n - m).astype(F32)
    pos = lax.broadcasted_iota(jnp.int32, (t, RET_DK), 0).astype(F32)
    for h in range(RET_HEADS):
        lg = LOG_DECAY[h]
        dmat_scr[h] = jnp.where(diff >= 0, jnp.exp(jnp.maximum(diff, 0.0) * lg), 0.0)
        sl = slice(h * RET_DK, (h + 1) * RET_DK)
        xi_scr[:, sl] = jnp.exp((pos + 1.0) * lg)
        zeta_scr[:, sl] = jnp.exp((t - 1.0 - pos) * lg)


def _toeplitz_rows(vb_ref, first_shift):
    r = lax.broadcasted_iota(jnp.int32, (ATTN_HEADS * SUBLANES, BIAS_LANES), 0)
    base = jnp.concatenate([jnp.broadcast_to(vb_ref[h:h + 1, :], (SUBLANES, BIAS_LANES))
                            for h in range(ATTN_HEADS)], axis=0) * LOG2E
    base = pltpu.roll(base, first_shift, 1)
    for shift in (1, 2, 4):
        base = jnp.where((r & shift) != 0, pltpu.roll(base, shift, 1), base)
    return [base[h * SUBLANES:(h + 1) * SUBLANES] for h in range(ATTN_HEADS)]


def _init_bias_keys(vb_ref, bias_scr):
    _, n_keys, t = bias_scr.shape
    lane = lax.broadcasted_iota(jnp.int32, (SUBLANES, t), 1)
    for h, base in enumerate(_toeplitz_rows(vb_ref, BIAS_LANES - n_keys)):
        for g in range(n_keys // SUBLANES):
            rows = (pltpu.roll(base, SUBLANES * g, 1) if g else base)[:, :t]
            kc = SUBLANES * g // CHUNK
            lo = max(kc - BAND_PAST_CHUNKS, 0) * CHUNK
            hi = (min(kc, t // CHUNK - 1) + 1) * CHUNK
            rows = jnp.where(jnp.logical_and(lane >= lo, lane < hi), rows, MASKED)
            bias_scr[h, g * SUBLANES:(g + 1) * SUBLANES, :] = rows


def _init_bias_queries(vb_ref, bias_scr, t, w):
    first = lax.broadcasted_iota(jnp.int32, (SUBLANES, LANES), 1) < t
    for h, base in enumerate(_toeplitz_rows(vb_ref, BIAS_LANES - REL_CLIP)):
        for g in range(t // SUBLANES):
            rows = pltpu.roll(base, SUBLANES * g, 1) if g else base
            shifted = pltpu.roll(base, SUBLANES * g + t, 1)
            r0 = (h % 2) * t + g * SUBLANES
            bias_scr[h // 2, r0:r0 + SUBLANES, :w] = rows[:, :w]
            bias_scr[h // 2, r0:r0 + SUBLANES, w:w + LANES] = jnp.where(
                first, rows[:, w:w + LANES], MASKED)
            bias_scr[h // 2, r0:r0 + SUBLANES, w + LANES:] = jnp.where(
                first, MASKED, shifted[:, w:w + LANES])


def _retention(q, k, v, r, dmat, xi, zeta, decay_t):
    s = lax.dot_general(q, k, _NT, preferred_element_type=F32) * dmat
    inner = jnp.dot(s.astype(BF16), v, preferred_element_type=F32)
    cross = jnp.dot(q, r.astype(BF16), preferred_element_type=F32) * xi
    kz = (k.astype(F32) * zeta).astype(BF16)
    r_new = r * decay_t + lax.dot_general(kz, v, _TN, preferred_element_type=F32)
    return inner + cross, r_new


def _column_max(s):
    rows = s.shape[0]
    while rows % (2 * SUBLANES) == 0:
        rows //= 2
        s = jnp.maximum(s[:rows], s[rows:])
    return jnp.max(s, axis=0, keepdims=True)


def _mix_prompt_kernel(qr_ref, kr_ref, vr_ref, sg_ref, qa_ref, qb_ref,
                       k0_ref, k1_ref, k2_ref, k3_ref, v0_ref, v1_ref, v2_ref, v3_ref,
                       gro_ref, vb_ref,
                       mixed_ref, rfin_ref,
                       r_scr, bias_scr, dmat_scr, xi_scr, zeta_scr):
    i = pl.program_id(0)
    t = TQ_PROMPT
    n_keys = bias_scr.shape[1]
    k_refs = (k0_ref, k1_ref, k2_ref, k3_ref)
    v_refs = (v0_ref, v1_ref, v2_ref, v3_ref)

    @pl.when(i == 0)
    def _init():
        r_scr[...] = jnp.zeros_like(r_scr)
        _init_decay(dmat_scr, xi_scr, zeta_scr, t)
        _init_bias_keys(vb_ref, bias_scr)

    ones = jnp.ones((DENOM_ROWS, n_keys), BF16)

    def query_block(b, start_mask):
        rows = slice(b * t, (b + 1) * t)
        ks, vs = k_refs[b:b + 3], v_refs[b:b + 3]

        def retention(h):
            sl = slice(h * RET_DK, (h + 1) * RET_DK)
            o, r_new = _retention(qr_ref[rows, sl], kr_ref[rows, sl], vr_ref[rows, sl], r_scr[h],
                                  dmat_scr[h], xi_scr[:, sl], zeta_scr[:, sl],
                                  math.exp(t * LOG_DECAY[h]))
            r_scr[h] = r_new
            mixed_ref[rows, sl] = (sg_ref[rows, sl].astype(F32)
                                   * _rms(o, gro_ref[:, sl])).astype(BF16)

        def scores(j):
            sl = slice(j * LANES, (j + 1) * LANES)
            k = jnp.concatenate([k_ref[:, sl] for k_ref in ks], axis=0)
            q = jnp.concatenate([qa_ref[b, sl, :], qb_ref[b, sl, :]], axis=1)
            bias = jnp.concatenate([bias_scr[2 * j], bias_scr[2 * j + 1]], axis=1)
            s = jnp.dot(k, q, preferred_element_type=F32) + bias
            if start_mask is not None:
                s = s + jnp.concatenate([start_mask, start_mask], axis=1)
            return s

        pending = [scores(j) for j in range(SCORE_LOOKAHEAD)]
        for j in range(ATTN_HEADS // 2):
            retention(j)
            if j + SCORE_LOOKAHEAD < ATTN_HEADS // 2:
                pending.append(scores(j + SCORE_LOOKAHEAD))
            s = pending.pop(0)
            e = jnp.exp2(s - _column_max(s)).astype(BF16)
            halves = []
            for half in range(2):
                feat = slice((2 * j + half) * ATTN_DH, (2 * j + half + 1) * ATTN_DH)
                v_t = jnp.concatenate([v_ref[0, feat, :] for v_ref in vs], axis=1)
                out = jnp.dot(jnp.concatenate([v_t, ones], axis=0), e[:, half * t:(half + 1) * t],
                              preferred_element_type=F32)
                halves.append(out[:ATTN_DH] / out[ATTN_DH:ATTN_DH + 1])
            o_t = jnp.concatenate(halves, axis=0)
            mixed_ref[rows, RET_W + j * LANES:RET_W + (j + 1) * LANES] = o_t.T.astype(BF16)

    @pl.when(i == 0)
    def _start():
        blk = lax.broadcasted_iota(jnp.int32, (n_keys, t), 0) // t
        for b in range(PROMPT_BLOCKS_PER_STEP):
            query_block(b, jnp.where(blk + b >= 2, 0.0, MASKED))

    @pl.when(i > 0)
    def _steady():
        for b in range(PROMPT_BLOCKS_PER_STEP):
            query_block(b, None)

    @pl.when(i == pl.num_programs(0) - 1)
    def _fin():
        rfin_ref[...] = r_scr[...]


def _mix_prompt_call(qr, kr, vr, sg, qa_t, qb_t, ka, va_t, gro, vb):
    n = qr.shape[0]
    t, nb = TQ_PROMPT, PROMPT_BLOCKS_PER_STEP
    assert nb == 2
    last = n // t - 1
    key_blk = lambda i, j: jnp.clip(nb * i - 2 + j, 0, last)
    row = pl.BlockSpec((nb * t, RET_W), lambda i: (i, 0))
    col = pl.BlockSpec((nb, ATT_W, t), lambda i: (i, 0, 0))
    k_specs = [pl.BlockSpec((t, ATT_W), lambda i, j=j: (key_blk(i, j), 0)) for j in range(nb + 2)]
    v_specs = [pl.BlockSpec((1, ATT_W, t), lambda i, j=j: (key_blk(i, j), 0, 0))
               for j in range(nb + 2)]
    state = (RET_HEADS, RET_DK, RET_DV)
    return pl.pallas_call(
        _mix_prompt_kernel,
        grid=(n // (nb * t),),
        in_specs=[row] * 4 + [col] * 2 + k_specs + v_specs
                 + [_const_spec((1, RET_W)), _const_spec((ATTN_HEADS, BIAS_LANES))],
        out_specs=[pl.BlockSpec((nb * t, D_MODEL), lambda i: (i, 0)),
                   pl.BlockSpec(state, lambda i: (0, 0, 0))],
        out_shape=[jax.ShapeDtypeStruct((n, D_MODEL), BF16), jax.ShapeDtypeStruct(state, F32)],
        scratch_shapes=[pltpu.VMEM(state, F32),
                        pltpu.VMEM((ATTN_HEADS, 3 * t, t), F32),
                        pltpu.VMEM((RET_HEADS, t, t), F32),
                        pltpu.VMEM((t, RET_W), F32),
                        pltpu.VMEM((t, RET_W), F32)],
        compiler_params=pltpu.CompilerParams(dimension_semantics=("arbitrary",),
                                             vmem_limit_bytes=VMEM_LIMIT),
        name="mix_prompt",
    )(qr, kr, vr, sg, qa_t, qb_t, *([ka] * (nb + 2)), *([va_t] * (nb + 2)), gro, vb)


def _mix_sample_kernel(qr_ref, kr_ref, vr_ref, sg_ref, qa_ref, qb_ref, ka_ref, vt_ref, ck_ref, cv_ref,
                       r0_ref, gro_ref, vb_ref,
                       mixed_ref, rnew_ref,
                       bias_scr, dmat_scr, xi_scr, zeta_scr, *, t):
    streams, _, w = ck_ref.shape
    assert streams * t == LANES

    @pl.when(pl.program_id(0) == 0)
    def _init():
        _init_decay(dmat_scr, xi_scr, zeta_scr, t)
        _init_bias_queries(vb_ref, bias_scr, t, w)

    ones = jnp.ones((DENOM_ROWS, w), BF16)
    own_head = ((lax.broadcasted_iota(jnp.int32, (LANES, 2 * t), 0) < ATTN_DH)
                == (lax.broadcasted_iota(jnp.int32, (LANES, 2 * t), 1) < t))

    def retention(b, h):
        rows = slice(b * t, (b + 1) * t)
        sl = slice(h * RET_DK, (h + 1) * RET_DK)
        o, r_new = _retention(qr_ref[rows, sl], kr_ref[rows, sl], vr_ref[rows, sl], r0_ref[b, h],
                              dmat_scr[h], xi_scr[:, sl], zeta_scr[:, sl],
                              math.exp(t * LOG_DECAY[h]))
        rnew_ref[b, h] = r_new
        mixed_ref[rows, sl] = (sg_ref[rows, sl].astype(F32) * _rms(o, gro_ref[:, sl])).astype(BF16)

    def scores(b, j):
        rows = slice(b * t, (b + 1) * t)
        sl = slice(j * LANES, (j + 1) * LANES)
        q2 = jnp.concatenate([qa_ref[rows, sl], qb_ref[rows, sl]], axis=0)
        s_c = (jnp.dot(q2, ck_ref[b, sl, :].astype(BF16), preferred_element_type=F32)
               + bias_scr[j, :, :w])
        s_n = (lax.dot_general(q2, ka_ref[:, sl], _NT, preferred_element_type=F32)
               + bias_scr[j, :, w + b * LANES:w + (b + 1) * LANES])
        return s_c, s_n

    def finish(b, j, s_c, s_n):
        rows = slice(b * t, (b + 1) * t)
        sl = slice(j * LANES, (j + 1) * LANES)
        m = jnp.maximum(jnp.max(s_c, axis=-1, keepdims=True),
                        jnp.max(s_n, axis=-1, keepdims=True))
        e_c = jnp.exp2(s_c - m).astype(BF16)
        e_n = jnp.exp2(s_n - m).astype(BF16)
        lhs_c = jnp.concatenate([cv_ref[b, sl, :].astype(BF16), ones], axis=0)
        lhs_n = jnp.concatenate([vt_ref[0, sl, :], ones[:, :LANES]], axis=0)
        out = (lax.dot_general(lhs_c, e_c, _NT, preferred_element_type=F32)
               + lax.dot_general(lhs_n, e_n, _NT, preferred_element_type=F32))
        x = out[:LANES] / out[LANES:LANES + 1]
        x_t = jnp.where(own_head, x, 0.0).T
        mixed_ref[rows, RET_W + j * LANES:RET_W + (j + 1) * LANES] = (
            x_t[:t] + x_t[t:]).astype(BF16)

    items = [(b, j) for b in range(streams) for j in range(ATTN_HEADS // 2)]
    ret_items = [(b, h) for b in range(streams) for h in range(RET_HEADS)]
    pending = [scores(*it) for it in items[:SAMPLE_LOOKAHEAD]]
    for n, (b, j) in enumerate(items):
        if n < len(ret_items):
            retention(*ret_items[n])
        if n + SAMPLE_LOOKAHEAD < len(items):
            pending.append(scores(*items[n + SAMPLE_LOOKAHEAD]))
        finish(b, j, *pending.pop(0))


def _mix_sample_call(qr, kr, vr, sg, qa, qb, ka, va_t, ck, cv, r0, gro, vb, t):
    n = qr.shape[0]
    nb, _, w = ck.shape
    g = LANES // t
    assert nb % g == 0
    row = pl.BlockSpec((g * t, RET_W), lambda b: (b, 0))
    col = pl.BlockSpec((1, ATT_W, g * t), lambda b: (b, 0, 0))
    cache = pl.BlockSpec((g, ATT_W, w), lambda b: (b, 0, 0))
    state = pl.BlockSpec((g, RET_HEADS, RET_DK, RET_DV), lambda b: (b, 0, 0, 0))
    return pl.pallas_call(
        functools.partial(_mix_sample_kernel, t=t),
        grid=(nb // g,),
        in_specs=[row] * 7 + [col, cache, cache, state,
                              _const_spec((1, RET_W)), _const_spec((ATTN_HEADS, BIAS_LANES))],
        out_specs=[pl.BlockSpec((g * t, D_MODEL), lambda b: (b, 0)), state],
        out_shape=[jax.ShapeDtypeStruct((n, D_MODEL), BF16),
                   jax.ShapeDtypeStruct((nb, RET_HEADS, RET_DK, RET_DV), F32)],
        scratch_shapes=[pltpu.VMEM((ATTN_HEADS // 2, 2 * t, w + g * LANES), F32),
                        pltpu.VMEM((RET_HEADS, t, t), F32),
                        pltpu.VMEM((t, RET_W), F32),
                        pltpu.VMEM((t, RET_W), F32)],
        compiler_params=pltpu.CompilerParams(dimension_semantics=("arbitrary",),
                                             vmem_limit_bytes=VMEM_LIMIT),
        name="mix_sample",
    )(qr, kr, vr, sg, qa, qb, ka, va_t, ck, cv, r0, gro, vb)


FF_SPLITS = (0, 1024, 2048, D_FF)


def _tail_kernel(x_ref, mixed_ref, p_ref, gffn_ref, gple_ref,
                 wout_hbm, wg_hbm, wu_hbm, wd_hbm, wpp_hbm, wpg_hbm,
                 out_ref,
                 wout_ref, wg_ref, wu_ref, wd_ref, wpp_ref, wpg_ref, sems):
    i = pl.program_id(0)
    tm = x_ref.shape[0]
    halves = (slice(0, tm // 2), slice(tm // 2, tm))
    ranges = list(zip(FF_SPLITS[:-1], FF_SPLITS[1:]))
    dot = functools.partial(jnp.dot, preferred_element_type=F32)
    pieces = [(wout_hbm, wout_ref)]
    for c0, c1 in ranges:
        pieces += [(wg_hbm.at[:, c0:c1], wg_ref.at[:, c0:c1]),
                   (wu_hbm.at[:, c0:c1], wu_ref.at[:, c0:c1]),
                   (wd_hbm.at[c0:c1, :], wd_ref.at[c0:c1, :])]
    pieces += [(wpp_hbm, wpp_ref), (wpg_hbm, wpg_ref)]
    copies = [pltpu.make_async_copy(src, dst, sems.at[n]) for n, (src, dst) in enumerate(pieces)]

    def body(first_step):
        def arrived(n):
            if first_step:
                copies[n].wait()

        arrived(0)
        h = [x_ref[r, :] + dot(mixed_ref[r, :], wout_ref[...]) for r in halves]
        u = [_rms(hr, gffn_ref[...]).astype(BF16) for hr in h]

        c0, c1 = ranges[0]
        arrived(1)
        gate = jnp.concatenate([dot(ur, wg_ref[:, c0:c1]) for ur in u], axis=0)
        u = jnp.concatenate(u, axis=0)
        arrived(2)
        act = (jax.nn.silu(gate) * dot(u, wu_ref[:, c0:c1])).astype(BF16)
        arrived(3)
        ffn = dot(act, wd_ref[c0:c1, :])
        for k, (c0, c1) in enumerate(ranges[1:], start=1):
            arrived(1 + 3 * k)
            arrived(2 + 3 * k)
            act = (jax.nn.silu(dot(u, wg_ref[:, c0:c1])) * dot(u, wu_ref[:, c0:c1])).astype(BF16)
            arrived(3 + 3 * k)
            if (c0, c1) != ranges[-1]:
                ffn = ffn + dot(act, wd_ref[c0:c1, :])

        c0, c1 = ranges[-1]
        h = [hr + ffn[r, :] + dot(act[r, :], wd_ref[c0:c1, :]) for hr, r in zip(h, halves)]
        arrived(1 + 3 * len(ranges))
        ple = dot(p_ref[...].astype(BF16), wpp_ref[...])
        arrived(2 + 3 * len(ranges))
        for hr, r in zip(h, halves):
            gate = jax.nn.sigmoid(dot(_rms(hr, gple_ref[...]).astype(BF16), wpg_ref[...]))
            out_ref[r, :] = hr + ple[r, :] * gate

    @pl.when(i == 0)
    def _first():
        for copy in copies:
            copy.start()
        body(True)

    @pl.when(i > 0)
    def _rest():
        body(False)


def _tail_call(x, mixed, p, w_out, g_ffn, w_gate, w_up, w_down, g_ple, w_pg, w_pp):
    n = x.shape[0]
    tm = TM_DENSE
    row = lambda w: pl.BlockSpec((tm, w), lambda i: (i, 0))
    weights = (w_out, w_gate, w_up, w_down, w_pp, w_pg)
    return pl.pallas_call(
        _tail_kernel,
        grid=(n // tm,),
        in_specs=[row(D_MODEL), row(D_MODEL), row(PLE_DIM),
                  _const_spec((1, D_MODEL)), _const_spec((1, D_MODEL))]
                 + [pl.BlockSpec(memory_space=pl.ANY)] * len(weights),
        out_specs=row(D_MODEL),
        out_shape=jax.ShapeDtypeStruct((n, D_MODEL), F32),
        scratch_shapes=[pltpu.VMEM(w.shape, BF16) for w in weights]
                       + [pltpu.SemaphoreType.DMA((3 * len(FF_SPLITS),))],
        compiler_params=pltpu.CompilerParams(dimension_semantics=("arbitrary",),
                                             vmem_limit_bytes=VMEM_LIMIT),
        name="tail",
    )(x, mixed, p, g_ffn, g_ple, *weights)


def _bias_rows(rel_bias):
    edge = jnp.broadcast_to(rel_bias[:, 2 * REL_CLIP:], (ATTN_HEADS, BIAS_LANES - 2 * REL_CLIP))
    by_query = jnp.concatenate([edge, rel_bias[:, 2 * REL_CLIP:0:-1]], axis=1)
    by_key = jnp.concatenate([rel_bias[:, :2 * REL_CLIP], edge], axis=1)
    return by_query, by_key


def kernel(x_prompt, x_sample, cache_attn_k, cache_attn_v, state_ret, p_prompt, p_sample,
           g_mix, w_in, g_ret_out, g_q_attn, g_k_attn, rel_bias, w_out, g_ffn,
           w_ffn_gate, w_ffn_up, w_ffn_down, g_ple, w_ple_gate, w_ple_proj):
    depth = w_in.shape[0]
    assert depth == 1, "single-layer trunk"
    batch, seq, _ = x_prompt.shape
    dec_batch, dec_seq, _ = x_sample.shape
    cache_w = cache_attn_k.shape[2]
    assert batch == 1 and seq % TM_PROJ == 0 and dec_seq == CHUNK and cache_w == BAND_PAST
    assert BAND_PAST <= seq, "the returned prompt keys are the tail of the last projection tile"

    row = lambda g: g.reshape(1, -1)
    gq = row(jnp.tile(g_q_attn[0], ATTN_HEADS))
    gk = row(jnp.tile(g_k_attn[0], ATTN_HEADS))
    gro = row(g_ret_out[0])
    vb_query, vb_key = _bias_rows(rel_bias[0])
    half = RET_DK // 2
    inv = ROPE_THETA ** (-jnp.arange(half, dtype=F32) / half)
    inv = row(jnp.concatenate([inv, inv]))

    xp = x_prompt.reshape(seq, D_MODEL)
    (qr, kr, vr, sg, qa_t, qb_t, ka, va_t, k_tail, v_tail,
     w_out_b, w_gate_b, w_up_b, w_down_b, w_pg_b, w_pp_b) = _proj_call(
        xp, row(g_mix[0]), w_in[0], inv, gq, gk,
        pos0=0, pos_stride=TM_PROJ, period=TM_PROJ, prompt=True, feat_block=TQ_PROMPT,
        cast_weights=(w_out[0], w_ffn_gate[0], w_ffn_up[0], w_ffn_down[0],
                      w_ple_gate[0], w_ple_proj[0]))
    tail_w_args = (w_out_b, row(g_ffn[0]), w_gate_b, w_up_b, w_down_b, row(g_ple[0]),
                   w_pg_b, w_pp_b)
    mixed_p, r_fin = _mix_prompt_call(qr, kr, vr, sg, qa_t, qb_t, ka, va_t, gro, vb_key)
    y_prompt = _tail_call(xp, mixed_p, p_prompt[0].reshape(seq, PLE_DIM), *tail_w_args)

    n_s = dec_batch * dec_seq
    xs = x_sample.reshape(n_s, D_MODEL)
    qr, kr, vr, sg, qa, qb, ka, va_t, k_new, v_new = _proj_call(
        xs, row(g_mix[0]), w_in[0], inv, gq, gk,
        pos0=PAST_LEN, pos_stride=0, period=dec_seq, prompt=False, feat_block=LANES)
    feat_major = lambda c: c.transpose(0, 2, 3, 1).reshape(dec_batch, ATT_W, cache_w)
    mixed_s, r_new = _mix_sample_call(
        qr, kr, vr, sg, qa, qb, ka, va_t,
        feat_major(cache_attn_k[0]), feat_major(cache_attn_v[0]),
        state_ret[0], gro, vb_query, dec_seq)
    y_sample = _tail_call(xs, mixed_s, p_sample[0].reshape(n_s, PLE_DIM), *tail_w_args)

    kv_p = lambda a: a.reshape(ATTN_HEADS, ATTN_DH, BAND_PAST).transpose(2, 0, 1).reshape(
        1, batch, BAND_PAST, ATTN_HEADS, ATTN_DH)
    kv_s = lambda a: a.reshape(1, dec_batch, dec_seq, ATTN_HEADS, ATTN_DH)
    return (y_prompt.reshape(batch, seq, D_MODEL),
            y_sample.reshape(dec_batch, dec_seq, D_MODEL),
            r_fin.reshape(1, batch, RET_HEADS, RET_DK, RET_DV),
            kv_p(k_tail), kv_p(v_tail),
            r_new.reshape(1, dec_batch, RET_HEADS, RET_DK, RET_DV),
            kv_s(k_new), kv_s(v_new))
```

```python
import functools
import math

import jax
import jax.numpy as jnp
from jax import lax
from jax.experimental import pallas as pl
from jax.experimental.pallas import tpu as pltpu

F32 = jnp.float32
BF16 = jnp.bfloat16

D_MODEL = 1024
CHUNK = 64
PAST_LEN = 2048
RET_HEADS = 4
RET_DK = 128
RET_DV = 128
ATTN_HEADS = 8
ATTN_DH = 64
BAND_PAST_CHUNKS = 8
BAND_PAST = BAND_PAST_CHUNKS * CHUNK
REL_CLIP = 256
D_FF = 2816
PLE_DIM = 256
ROPE_THETA = 10000.0
EPS = 1e-6
RET_W = RET_HEADS * RET_DK
ATT_W = ATTN_HEADS * ATTN_DH
IN_COLS = 4 * RET_W + 3 * ATT_W
LOG_DECAY = tuple(math.log1p(-(2.0 ** (-5 - h))) for h in range(RET_HEADS))
LOG2E = math.log2(math.e)
MASKED = -1e30

LANES = 128
SUBLANES = 8
BF16_ROWS = 16
BIAS_LANES = 1024
TM_PROJ = 512
PROJ_GROUP_ORDER = (4, 5, 6, 0, 1, 3, 2)
TM_DENSE = 512
TQ_PROMPT = 256
PROMPT_BLOCKS_PER_STEP = 2
DENOM_ROWS = BF16_ROWS
SCORE_LOOKAHEAD = 1
SAMPLE_LOOKAHEAD = 3
VMEM_LIMIT = 56 * 1024 * 1024

_NT = (((1,), (1,)), ((), ()))
_TN = (((0,), (0,)), ((), ()))


def _rms(x, g):
    return x * lax.rsqrt(jnp.mean(x * x, axis=-1, keepdims=True) + EPS) * g


def _const_spec(shape):
    zeros = (0,) * len(shape)
    return pl.BlockSpec(shape, lambda i: zeros, pipeline_mode=pl.Buffered(1))


def _proj_kernel(*refs, n_cast, pos0, pos_stride, period, prompt):
    x_ref, gmix_ref, inv_ref, gq_ref, gk_ref = refs[:5]
    cast_in = refs[5:5 + n_cast]
    w_hbm = refs[5 + n_cast]
    (qr_ref, kr_ref, vr_ref, sg_ref, qa_ref, qb_ref, ka_ref, vt_ref,
     ka32_ref, va32_ref) = refs[6 + n_cast:16 + n_cast]
    cast_out = refs[16 + n_cast:16 + 2 * n_cast]
    cos_scr, sin_scr, w_ref, stage_ref, sems = refs[16 + 2 * n_cast:]

    i = pl.program_id(0)
    tm = x_ref.shape[0]
    inv = inv_ref[...]
    group_cols = lambda g: slice(g * RET_W, (g + 1) * RET_W)
    copies = {g: pltpu.make_async_copy(w_hbm.at[:, group_cols(g)], stage_ref.at[n], sems.at[n])
              for n, g in enumerate(PROJ_GROUP_ORDER)}

    def body(first_step):
        base = jnp.full((1, LANES), pos0 + i * pos_stride, jnp.int32).astype(F32) * inv
        cb, sb = jnp.cos(base), jnp.sin(base)
        cos = cb * cos_scr[...] - sb * sin_scr[...]
        sin = sb * cos_scr[...] + cb * sin_scr[...]
        lane = lax.broadcasted_iota(jnp.int32, (tm, LANES), 1)
        first_half = lane < LANES // 2
        sin = jnp.where(first_half, -sin, sin)

        xn = _rms(x_ref[...], gmix_ref[...]).astype(BF16)
        for src_ref, dst_ref in zip(cast_in, cast_out):
            dst_ref[...] = src_ref[...].astype(BF16)

        def proj(group):
            cols = group_cols(group)
            if first_step:
                copies[group].wait()
                w_ref[:, cols] = stage_ref[PROJ_GROUP_ORDER.index(group)].astype(BF16)
            return jnp.dot(xn, w_ref[:, cols], preferred_element_type=F32)

        def rope_store(z, out_ref, scale):
            for h in range(RET_HEADS):
                sl = slice(h * RET_DK, (h + 1) * RET_DK)
                zh = z[:, sl]
                r = zh * cos + pltpu.roll(zh, RET_DK // 2, 1) * sin
                if scale is not None:
                    r = r * scale
                out_ref[:, sl] = r.astype(out_ref.dtype)

        def head_rms(z, g_ref):
            outs = []
            for j in range(ATT_W // LANES):
                sl = slice(j * LANES, (j + 1) * LANES)
                zj = z[:, sl]
                zz = zj * zj
                s0 = jnp.sum(jnp.where(first_half, zz, 0.0), axis=-1, keepdims=True)
                s1 = jnp.sum(jnp.where(first_half, 0.0, zz), axis=-1, keepdims=True)
                ms = jnp.where(first_half, s0, s1) * (1.0 / ATTN_DH)
                outs.append(zj * lax.rsqrt(ms + EPS) * g_ref[:, sl])
            return jnp.concatenate(outs, axis=1)

        def store_blocks(ref, val_t):
            width = ref.shape[2]
            for b in range(ref.shape[0]):
                ref[b] = val_t[:, b * width:(b + 1) * width]

        qs = head_rms(proj(4), gq_ref) * (ATTN_DH ** -0.5 * LOG2E)
        if prompt:
            qs = qs.T
            pair_first = lax.broadcasted_iota(jnp.int32, qs.shape, 0) % LANES < ATTN_DH
            store_blocks(qa_ref, jnp.where(pair_first, qs, 0.0).astype(BF16))
            store_blocks(qb_ref, jnp.where(pair_first, 0.0, qs).astype(BF16))
        else:
            pair_first = jnp.concatenate([first_half] * (ATT_W // LANES), axis=1)
            qa_ref[...] = jnp.where(pair_first, qs, 0.0).astype(BF16)
            qb_ref[...] = jnp.where(pair_first, 0.0, qs).astype(BF16)
        ka = head_rms(proj(5), gk_ref)
        ka_ref[...] = ka.astype(BF16)
        va = proj(6)
        va_t = va.T
        store_blocks(vt_ref, va_t.astype(BF16))
        rope_store(proj(0), qr_ref, None)
        rope_store(proj(1), kr_ref, RET_DK ** -0.5)
        sg_ref[...] = jax.nn.silu(proj(3)).astype(BF16)
        vr_ref[...] = proj(2).astype(BF16)

        if prompt:
            @pl.when(i == pl.num_programs(0) - 1)
            def _tail():
                ka32_ref[...] = ka[tm - BAND_PAST:].T
                va32_ref[...] = va_t[:, tm - BAND_PAST:]
        else:
            ka32_ref[...] = ka.reshape(tm, ATTN_HEADS, ATTN_DH)
            va32_ref[...] = va.reshape(tm, ATTN_HEADS, ATTN_DH)

    @pl.when(i == 0)
    def _first():
        for g in PROJ_GROUP_ORDER:
            copies[g].start()
        r = lax.broadcasted_iota(jnp.int32, (tm, LANES), 0) % period
        ang = r.astype(F32) * inv
        cos_scr[...] = jnp.cos(ang)
        sin_scr[...] = jnp.sin(ang)
        body(True)

    @pl.when(i > 0)
    def _rest():
        body(False)


def _cast_block_rows(rows, steps):
    block = BF16_ROWS
    while rows % block or block * steps < rows:
        block += BF16_ROWS
    return block


def _proj_call(x, g_mix, w_in, inv, gq, gk, *, pos0, pos_stride, period, prompt, feat_block,
               cast_weights=()):
    n = x.shape[0]
    tm = TM_PROJ
    cast_specs, cast_shapes = [], []
    for w in cast_weights:
        rows, cols = w.shape
        block = _cast_block_rows(rows, n // tm)
        last = rows // block - 1
        cast_specs.append(pl.BlockSpec((block, cols), lambda i, last=last: (jnp.minimum(i, last), 0)))
        cast_shapes.append(jax.ShapeDtypeStruct(w.shape, BF16))
    row = lambda w: pl.BlockSpec((tm, w), lambda i: (i, 0))
    col_spec = pl.BlockSpec((tm // feat_block, ATT_W, feat_block), lambda i: (i, 0, 0))
    bf = jax.ShapeDtypeStruct((n, RET_W), BF16)
    bf_t = jax.ShapeDtypeStruct((n // feat_block, ATT_W, feat_block), BF16)
    if prompt:
        assert tm >= BAND_PAST
        q_shape, q_spec = bf_t, col_spec
        kv32 = jax.ShapeDtypeStruct((ATT_W, BAND_PAST), F32)
        kv32_spec = pl.BlockSpec((ATT_W, BAND_PAST), lambda i: (0, 0))
    else:
        q_shape, q_spec = bf, row(ATT_W)
        kv32 = jax.ShapeDtypeStruct((n, ATTN_HEADS, ATTN_DH), F32)
        kv32_spec = pl.BlockSpec((tm, ATTN_HEADS, ATTN_DH), lambda i: (i, 0, 0))
    body = functools.partial(_proj_kernel, n_cast=len(cast_weights), pos0=pos0,
                             pos_stride=pos_stride, period=period, prompt=prompt)
    return pl.pallas_call(
        body,
        grid=(n // tm,),
        in_specs=[row(D_MODEL), _const_spec((1, D_MODEL)),
                  _const_spec((1, LANES)), _const_spec((1, ATT_W)), _const_spec((1, ATT_W))]
                 + cast_specs + [pl.BlockSpec(memory_space=pl.ANY)],
        out_specs=[row(RET_W)] * 4 + [q_spec] * 2 + [row(ATT_W), col_spec] + [kv32_spec] * 2
                  + cast_specs,
        out_shape=[bf] * 4 + [q_shape] * 2 + [bf, bf_t] + [kv32] * 2 + cast_shapes,
        scratch_shapes=[pltpu.VMEM((tm, LANES), F32), pltpu.VMEM((tm, LANES), F32),
                        pltpu.VMEM((D_MODEL, IN_COLS), BF16),
                        pltpu.VMEM((len(PROJ_GROUP_ORDER), D_MODEL, RET_W), F32),
                        pltpu.SemaphoreType.DMA((len(PROJ_GROUP_ORDER),))],
        compiler_params=pltpu.CompilerParams(dimension_semantics=("arbitrary",),
                                             vmem_limit_bytes=VMEM_LIMIT),
        name="proj",
    )(x, g_mix, inv, gq, gk, *cast_weights, w_in)


def _init_decay(dmat_scr, xi_scr, zeta_scr, t):
    n = lax.broadcasted_iota(jnp.int32, (t, t), 0)
    m = lax.broadcasted_iota(jnp.int32, (t, t), 1)
    diff = (n - m).astype(F32)
    pos = lax.broadcasted_iota(jnp.int32, (t, RET_DK), 0).astype(F32)
    for h in range(RET_HEADS):
        lg = LOG_DECAY[h]
        dmat_scr[h] = jnp.where(diff >= 0, jnp.exp(jnp.maximum(diff, 0.0) * lg), 0.0)
        sl = slice(h * RET_DK, (h + 1) * RET_DK)
        xi_scr[:, sl] = jnp.exp((pos + 1.0) * lg)
        zeta_scr[:, sl] = jnp.exp((t - 1.0 - pos) * lg)


def _toeplitz_rows(vb_ref, first_shift):
    r = lax.broadcasted_iota(jnp.int32, (ATTN_HEADS * SUBLANES, BIAS_LANES), 0)
    base = jnp.concatenate([jnp.broadcast_to(vb_ref[h:h + 1, :], (SUBLANES, BIAS_LANES))
                            for h in range(ATTN_HEADS)], axis=0) * LOG2E
    base = pltpu.roll(base, first_shift, 1)
    for shift in (1, 2, 4):
        base = jnp.where((r & shift) != 0, pltpu.roll(base, shift, 1), base)
    return [base[h * SUBLANES:(h + 1) * SUBLANES] for h in range(ATTN_HEADS)]


def _init_bias_keys(vb_ref, bias_scr):
    _, n_keys, t = bias_scr.shape
    lane = lax.broadcasted_iota(jnp.int32, (SUBLANES, t), 1)
    for h, base in enumerate(_toeplitz_rows(vb_ref, BIAS_LANES - n_keys)):
        for g in range(n_keys // SUBLANES):
            rows = (pltpu.roll(base, SUBLANES * g, 1) if g else base)[:, :t]
            kc = SUBLANES * g // CHUNK
            lo = max(kc - BAND_PAST_CHUNKS, 0) * CHUNK
            hi = (min(kc, t // CHUNK - 1) + 1) * CHUNK
            rows = jnp.where(jnp.logical_and(lane >= lo, lane < hi), rows, MASKED)
            bias_scr[h, g * SUBLANES:(g + 1) * SUBLANES, :] = rows


def _init_bias_queries(vb_ref, bias_scr, t, w):
    first = lax.broadcasted_iota(jnp.int32, (SUBLANES, LANES), 1) < t
    for h, base in enumerate(_toeplitz_rows(vb_ref, BIAS_LANES - REL_CLIP)):
        for g in range(t // SUBLANES):
            rows = pltpu.roll(base, SUBLANES * g, 1) if g else base
            shifted = pltpu.roll(base, SUBLANES * g + t, 1)
            r0 = (h % 2) * t + g * SUBLANES
            bias_scr[h // 2, r0:r0 + SUBLANES, :w] = rows[:, :w]
            bias_scr[h // 2, r0:r0 + SUBLANES, w:w + LANES] = jnp.where(
                first, rows[:, w:w + LANES], MASKED)
            bias_scr[h // 2, r0:r0 + SUBLANES, w + LANES:] = jnp.where(
                first, MASKED, shifted[:, w:w + LANES])


def _retention(q, k, v, r, dmat, xi, zeta, decay_t):
    s = lax.dot_general(q, k, _NT, preferred_element_type=F32) * dmat
    inner = jnp.dot(s.astype(BF16), v, preferred_element_type=F32)
    cross = jnp.dot(q, r.astype(BF16), preferred_element_type=F32) * xi
    kz = (k.astype(F32) * zeta).astype(BF16)
    r_new = r * decay_t + lax.dot_general(kz, v, _TN, preferred_element_type=F32)
    return inner + cross, r_new


def _column_max(s):
    rows = s.shape[0]
    while rows % (2 * SUBLANES) == 0:
        rows //= 2
        s = jnp.maximum(s[:rows], s[rows:])
    return jnp.max(s, axis=0, keepdims=True)


def _mix_prompt_kernel(qr_ref, kr_ref, vr_ref, sg_ref, qa_ref, qb_ref,
                       k0_ref, k1_ref, k2_ref, k3_ref, v0_ref, v1_ref, v2_ref, v3_ref,
                       gro_ref, vb_ref,
                       mixed_ref, rfin_ref,
                       r_scr, bias_scr, dmat_scr, xi_scr, zeta_scr):
    i = pl.program_id(0)
    t = TQ_PROMPT
    n_keys = bias_scr.shape[1]
    k_refs = (k0_ref, k1_ref, k2_ref, k3_ref)
    v_refs = (v0_ref, v1_ref, v2_ref, v3_ref)

    @pl.when(i == 0)
    def _init():
        r_scr[...] = jnp.zeros_like(r_scr)
        _init_decay(dmat_scr, xi_scr, zeta_scr, t)
        _init_bias_keys(vb_ref, bias_scr)

    ones = jnp.ones((DENOM_ROWS, n_keys), BF16)

    def retention(b, h):
        rows = slice(b * t, (b + 1) * t)
        sl = slice(h * RET_DK, (h + 1) * RET_DK)
        o, r_new = _retention(qr_ref[rows, sl], kr_ref[rows, sl], vr_ref[rows, sl], r_scr[h],
                              dmat_scr[h], xi_scr[:, sl], zeta_scr[:, sl],
                              math.exp(t * LOG_DECAY[h]))
        r_scr[h] = r_new
        mixed_ref[rows, sl] = (sg_ref[rows, sl].astype(F32) * _rms(o, gro_ref[:, sl])).astype(BF16)

    def scores(b, j, start_masks):
        sl = slice(j * LANES, (j + 1) * LANES)
        k = jnp.concatenate([k_ref[:, sl] for k_ref in k_refs[b:b + 3]], axis=0)
        q = jnp.concatenate([qa_ref[b, sl, :], qb_ref[b, sl, :]], axis=1)
        bias = jnp.concatenate([bias_scr[2 * j], bias_scr[2 * j + 1]], axis=1)
        s = jnp.dot(k, q, preferred_element_type=F32) + bias
        if start_masks is not None:
            s = s + jnp.concatenate([start_masks[b], start_masks[b]], axis=1)
        return s

    def finish(b, j, s):
        rows = slice(b * t, (b + 1) * t)
        e = jnp.exp2(s - _column_max(s)).astype(BF16)
        halves = []
        for half in range(2):
            feat = slice((2 * j + half) * ATTN_DH, (2 * j + half + 1) * ATTN_DH)
            v_t = jnp.concatenate([v_ref[0, feat, :] for v_ref in v_refs[b:b + 3]], axis=1)
            out = jnp.dot(jnp.concatenate([v_t, ones], axis=0), e[:, half * t:(half + 1) * t],
                          preferred_element_type=F32)
            halves.append(out[:ATTN_DH] / out[ATTN_DH:ATTN_DH + 1])
        o_t = jnp.concatenate(halves, axis=0)
        mixed_ref[rows, RET_W + j * LANES:RET_W + (j + 1) * LANES] = o_t.T.astype(BF16)

    def mix(start_masks):
        items = [(b, j) for b in range(PROMPT_BLOCKS_PER_STEP) for j in range(ATTN_HEADS // 2)]
        pending = [scores(b, j, start_masks) for b, j in items[:SCORE_LOOKAHEAD]]
        for n, (b, j) in enumerate(items):
            retention(b, j)
            if n + SCORE_LOOKAHEAD < len(items):
                pending.append(scores(*items[n + SCORE_LOOKAHEAD], start_masks))
            finish(b, j, pending.pop(0))

    @pl.when(i == 0)
    def _start():
        blk = lax.broadcasted_iota(jnp.int32, (n_keys, t), 0) // t
        mix([jnp.where(blk + b >= 2, 0.0, MASKED) for b in range(PROMPT_BLOCKS_PER_STEP)])

    @pl.when(i > 0)
    def _steady():
        mix(None)

    @pl.when(i == pl.num_programs(0) - 1)
    def _fin():
        rfin_ref[...] = r_scr[...]


def _mix_prompt_call(qr, kr, vr, sg, qa_t, qb_t, ka, va_t, gro, vb):
    n = qr.shape[0]
    t, nb = TQ_PROMPT, PROMPT_BLOCKS_PER_STEP
    assert nb == 2
    last = n // t - 1
    key_blk = lambda i, j: jnp.clip(nb * i - 2 + j, 0, last)
    row = pl.BlockSpec((nb * t, RET_W), lambda i: (i, 0))
    col = pl.BlockSpec((nb, ATT_W, t), lambda i: (i, 0, 0))
    k_specs = [pl.BlockSpec((t, ATT_W), lambda i, j=j: (key_blk(i, j), 0)) for j in range(nb + 2)]
    v_specs = [pl.BlockSpec((1, ATT_W, t), lambda i, j=j: (key_blk(i, j), 0, 0))
               for j in range(nb + 2)]
    state = (RET_HEADS, RET_DK, RET_DV)
    return pl.pallas_call(
        _mix_prompt_kernel,
        grid=(n // (nb * t),),
        in_specs=[row] * 4 + [col] * 2 + k_specs + v_specs
                 + [_const_spec((1, RET_W)), _const_spec((ATTN_HEADS, BIAS_LANES))],
        out_specs=[pl.BlockSpec((nb * t, D_MODEL), lambda i: (i, 0)),
                   pl.BlockSpec(state, lambda i: (0, 0, 0))],
        out_shape=[jax.ShapeDtypeStruct((n, D_MODEL), BF16), jax.ShapeDtypeStruct(state, F32)],
        scratch_shapes=[pltpu.VMEM(state, F32),
                        pltpu.VMEM((ATTN_HEADS, 3 * t, t), F32),
                        pltpu.VMEM((RET_HEADS, t, t), F32),
                        pltpu.VMEM((t, RET_W), F32),
                        pltpu.VMEM((t, RET_W), F32)],
        compiler_params=pltpu.CompilerParams(dimension_semantics=("arbitrary",),
                                             vmem_limit_bytes=VMEM_LIMIT),
        name="mix_prompt",
    )(qr, kr, vr, sg, qa_t, qb_t, *([ka] * (nb + 2)), *([va_t] * (nb + 2)), gro, vb)


def _mix_sample_kernel(qr_ref, kr_ref, vr_ref, sg_ref, qa_ref, qb_ref, ka_ref, vt_ref, ck_ref, cv_ref,
                       r0_ref, gro_ref, vb_ref,
                       mixed_ref, rnew_ref,
                       bias_scr, dmat_scr, xi_scr, zeta_scr, *, t):
    streams, _, w = ck_ref.shape
    assert streams * t == LANES

    @pl.when(pl.program_id(0) == 0)
    def _init():
        _init_decay(dmat_scr, xi_scr, zeta_scr, t)
        _init_bias_queries(vb_ref, bias_scr, t, w)

    ones = jnp.ones((DENOM_ROWS, w), BF16)
    own_head = ((lax.broadcasted_iota(jnp.int32, (LANES, 2 * t), 0) < ATTN_DH)
                == (lax.broadcasted_iota(jnp.int32, (LANES, 2 * t), 1) < t))

    def retention(b, h):
        rows = slice(b * t, (b + 1) * t)
        sl = slice(h * RET_DK, (h + 1) * RET_DK)
        o, r_new = _retention(qr_ref[rows, sl], kr_ref[rows, sl], vr_ref[rows, sl], r0_ref[b, h],
                              dmat_scr[h], xi_scr[:, sl], zeta_scr[:, sl],
                              math.exp(t * LOG_DECAY[h]))
        rnew_ref[b, h] = r_new
        mixed_ref[rows, sl] = (sg_ref[rows, sl].astype(F32) * _rms(o, gro_ref[:, sl])).astype(BF16)

    def scores(b, j):
        rows = slice(b * t, (b + 1) * t)
        sl = slice(j * LANES, (j + 1) * LANES)
        q2 = jnp.concatenate([qa_ref[rows, sl], qb_ref[rows, sl]], axis=0)
        s_c = (jnp.dot(q2, ck_ref[b, sl, :].astype(BF16), preferred_element_type=F32)
               + bias_scr[j, :, :w])
        s_n = (lax.dot_general(q2, ka_ref[:, sl], _NT, preferred_element_type=F32)
               + bias_scr[j, :, w + b * LANES:w + (b + 1) * LANES])
        return s_c, s_n

    def finish(b, j, s_c, s_n):
        rows = slice(b * t, (b + 1) * t)
        sl = slice(j * LANES, (j + 1) * LANES)
        m = jnp.maximum(jnp.max(s_c, axis=-1, keepdims=True),
                        jnp.max(s_n, axis=-1, keepdims=True))
        e_c = jnp.exp2(s_c - m).astype(BF16)
        e_n = jnp.exp2(s_n - m).astype(BF16)
        lhs_c = jnp.concatenate([cv_ref[b, sl, :].astype(BF16), ones], axis=0)
        lhs_n = jnp.concatenate([vt_ref[0, sl, :], ones[:, :LANES]], axis=0)
        out = (lax.dot_general(lhs_c, e_c, _NT, preferred_element_type=F32)
               + lax.dot_general(lhs_n, e_n, _NT, preferred_element_type=F32))
        x = out[:LANES] / out[LANES:LANES + 1]
        x_t = jnp.where(own_head, x, 0.0).T
        mixed_ref[rows, RET_W + j * LANES:RET_W + (j + 1) * LANES] = (
            x_t[:t] + x_t[t:]).astype(BF16)

    items = [(b, j) for b in range(streams) for j in range(ATTN_HEADS // 2)]
    ret_items = [(b, h) for b in range(streams) for h in range(RET_HEADS)]
    pending = [scores(*it) for it in items[:SAMPLE_LOOKAHEAD]]
    for n, (b, j) in enumerate(items):
        if n < len(ret_items):
            retention(*ret_items[n])
        if n + SAMPLE_LOOKAHEAD < len(items):
            pending.append(scores(*items[n + SAMPLE_LOOKAHEAD]))
        finish(b, j, *pending.pop(0))


def _mix_sample_call(qr, kr, vr, sg, qa, qb, ka, va_t, ck, cv, r0, gro, vb, t):
    n = qr.shape[0]
    nb, _, w = ck.shape
    g = LANES // t
    assert nb % g == 0
    row = pl.BlockSpec((g * t, RET_W), lambda b: (b, 0))
    col = pl.BlockSpec((1, ATT_W, g * t), lambda b: (b, 0, 0))
    cache = pl.BlockSpec((g, ATT_W, w), lambda b: (b, 0, 0))
    state = pl.BlockSpec((g, RET_HEADS, RET_DK, RET_DV), lambda b: (b, 0, 0, 0))
    return pl.pallas_call(
        functools.partial(_mix_sample_kernel, t=t),
        grid=(nb // g,),
        in_specs=[row] * 7 + [col, cache, cache, state,
                              _const_spec((1, RET_W)), _const_spec((ATTN_HEADS, BIAS_LANES))],
        out_specs=[pl.BlockSpec((g * t, D_MODEL), lambda b: (b, 0)), state],
        out_shape=[jax.ShapeDtypeStruct((n, D_MODEL), BF16),
                   jax.ShapeDtypeStruct((nb, RET_HEADS, RET_DK, RET_DV), F32)],
        scratch_shapes=[pltpu.VMEM((ATTN_HEADS // 2, 2 * t, w + g * LANES), F32),
                        pltpu.VMEM((RET_HEADS, t, t), F32),
                        pltpu.VMEM((t, RET_W), F32),
                        pltpu.VMEM((t, RET_W), F32)],
        compiler_params=pltpu.CompilerParams(dimension_semantics=("arbitrary",),
                                             vmem_limit_bytes=VMEM_LIMIT),
        name="mix_sample",
    )(qr, kr, vr, sg, qa, qb, ka, va_t, ck, cv, r0, gro, vb)


FF_SPLITS = (0, 1536, D_FF)


def _tail_kernel(x_ref, mixed_ref, p_ref, gffn_ref, gple_ref,
                 wout_hbm, wg_hbm, wu_hbm, wd_hbm, wpp_hbm, wpg_hbm,
                 out_ref,
                 wout_ref, wg_ref, wu_ref, wd_ref, wpp_ref, wpg_ref, sems):
    i = pl.program_id(0)
    tm = x_ref.shape[0]
    halves = (slice(0, tm // 2), slice(tm // 2, tm))
    ranges = list(zip(FF_SPLITS[:-1], FF_SPLITS[1:]))
    dot = functools.partial(jnp.dot, preferred_element_type=F32)
    pieces = [(wout_hbm, wout_ref)]
    for c0, c1 in ranges:
        pieces += [(wg_hbm.at[:, c0:c1], wg_ref.at[:, c0:c1]),
                   (wu_hbm.at[:, c0:c1], wu_ref.at[:, c0:c1]),
                   (wd_hbm.at[c0:c1, :], wd_ref.at[c0:c1, :])]
    pieces += [(wpp_hbm, wpp_ref), (wpg_hbm, wpg_ref)]
    copies = [pltpu.make_async_copy(src, dst, sems.at[n]) for n, (src, dst) in enumerate(pieces)]

    def body(first_step):
        def arrived(n):
            if first_step:
                copies[n].wait()

        arrived(0)
        h = [x_ref[r, :] + dot(mixed_ref[r, :], wout_ref[...]) for r in halves]
        u = [_rms(hr, gffn_ref[...]).astype(BF16) for hr in h]

        c0, c1 = ranges[0]
        arrived(1)
        gate = jnp.concatenate([dot(ur, wg_ref[:, c0:c1]) for ur in u], axis=0)
        u = jnp.concatenate(u, axis=0)
        arrived(2)
        act = (jax.nn.silu(gate) * dot(u, wu_ref[:, c0:c1])).astype(BF16)
        arrived(3)
        ffn = dot(act, wd_ref[c0:c1, :])
        for k, (c0, c1) in enumerate(ranges[1:], start=1):
            arrived(1 + 3 * k)
            arrived(2 + 3 * k)
            act = (jax.nn.silu(dot(u, wg_ref[:, c0:c1])) * dot(u, wu_ref[:, c0:c1])).astype(BF16)
            arrived(3 + 3 * k)
            if (c0, c1) != ranges[-1]:
                ffn = ffn + dot(act, wd_ref[c0:c1, :])

        c0, c1 = ranges[-1]
        h = [hr + ffn[r, :] + dot(act[r, :], wd_ref[c0:c1, :]) for hr, r in zip(h, halves)]
        arrived(1 + 3 * len(ranges))
        ple = dot(p_ref[...].astype(BF16), wpp_ref[...])
        arrived(2 + 3 * len(ranges))
        for hr, r in zip(h, halves):
            gate = jax.nn.sigmoid(dot(_rms(hr, gple_ref[...]).astype(BF16), wpg_ref[...]))
            out_ref[r, :] = hr + ple[r, :] * gate

    @pl.when(i == 0)
    def _first():
        for copy in copies:
            copy.start()
        body(True)

    @pl.when(i > 0)
    def _rest():
        body(False)


def _tail_call(x, mixed, p, w_out, g_ffn, w_gate, w_up, w_down, g_ple, w_pg, w_pp):
    n = x.shape[0]
    tm = TM_DENSE
    row = lambda w: pl.BlockSpec((tm, w), lambda i: (i, 0))
    weights = (w_out, w_gate, w_up, w_down, w_pp, w_pg)
    return pl.pallas_call(
        _tail_kernel,
        grid=(n // tm,),
        in_specs=[row(D_MODEL), row(D_MODEL), row(PLE_DIM),
                  _const_spec((1, D_MODEL)), _const_spec((1, D_MODEL))]
                 + [pl.BlockSpec(memory_space=pl.ANY)] * len(weights),
        out_specs=row(D_MODEL),
        out_shape=jax.ShapeDtypeStruct((n, D_MODEL), F32),
        scratch_shapes=[pltpu.VMEM(w.shape, BF16) for w in weights]
                       + [pltpu.SemaphoreType.DMA((3 * len(FF_SPLITS),))],
        compiler_params=pltpu.CompilerParams(dimension_semantics=("arbitrary",),
                                             vmem_limit_bytes=VMEM_LIMIT),
        name="tail",
    )(x, mixed, p, g_ffn, g_ple, *weights)


def _bias_rows(rel_bias):
    edge = jnp.broadcast_to(rel_bias[:, 2 * REL_CLIP:], (ATTN_HEADS, BIAS_LANES - 2 * REL_CLIP))
    by_query = jnp.concatenate([edge, rel_bias[:, 2 * REL_CLIP:0:-1]], axis=1)
    by_key = jnp.concatenate([rel_bias[:, :2 * REL_CLIP], edge], axis=1)
    return by_query, by_key


def kernel(x_prompt, x_sample, cache_attn_k, cache_attn_v, state_ret, p_prompt, p_sample,
           g_mix, w_in, g_ret_out, g_q_attn, g_k_attn, rel_bias, w_out, g_ffn,
           w_ffn_gate, w_ffn_up, w_ffn_down, g_ple, w_ple_gate, w_ple_proj):
    depth = w_in.shape[0]
    assert depth == 1, "single-layer trunk"
    batch, seq, _ = x_prompt.shape
    dec_batch, dec_seq, _ = x_sample.shape
    cache_w = cache_attn_k.shape[2]
    assert batch == 1 and seq % TM_PROJ == 0 and dec_seq == CHUNK and cache_w == BAND_PAST
    assert BAND_PAST <= seq, "the returned prompt keys are the tail of the last projection tile"

    row = lambda g: g.reshape(1, -1)
    gq = row(jnp.tile(g_q_attn[0], ATTN_HEADS))
    gk = row(jnp.tile(g_k_attn[0], ATTN_HEADS))
    gro = row(g_ret_out[0])
    vb_query, vb_key = _bias_rows(rel_bias[0])
    half = RET_DK // 2
    inv = ROPE_THETA ** (-jnp.arange(half, dtype=F32) / half)
    inv = row(jnp.concatenate([inv, inv]))

    xp = x_prompt.reshape(seq, D_MODEL)
    (qr, kr, vr, sg, qa_t, qb_t, ka, va_t, k_tail, v_tail,
     w_out_b, w_gate_b, w_up_b, w_down_b, w_pg_b, w_pp_b) = _proj_call(
        xp, row(g_mix[0]), w_in[0], inv, gq, gk,
        pos0=0, pos_stride=TM_PROJ, period=TM_PROJ, prompt=True, feat_block=TQ_PROMPT,
        cast_weights=(w_out[0], w_ffn_gate[0], w_ffn_up[0], w_ffn_down[0],
                      w_ple_gate[0], w_ple_proj[0]))
    tail_w_args = (w_out_b, row(g_ffn[0]), w_gate_b, w_up_b, w_down_b, row(g_ple[0]),
                   w_pg_b, w_pp_b)
    mixed_p, r_fin = _mix_prompt_call(qr, kr, vr, sg, qa_t, qb_t, ka, va_t, gro, vb_key)
    y_prompt = _tail_call(xp, mixed_p, p_prompt[0].reshape(seq, PLE_DIM), *tail_w_args)

    n_s = dec_batch * dec_seq
    xs = x_sample.reshape(n_s, D_MODEL)
    qr, kr, vr, sg, qa, qb, ka, va_t, k_new, v_new = _proj_call(
        xs, row(g_mix[0]), w_in[0], inv, gq, gk,
        pos0=PAST_LEN, pos_stride=0, period=dec_seq, prompt=False, feat_block=LANES)
    feat_major = lambda c: c.transpose(0, 2, 3, 1).reshape(dec_batch, ATT_W, cache_w)
    mixed_s, r_new = _mix_sample_call(
        qr, kr, vr, sg, qa, qb, ka, va_t,
        feat_major(cache_attn_k[0]), feat_major(cache_attn_v[0]),
        state_ret[0], gro, vb_query, dec_seq)
    y_sample = _tail_call(xs, mixed_s, p_sample[0].reshape(n_s, PLE_DIM), *tail_w_args)

    kv_p = lambda a: a.reshape(ATTN_HEADS, ATTN_DH, BAND_PAST).transpose(2, 0, 1).reshape(
        1, batch, BAND_PAST, ATTN_HEADS, ATTN_DH)
    kv_s = lambda a: a.reshape(1, dec_batch, dec_seq, ATTN_HEADS, ATTN_DH)
    return (y_prompt.reshape(batch, seq, D_MODEL),
            y_sample.reshape(dec_batch, dec_seq, D_MODEL),
            r_fin.reshape(1, batch, RET_HEADS, RET_DK, RET_DV),
            kv_p(k_tail), kv_p(v_tail),
            r_new.reshape(1, dec_batch, RET_HEADS, RET_DK, RET_DV),
            kv_s(k_new), kv_s(v_new))
```

```python
import functools
import math

import jax
import jax.numpy as jnp
from jax import lax
from jax.experimental import pallas as pl
from jax.experimental.pallas import tpu as pltpu

F32 = jnp.float32
BF16 = jnp.bfloat16

D_MODEL = 1024
CHUNK = 64
PAST_LEN = 2048
RET_HEADS = 4
RET_DK = 128
RET_DV = 128
ATTN_HEADS = 8
ATTN_DH = 64
BAND_PAST_CHUNKS = 8
BAND_PAST = BAND_PAST_CHUNKS * CHUNK
REL_CLIP = 256
D_FF = 2816
PLE_DIM = 256
ROPE_THETA = 10000.0
EPS = 1e-6
RET_W = RET_HEADS * RET_DK
ATT_W = ATTN_HEADS * ATTN_DH
IN_COLS = 4 * RET_W + 3 * ATT_W
LOG_DECAY = tuple(math.log1p(-(2.0 ** (-5 - h))) for h in range(RET_HEADS))
LOG2E = math.log2(math.e)
MASKED = -1e30

LANES = 128
SUBLANES = 8
BF16_ROWS = 16
BIAS_LANES = 1024
TM_PROJ = 512
PROJ_GROUP_ORDER = (4, 5, 6, 0, 1, 3, 2)
TM_DENSE = 512
TQ_PROMPT = 256
PROMPT_BLOCKS_PER_STEP = 2
DENOM_ROWS = BF16_ROWS
SCORE_LOOKAHEAD = 1
SAMPLE_LOOKAHEAD = 3
VMEM_LIMIT = 56 * 1024 * 1024

_NT = (((1,), (1,)), ((), ()))
_TN = (((0,), (0,)), ((), ()))


def _rms(x, g):
    return x * lax.rsqrt(jnp.mean(x * x, axis=-1, keepdims=True) + EPS) * g


def _const_spec(shape):
    zeros = (0,) * len(shape)
    return pl.BlockSpec(shape, lambda i: zeros, pipeline_mode=pl.Buffered(1))


def _proj_kernel(*refs, n_cast, pos0, pos_stride, period, prompt):
    x_ref, gmix_ref, inv_ref, gq_ref, gk_ref = refs[:5]
    cast_in = refs[5:5 + n_cast]
    w_hbm = refs[5 + n_cast]
    (qr_ref, kr_ref, vr_ref, sg_ref, qa_ref, qb_ref, ka_ref, vt_ref,
     ka32_ref, va32_ref) = refs[6 + n_cast:16 + n_cast]
    cast_out = refs[16 + n_cast:16 + 2 * n_cast]
    cos_scr, sin_scr, w_ref, stage_ref, sems = refs[16 + 2 * n_cast:]

    i = pl.program_id(0)
    tm = x_ref.shape[0]
    inv = inv_ref[...]
    group_cols = lambda g: slice(g * RET_W, (g + 1) * RET_W)
    copies = {g: pltpu.make_async_copy(w_hbm.at[:, group_cols(g)], stage_ref.at[n], sems.at[n])
              for n, g in enumerate(PROJ_GROUP_ORDER)}

    def body(first_step):
        base = jnp.full((1, LANES), pos0 + i * pos_stride, jnp.int32).astype(F32) * inv
        cb, sb = jnp.cos(base), jnp.sin(base)
        cos = cb * cos_scr[...] - sb * sin_scr[...]
        sin = sb * cos_scr[...] + cb * sin_scr[...]
        lane = lax.broadcasted_iota(jnp.int32, (tm, LANES), 1)
        first_half = lane < LANES // 2
        sin = jnp.where(first_half, -sin, sin)

        xn = _rms(x_ref[...], gmix_ref[...]).astype(BF16)
        for src_ref, dst_ref in zip(cast_in, cast_out):
            dst_ref[...] = src_ref[...].astype(BF16)

        def proj(group):
            cols = group_cols(group)
            if first_step:
                copies[group].wait()
                w_ref[:, cols] = stage_ref[PROJ_GROUP_ORDER.index(group)].astype(BF16)
            return jnp.dot(xn, w_ref[:, cols], preferred_element_type=F32)

        def rope_store(z, out_ref, scale):
            for h in range(RET_HEADS):
                sl = slice(h * RET_DK, (h + 1) * RET_DK)
                zh = z[:, sl]
                r = zh * cos + pltpu.roll(zh, RET_DK // 2, 1) * sin
                if scale is not None:
                    r = r * scale
                out_ref[:, sl] = r.astype(out_ref.dtype)

        def head_rms(z, g_ref):
            outs = []
            for j in range(ATT_W // LANES):
                sl = slice(j * LANES, (j + 1) * LANES)
                zj = z[:, sl]
                zz = zj * zj
                s0 = jnp.sum(jnp.where(first_half, zz, 0.0), axis=-1, keepdims=True)
                s1 = jnp.sum(jnp.where(first_half, 0.0, zz), axis=-1, keepdims=True)
                ms = jnp.where(first_half, s0, s1) * (1.0 / ATTN_DH)
                outs.append(zj * lax.rsqrt(ms + EPS) * g_ref[:, sl])
            return jnp.concatenate(outs, axis=1)

        def store_blocks(ref, val_t):
            width = ref.shape[2]
            for b in range(ref.shape[0]):
                ref[b] = val_t[:, b * width:(b + 1) * width]

        qs = head_rms(proj(4), gq_ref) * (ATTN_DH ** -0.5 * LOG2E)
        if prompt:
            qs = qs.T
            pair_first = lax.broadcasted_iota(jnp.int32, qs.shape, 0) % LANES < ATTN_DH
            store_blocks(qa_ref, jnp.where(pair_first, qs, 0.0).astype(BF16))
            store_blocks(qb_ref, jnp.where(pair_first, 0.0, qs).astype(BF16))
        else:
            pair_first = jnp.concatenate([first_half] * (ATT_W // LANES), axis=1)
            qa_ref[...] = jnp.where(pair_first, qs, 0.0).astype(BF16)
            qb_ref[...] = jnp.where(pair_first, 0.0, qs).astype(BF16)
        ka = head_rms(proj(5), gk_ref)
        ka_ref[...] = ka.astype(BF16)
        va = proj(6)
        va_t = va.T
        store_blocks(vt_ref, va_t.astype(BF16))
        rope_store(proj(0), qr_ref, None)
        rope_store(proj(1), kr_ref, RET_DK ** -0.5)
        sg_ref[...] = jax.nn.silu(proj(3)).astype(BF16)
        vr_ref[...] = proj(2).astype(BF16)

        if prompt:
            @pl.when(i == pl.num_programs(0) - 1)
            def _tail():
                ka32_ref[...] = ka[tm - BAND_PAST:].T
                va32_ref[...] = va_t[:, tm - BAND_PAST:]
        else:
            ka32_ref[...] = ka.reshape(tm, ATTN_HEADS, ATTN_DH)
            va32_ref[...] = va.reshape(tm, ATTN_HEADS, ATTN_DH)

    @pl.when(i == 0)
    def _first():
        for n, g in enumerate(PROJ_GROUP_ORDER):
            copies[g].start(priority=n % 2)
        r = lax.broadcasted_iota(jnp.int32, (tm, LANES), 0) % period
        ang = r.astype(F32) * inv
        cos_scr[...] = jnp.cos(ang)
        sin_scr[...] = jnp.sin(ang)
        body(True)

    @pl.when(i > 0)
    def _rest():
        body(False)


def _cast_block_rows(rows, steps):
    block = BF16_ROWS
    while rows % block or block * steps < rows:
        block += BF16_ROWS
    return block


def _proj_call(x, g_mix, w_in, inv, gq, gk, *, pos0, pos_stride, period, prompt, feat_block,
               cast_weights=()):
    n = x.shape[0]
    tm = TM_PROJ
    cast_specs, cast_shapes = [], []
    for w in cast_weights:
        rows, cols = w.shape
        block = _cast_block_rows(rows, n // tm)
        last = rows // block - 1
        cast_specs.append(pl.BlockSpec((block, cols), lambda i, last=last: (jnp.minimum(i, last), 0)))
        cast_shapes.append(jax.ShapeDtypeStruct(w.shape, BF16))
    row = lambda w: pl.BlockSpec((tm, w), lambda i: (i, 0))
    col_spec = pl.BlockSpec((tm // feat_block, ATT_W, feat_block), lambda i: (i, 0, 0))
    bf = jax.ShapeDtypeStruct((n, RET_W), BF16)
    bf_t = jax.ShapeDtypeStruct((n // feat_block, ATT_W, feat_block), BF16)
    if prompt:
        assert tm >= BAND_PAST
        q_shape, q_spec = bf_t, col_spec
        kv32 = jax.ShapeDtypeStruct((ATT_W, BAND_PAST), F32)
        kv32_spec = pl.BlockSpec((ATT_W, BAND_PAST), lambda i: (0, 0))
    else:
        q_shape, q_spec = bf, row(ATT_W)
        kv32 = jax.ShapeDtypeStruct((n, ATTN_HEADS, ATTN_DH), F32)
        kv32_spec = pl.BlockSpec((tm, ATTN_HEADS, ATTN_DH), lambda i: (i, 0, 0))
    body = functools.partial(_proj_kernel, n_cast=len(cast_weights), pos0=pos0,
                             pos_stride=pos_stride, period=period, prompt=prompt)
    return pl.pallas_call(
        body,
        grid=(n // tm,),
        in_specs=[row(D_MODEL), _const_spec((1, D_MODEL)),
                  _const_spec((1, LANES)), _const_spec((1, ATT_W)), _const_spec((1, ATT_W))]
                 + cast_specs + [pl.BlockSpec(memory_space=pl.ANY)],
        out_specs=[row(RET_W)] * 4 + [q_spec] * 2 + [row(ATT_W), col_spec] + [kv32_spec] * 2
                  + cast_specs,
        out_shape=[bf] * 4 + [q_shape] * 2 + [bf, bf_t] + [kv32] * 2 + cast_shapes,
        scratch_shapes=[pltpu.VMEM((tm, LANES), F32), pltpu.VMEM((tm, LANES), F32),
                        pltpu.VMEM((D_MODEL, IN_COLS), BF16),
                        pltpu.VMEM((len(PROJ_GROUP_ORDER), D_MODEL, RET_W), F32),
                        pltpu.SemaphoreType.DMA((len(PROJ_GROUP_ORDER),))],
        compiler_params=pltpu.CompilerParams(dimension_semantics=("arbitrary",),
                                             vmem_limit_bytes=VMEM_LIMIT),
        name="proj",
    )(x, g_mix, inv, gq, gk, *cast_weights, w_in)


def _init_decay(dmat_scr, xi_scr, zeta_scr, t):
    n = lax.broadcasted_iota(jnp.int32, (t, t), 0)
    m = lax.broadcasted_iota(jnp.int32, (t, t), 1)
    diff = (n - m).astype(F32)
    pos = lax.broadcasted_iota(jnp.int32, (t, RET_DK), 0).astype(F32)
    for h in range(RET_HEADS):
        lg = LOG_DECAY[h]
        dmat_scr[h] = jnp.where(diff >= 0, jnp.exp(jnp.maximum(diff, 0.0) * lg), 0.0)
        sl = slice(h * RET_DK, (h + 1) * RET_DK)
        xi_scr[:, sl] = jnp.exp((pos + 1.0) * lg)
        zeta_scr[:, sl] = jnp.exp((t - 1.0 - pos) * lg)


def _toeplitz_rows(vb_ref, first_shift):
    r = lax.broadcasted_iota(jnp.int32, (ATTN_HEADS * SUBLANES, BIAS_LANES), 0)
    base = jnp.concatenate([jnp.broadcast_to(vb_ref[h:h + 1, :], (SUBLANES, BIAS_LANES))
                            for h in range(ATTN_HEADS)], axis=0) * LOG2E
    base = pltpu.roll(base, first_shift, 1)
    for shift in (1, 2, 4):
        base = jnp.where((r & shift) != 0, pltpu.roll(base, shift, 1), base)
    return [base[h * SUBLANES:(h + 1) * SUBLANES] for h in range(ATTN_HEADS)]


def _init_bias_keys(vb_ref, bias_scr):
    _, n_keys, t = bias_scr.shape
    lane = lax.broadcasted_iota(jnp.int32, (SUBLANES, t), 1)
    for h, base in enumerate(_toeplitz_rows(vb_ref, BIAS_LANES - n_keys)):
        for g in range(n_keys // SUBLANES):
            rows = (pltpu.roll(base, SUBLANES * g, 1) if g else base)[:, :t]
            kc = SUBLANES * g // CHUNK
            lo = max(kc - BAND_PAST_CHUNKS, 0) * CHUNK
            hi = (min(kc, t // CHUNK - 1) + 1) * CHUNK
            rows = jnp.where(jnp.logical_and(lane >= lo, lane < hi), rows, MASKED)
            bias_scr[h, g * SUBLANES:(g + 1) * SUBLANES, :] = rows


def _init_bias_queries(vb_ref, bias_scr, t, w):
    first = lax.broadcasted_iota(jnp.int32, (SUBLANES, LANES), 1) < t
    for h, base in enumerate(_toeplitz_rows(vb_ref, BIAS_LANES - REL_CLIP)):
        for g in range(t // SUBLANES):
            rows = pltpu.roll(base, SUBLANES * g, 1) if g else base
            shifted = pltpu.roll(base, SUBLANES * g + t, 1)
            r0 = (h % 2) * t + g * SUBLANES
            bias_scr[h // 2, r0:r0 + SUBLANES, :w] = rows[:, :w]
            bias_scr[h // 2, r0:r0 + SUBLANES, w:w + LANES] = jnp.where(
                first, rows[:, w:w + LANES], MASKED)
            bias_scr[h // 2, r0:r0 + SUBLANES, w + LANES:] = jnp.where(
                first, MASKED, shifted[:, w:w + LANES])


def _retention(q, k, v, r, dmat, xi, zeta, decay_t):
    s = lax.dot_general(q, k, _NT, preferred_element_type=F32) * dmat
    inner = jnp.dot(s.astype(BF16), v, preferred_element_type=F32)
    cross = jnp.dot(q, r.astype(BF16), preferred_element_type=F32) * xi
    kz = (k.astype(F32) * zeta).astype(BF16)
    r_new = r * decay_t + lax.dot_general(kz, v, _TN, preferred_element_type=F32)
    return inner + cross, r_new


def _column_max(s):
    rows = s.shape[0]
    while rows % (2 * SUBLANES) == 0:
        rows //= 2
        s = jnp.maximum(s[:rows], s[rows:])
    return jnp.max(s, axis=0, keepdims=True)


def _mix_prompt_kernel(qr_ref, kr_ref, vr_ref, sg_ref, qa_ref, qb_ref,
                       k0_ref, k1_ref, k2_ref, k3_ref, v0_ref, v1_ref, v2_ref, v3_ref,
                       gro_ref, vb_ref,
                       mixed_ref, rfin_ref,
                       r_scr, bias_scr, dmat_scr, xi_scr, zeta_scr):
    i = pl.program_id(0)
    t = TQ_PROMPT
    n_keys = bias_scr.shape[1]
    k_refs = (k0_ref, k1_ref, k2_ref, k3_ref)
    v_refs = (v0_ref, v1_ref, v2_ref, v3_ref)

    @pl.when(i == 0)
    def _init():
        r_scr[...] = jnp.zeros_like(r_scr)
        _init_decay(dmat_scr, xi_scr, zeta_scr, t)
        _init_bias_keys(vb_ref, bias_scr)

    ones = jnp.ones((DENOM_ROWS, n_keys), BF16)

    def retention(b, h):
        rows = slice(b * t, (b + 1) * t)
        sl = slice(h * RET_DK, (h + 1) * RET_DK)
        o, r_new = _retention(qr_ref[rows, sl], kr_ref[rows, sl], vr_ref[rows, sl], r_scr[h],
                              dmat_scr[h], xi_scr[:, sl], zeta_scr[:, sl],
                              math.exp(t * LOG_DECAY[h]))
        r_scr[h] = r_new
        mixed_ref[rows, sl] = (sg_ref[rows, sl].astype(F32) * _rms(o, gro_ref[:, sl])).astype(BF16)

    def scores(b, j, start_masks):
        sl = slice(j * LANES, (j + 1) * LANES)
        k = jnp.concatenate([k_ref[:, sl] for k_ref in k_refs[b:b + 3]], axis=0)
        q = jnp.concatenate([qa_ref[b, sl, :], qb_ref[b, sl, :]], axis=1)
        bias = jnp.concatenate([bias_scr[2 * j], bias_scr[2 * j + 1]], axis=1)
        s = jnp.dot(k, q, preferred_element_type=F32) + bias
        if start_masks is not None:
            s = s + jnp.concatenate([start_masks[b], start_masks[b]], axis=1)
        return s

    def finish(b, j, s):
        rows = slice(b * t, (b + 1) * t)
        e = jnp.exp2(s - _column_max(s)).astype(BF16)
        halves = []
        for half in range(2):
            feat = slice((2 * j + half) * ATTN_DH, (2 * j + half + 1) * ATTN_DH)
            v_t = jnp.concatenate([v_ref[0, feat, :] for v_ref in v_refs[b:b + 3]], axis=1)
            out = jnp.dot(jnp.concatenate([v_t, ones], axis=0), e[:, half * t:(half + 1) * t],
                          preferred_element_type=F32)
            halves.append(out[:ATTN_DH] / out[ATTN_DH:ATTN_DH + 1])
        o_t = jnp.concatenate(halves, axis=0)
        mixed_ref[rows, RET_W + j * LANES:RET_W + (j + 1) * LANES] = o_t.T.astype(BF16)

    def mix(start_masks):
        items = [(b, j) for b in range(PROMPT_BLOCKS_PER_STEP) for j in range(ATTN_HEADS // 2)]
        pending = [scores(b, j, start_masks) for b, j in items[:SCORE_LOOKAHEAD]]
        for n, (b, j) in enumerate(items):
            retention(b, j)
            if n + SCORE_LOOKAHEAD < len(items):
                pending.append(scores(*items[n + SCORE_LOOKAHEAD], start_masks))
            finish(b, j, pending.pop(0))

    @pl.when(i == 0)
    def _start():
        blk = lax.broadcasted_iota(jnp.int32, (n_keys, t), 0) // t
        mix([jnp.where(blk + b >= 2, 0.0, MASKED) for b in range(PROMPT_BLOCKS_PER_STEP)])

    @pl.when(i > 0)
    def _steady():
        mix(None)

    @pl.when(i == pl.num_programs(0) - 1)
    def _fin():
        rfin_ref[...] = r_scr[...]


def _mix_prompt_call(qr, kr, vr, sg, qa_t, qb_t, ka, va_t, gro, vb):
    n = qr.shape[0]
    t, nb = TQ_PROMPT, PROMPT_BLOCKS_PER_STEP
    assert nb == 2
    last = n // t - 1
    key_blk = lambda i, j: jnp.clip(nb * i - 2 + j, 0, last)
    row = pl.BlockSpec((nb * t, RET_W), lambda i: (i, 0))
    col = pl.BlockSpec((nb, ATT_W, t), lambda i: (i, 0, 0))
    k_specs = [pl.BlockSpec((t, ATT_W), lambda i, j=j: (key_blk(i, j), 0)) for j in range(nb + 2)]
    v_specs = [pl.BlockSpec((1, ATT_W, t), lambda i, j=j: (key_blk(i, j), 0, 0))
               for j in range(nb + 2)]
    state = (RET_HEADS, RET_DK, RET_DV)
    return pl.pallas_call(
        _mix_prompt_kernel,
        grid=(n // (nb * t),),
        in_specs=[row] * 4 + [col] * 2 + k_specs + v_specs
                 + [_const_spec((1, RET_W)), _const_spec((ATTN_HEADS, BIAS_LANES))],
        out_specs=[pl.BlockSpec((nb * t, D_MODEL), lambda i: (i, 0)),
                   pl.BlockSpec(state, lambda i: (0, 0, 0))],
        out_shape=[jax.ShapeDtypeStruct((n, D_MODEL), BF16), jax.ShapeDtypeStruct(state, F32)],
        scratch_shapes=[pltpu.VMEM(state, F32),
                        pltpu.VMEM((ATTN_HEADS, 3 * t, t), F32),
                        pltpu.VMEM((RET_HEADS, t, t), F32),
                        pltpu.VMEM((t, RET_W), F32),
                        pltpu.VMEM((t, RET_W), F32)],
        compiler_params=pltpu.CompilerParams(dimension_semantics=("arbitrary",),
                                             vmem_limit_bytes=VMEM_LIMIT),
        name="mix_prompt",
    )(qr, kr, vr, sg, qa_t, qb_t, *([ka] * (nb + 2)), *([va_t] * (nb + 2)), gro, vb)


def _mix_sample_kernel(qr_ref, kr_ref, vr_ref, sg_ref, qa_ref, qb_ref, ka_ref, vt_ref, ck_ref, cv_ref,
                       r0_ref, gro_ref, vb_ref,
                       mixed_ref, rnew_ref,
                       bias_scr, dmat_scr, xi_scr, zeta_scr, *, t):
    streams, _, w = ck_ref.shape
    assert streams * t == LANES

    @pl.when(pl.program_id(0) == 0)
    def _init():
        _init_decay(dmat_scr, xi_scr, zeta_scr, t)
        _init_bias_queries(vb_ref, bias_scr, t, w)

    ones = jnp.ones((DENOM_ROWS, w), BF16)
    own_head = ((lax.broadcasted_iota(jnp.int32, (LANES, 2 * t), 0) < ATTN_DH)
                == (lax.broadcasted_iota(jnp.int32, (LANES, 2 * t), 1) < t))

    def retention(b, h):
        rows = slice(b * t, (b + 1) * t)
        sl = slice(h * RET_DK, (h + 1) * RET_DK)
        o, r_new = _retention(qr_ref[rows, sl], kr_ref[rows, sl], vr_ref[rows, sl], r0_ref[b, h],
                              dmat_scr[h], xi_scr[:, sl], zeta_scr[:, sl],
                              math.exp(t * LOG_DECAY[h]))
        rnew_ref[b, h] = r_new
        mixed_ref[rows, sl] = (sg_ref[rows, sl].astype(F32) * _rms(o, gro_ref[:, sl])).astype(BF16)

    def scores(b, j):
        rows = slice(b * t, (b + 1) * t)
        sl = slice(j * LANES, (j + 1) * LANES)
        q2 = jnp.concatenate([qa_ref[rows, sl], qb_ref[rows, sl]], axis=0)
        s_c = (jnp.dot(q2, ck_ref[b, sl, :].astype(BF16), preferred_element_type=F32)
               + bias_scr[j, :, :w])
        s_n = (lax.dot_general(q2, ka_ref[:, sl], _NT, preferred_element_type=F32)
               + bias_scr[j, :, w + b * LANES:w + (b + 1) * LANES])
        return s_c, s_n

    def finish(b, j, s_c, s_n):
        rows = slice(b * t, (b + 1) * t)
        sl = slice(j * LANES, (j + 1) * LANES)
        m = jnp.maximum(jnp.max(s_c, axis=-1, keepdims=True),
                        jnp.max(s_n, axis=-1, keepdims=True))
        e_c = jnp.exp2(s_c - m).astype(BF16)
        e_n = jnp.exp2(s_n - m).astype(BF16)
        lhs_c = jnp.concatenate([cv_ref[b, sl, :].astype(BF16), ones], axis=0)
        lhs_n = jnp.concatenate([vt_ref[0, sl, :], ones[:, :LANES]], axis=0)
        out = (lax.dot_general(lhs_c, e_c, _NT, preferred_element_type=F32)
               + lax.dot_general(lhs_n, e_n, _NT, preferred_element_type=F32))
        x = out[:LANES] / out[LANES:LANES + 1]
        x_t = jnp.where(own_head, x, 0.0).T
        mixed_ref[rows, RET_W + j * LANES:RET_W + (j + 1) * LANES] = (
            x_t[:t] + x_t[t:]).astype(BF16)

    items = [(b, j) for b in range(streams) for j in range(ATTN_HEADS // 2)]
    ret_items = [(b, h) for b in range(streams) for h in range(RET_HEADS)]
    pending = [scores(*it) for it in items[:SAMPLE_LOOKAHEAD]]
    for n, (b, j) in enumerate(items):
        if n < len(ret_items):
            retention(*ret_items[n])
        if n + SAMPLE_LOOKAHEAD < len(items):
            pending.append(scores(*items[n + SAMPLE_LOOKAHEAD]))
        finish(b, j, *pending.pop(0))


def _mix_sample_call(qr, kr, vr, sg, qa, qb, ka, va_t, ck, cv, r0, gro, vb, t):
    n = qr.shape[0]
    nb, _, w = ck.shape
    g = LANES // t
    assert nb % g == 0
    row = pl.BlockSpec((g * t, RET_W), lambda b: (b, 0))
    col = pl.BlockSpec((1, ATT_W, g * t), lambda b: (b, 0, 0))
    cache = pl.BlockSpec((g, ATT_W, w), lambda b: (b, 0, 0))
    state = pl.BlockSpec((g, RET_HEADS, RET_DK, RET_DV), lambda b: (b, 0, 0, 0))
    return pl.pallas_call(
        functools.partial(_mix_sample_kernel, t=t),
        grid=(nb // g,),
        in_specs=[row] * 7 + [col, cache, cache, state,
                              _const_spec((1, RET_W)), _const_spec((ATTN_HEADS, BIAS_LANES))],
        out_specs=[pl.BlockSpec((g * t, D_MODEL), lambda b: (b, 0)), state],
        out_shape=[jax.ShapeDtypeStruct((n, D_MODEL), BF16),
                   jax.ShapeDtypeStruct((nb, RET_HEADS, RET_DK, RET_DV), F32)],
        scratch_shapes=[pltpu.VMEM((ATTN_HEADS // 2, 2 * t, w + g * LANES), F32),
                        pltpu.VMEM((RET_HEADS, t, t), F32),
                        pltpu.VMEM((t, RET_W), F32),
                        pltpu.VMEM((t, RET_W), F32)],
        compiler_params=pltpu.CompilerParams(dimension_semantics=("arbitrary",),
                                             vmem_limit_bytes=VMEM_LIMIT),
        name="mix_sample",
    )(qr, kr, vr, sg, qa, qb, ka, va_t, ck, cv, r0, gro, vb)


FF_SPLITS = (0, 1536, D_FF)


def _tail_kernel(x_ref, mixed_ref, p_ref, gffn_ref, gple_ref,
                 wout_hbm, wg_hbm, wu_hbm, wd_hbm, wpp_hbm, wpg_hbm,
                 out_ref,
                 wout_ref, wg_ref, wu_ref, wd_ref, wpp_ref, wpg_ref, sems):
    i = pl.program_id(0)
    tm = x_ref.shape[0]
    halves = (slice(0, tm // 2), slice(tm // 2, tm))
    ranges = list(zip(FF_SPLITS[:-1], FF_SPLITS[1:]))
    dot = functools.partial(jnp.dot, preferred_element_type=F32)
    pieces = [(wout_hbm, wout_ref)]
    for c0, c1 in ranges:
        pieces += [(wg_hbm.at[:, c0:c1], wg_ref.at[:, c0:c1]),
                   (wu_hbm.at[:, c0:c1], wu_ref.at[:, c0:c1]),
                   (wd_hbm.at[c0:c1, :], wd_ref.at[c0:c1, :])]
    pieces += [(wpp_hbm, wpp_ref), (wpg_hbm, wpg_ref)]
    copies = [pltpu.make_async_copy(src, dst, sems.at[n]) for n, (src, dst) in enumerate(pieces)]

    def body(first_step):
        def arrived(n):
            if first_step:
                copies[n].wait()

        arrived(0)
        h = [x_ref[r, :] + dot(mixed_ref[r, :], wout_ref[...]) for r in halves]
        u = [_rms(hr, gffn_ref[...]).astype(BF16) for hr in h]

        c0, c1 = ranges[0]
        arrived(1)
        gate = jnp.concatenate([dot(ur, wg_ref[:, c0:c1]) for ur in u], axis=0)
        u = jnp.concatenate(u, axis=0)
        arrived(2)
        act = (jax.nn.silu(gate) * dot(u, wu_ref[:, c0:c1])).astype(BF16)
        arrived(3)
        ffn = dot(act, wd_ref[c0:c1, :])
        for k, (c0, c1) in enumerate(ranges[1:], start=1):
            arrived(1 + 3 * k)
            arrived(2 + 3 * k)
            act = (jax.nn.silu(dot(u, wg_ref[:, c0:c1])) * dot(u, wu_ref[:, c0:c1])).astype(BF16)
            arrived(3 + 3 * k)
            if (c0, c1) != ranges[-1]:
                ffn = ffn + dot(act, wd_ref[c0:c1, :])

        c0, c1 = ranges[-1]
        h = [hr + ffn[r, :] + dot(act[r, :], wd_ref[c0:c1, :]) for hr, r in zip(h, halves)]
        arrived(1 + 3 * len(ranges))
        ple = dot(p_ref[...].astype(BF16), wpp_ref[...])
        arrived(2 + 3 * len(ranges))
        for hr, r in zip(h, halves):
            gate = jax.nn.sigmoid(dot(_rms(hr, gple_ref[...]).astype(BF16), wpg_ref[...]))
            out_ref[r, :] = hr + ple[r, :] * gate

    @pl.when(i == 0)
    def _first():
        for n, copy in enumerate(copies):
            copy.start(priority=n % 2)
        body(True)

    @pl.when(i > 0)
    def _rest():
        body(False)


def _tail_call(x, mixed, p, w_out, g_ffn, w_gate, w_up, w_down, g_ple, w_pg, w_pp):
    n = x.shape[0]
    tm = TM_DENSE
    row = lambda w: pl.BlockSpec((tm, w), lambda i: (i, 0))
    weights = (w_out, w_gate, w_up, w_down, w_pp, w_pg)
    return pl.pallas_call(
        _tail_kernel,
        grid=(n // tm,),
        in_specs=[row(D_MODEL), row(D_MODEL), row(PLE_DIM),
                  _const_spec((1, D_MODEL)), _const_spec((1, D_MODEL))]
                 + [pl.BlockSpec(memory_space=pl.ANY)] * len(weights),
        out_specs=row(D_MODEL),
        out_shape=jax.ShapeDtypeStruct((n, D_MODEL), F32),
        scratch_shapes=[pltpu.VMEM(w.shape, BF16) for w in weights]
                       + [pltpu.SemaphoreType.DMA((3 * len(FF_SPLITS),))],
        compiler_params=pltpu.CompilerParams(dimension_semantics=("arbitrary",),
                                             vmem_limit_bytes=VMEM_LIMIT),
        name="tail",
    )(x, mixed, p, g_ffn, g_ple, *weights)


def _bias_rows(rel_bias):
    edge = jnp.broadcast_to(rel_bias[:, 2 * REL_CLIP:], (ATTN_HEADS, BIAS_LANES - 2 * REL_CLIP))
    by_query = jnp.concatenate([edge, rel_bias[:, 2 * REL_CLIP:0:-1]], axis=1)
    by_key = jnp.concatenate([rel_bias[:, :2 * REL_CLIP], edge], axis=1)
    return by_query, by_key


def kernel(x_prompt, x_sample, cache_attn_k, cache_attn_v, state_ret, p_prompt, p_sample,
           g_mix, w_in, g_ret_out, g_q_attn, g_k_attn, rel_bias, w_out, g_ffn,
           w_ffn_gate, w_ffn_up, w_ffn_down, g_ple, w_ple_gate, w_ple_proj):
    depth = w_in.shape[0]
    assert depth == 1, "single-layer trunk"
    batch, seq, _ = x_prompt.shape
    dec_batch, dec_seq, _ = x_sample.shape
    cache_w = cache_attn_k.shape[2]
    assert batch == 1 and seq % TM_PROJ == 0 and dec_seq == CHUNK and cache_w == BAND_PAST
    assert BAND_PAST <= seq, "the returned prompt keys are the tail of the last projection tile"

    row = lambda g: g.reshape(1, -1)
    gq = row(jnp.tile(g_q_attn[0], ATTN_HEADS))
    gk = row(jnp.tile(g_k_attn[0], ATTN_HEADS))
    gro = row(g_ret_out[0])
    vb_query, vb_key = _bias_rows(rel_bias[0])
    half = RET_DK // 2
    inv = ROPE_THETA ** (-jnp.arange(half, dtype=F32) / half)
    inv = row(jnp.concatenate([inv, inv]))

    xp = x_prompt.reshape(seq, D_MODEL)
    (qr, kr, vr, sg, qa_t, qb_t, ka, va_t, k_tail, v_tail,
     w_out_b, w_gate_b, w_up_b, w_down_b, w_pg_b, w_pp_b) = _proj_call(
        xp, row(g_mix[0]), w_in[0], inv, gq, gk,
        pos0=0, pos_stride=TM_PROJ, period=TM_PROJ, prompt=True, feat_block=TQ_PROMPT,
        cast_weights=(w_out[0], w_ffn_gate[0], w_ffn_up[0], w_ffn_down[0],
                      w_ple_gate[0], w_ple_proj[0]))
    tail_w_args = (w_out_b, row(g_ffn[0]), w_gate_b, w_up_b, w_down_b, row(g_ple[0]),
                   w_pg_b, w_pp_b)
    mixed_p, r_fin = _mix_prompt_call(qr, kr, vr, sg, qa_t, qb_t, ka, va_t, gro, vb_key)
    y_prompt = _tail_call(xp, mixed_p, p_prompt[0].reshape(seq, PLE_DIM), *tail_w_args)

    n_s = dec_batch * dec_seq
    xs = x_sample.reshape(n_s, D_MODEL)
    qr, kr, vr, sg, qa, qb, ka, va_t, k_new, v_new = _proj_call(
        xs, row(g_mix[0]), w_in[0], inv, gq, gk,
        pos0=PAST_LEN, pos_stride=0, period=dec_seq, prompt=False, feat_block=LANES)
    feat_major = lambda c: c.transpose(0, 2, 3, 1).reshape(dec_batch, ATT_W, cache_w)
    mixed_s, r_new = _mix_sample_call(
        qr, kr, vr, sg, qa, qb, ka, va_t,
        feat_major(cache_attn_k[0]), feat_major(cache_attn_v[0]),
        state_ret[0], gro, vb_query, dec_seq)
    y_sample = _tail_call(xs, mixed_s, p_sample[0].reshape(n_s, PLE_DIM), *tail_w_args)

    kv_p = lambda a: a.reshape(ATTN_HEADS, ATTN_DH, BAND_PAST).transpose(2, 0, 1).reshape(
        1, batch, BAND_PAST, ATTN_HEADS, ATTN_DH)
    kv_s = lambda a: a.reshape(1, dec_batch, dec_seq, ATTN_HEADS, ATTN_DH)
    return (y_prompt.reshape(batch, seq, D_MODEL),
            y_sample.reshape(dec_batch, dec_seq, D_MODEL),
            r_fin.reshape(1, batch, RET_HEADS, RET_DK, RET_DV),
            kv_p(k_tail), kv_p(v_tail),
            r_new.reshape(1, dec_batch, RET_HEADS, RET_DK, RET_DV),
            kv_s(k_new), kv_s(v_new))
```
